```python
import math
import jax, jax.numpy as jnp
from jax import lax
import numpy as np

D_MODEL = 2048
BATCH = 4
SEQ = 4096
DEPTH = 4
DEC_BATCH = 8
DEC_SEQ = 4096
PAST_LEN = 128

N_META = 16
MLA_HEADS = 16
MLA_Q_LORA = 512
MLA_KV_LORA = 512
MLA_NOPE = 128
MLA_ROPE = 64
MLA_V = 128
ROPE_THETA = 10000.0
SWA_HEADS = 32
SWA_KV_HEADS = 4
SWA_GROUP = SWA_HEADS // SWA_KV_HEADS
SWA_HEAD_DIM = 64
WINDOW = 128
BLOCK = 128
N_BUCKETS = 32
MAX_DISTANCE = 128
D_FF = 4 * D_MODEL
N_MIXERS = 2
N_MLA_LAYERS = (DEPTH + 1) // 2
N_SWA_LAYERS = DEPTH // 2
EPS = 1e-6

kernel_name = "hybrid_mla_swa_sink_encoder"


def _rmsnorm(x, g):
    xf = x.astype(jnp.float32)
    r = lax.rsqrt(jnp.mean(xf * xf, axis=-1, keepdims=True) + EPS)
    return (xf * r * g.astype(jnp.float32)).astype(x.dtype)


def _rope(x, pos):
    half = x.shape[-1] // 2
    inv = ROPE_THETA ** (-(jnp.arange(half, dtype=jnp.float32) / half))
    ang = pos.astype(jnp.float32)[:, None] * inv[None, :]
    cos = jnp.cos(ang)[None, :, None, :]
    sin = jnp.sin(ang)[None, :, None, :]
    x1 = x[..., :half].astype(jnp.float32)
    x2 = x[..., half:].astype(jnp.float32)
    return jnp.concatenate([x1 * cos - x2 * sin, x2 * cos + x1 * sin], axis=-1).astype(x.dtype)


def _t5_bucket(rel):
    nb = N_BUCKETS // 2
    max_exact = nb // 2
    ret = jnp.where(rel > 0, nb, 0)
    n = jnp.abs(rel)
    nf = jnp.maximum(n, 1).astype(jnp.float32)
    large = max_exact + (jnp.log(nf / max_exact) / math.log(MAX_DISTANCE / max_exact)
                         * (nb - max_exact)).astype(jnp.int32)
    large = jnp.minimum(large, nb - 1)
    return ret + jnp.where(n < max_exact, n, large)


def _swa_attend(q, k, v, qpos, kpos, valid, rel_bias, sink):
    lq, lk = q.shape[1], k.shape[1]
    s = jnp.einsum('bqkgd,bskd->bkgqs', q, k).astype(jnp.float32) * (SWA_HEAD_DIM ** -0.5)
    bias = rel_bias[_t5_bucket(kpos[None, :] - qpos[:, None])].astype(jnp.float32)
    bias = bias.reshape(lq, lk, SWA_KV_HEADS, SWA_GROUP).transpose(2, 3, 0, 1)
    s = jnp.where(valid[None, None, None], s + bias[None], -jnp.inf)
    sk = sink.astype(jnp.float32).reshape(SWA_KV_HEADS, SWA_GROUP)[None, :, :, None, None]
    m = jnp.maximum(jnp.max(s, axis=-1, keepdims=True), sk)
    e = jnp.exp(s - m)
    p = e / (jnp.sum(e, axis=-1, keepdims=True) + jnp.exp(sk - m))
    return jnp.einsum('bkgqs,bskd->bqkgd', p.astype(v.dtype), v)


def _swa_mixer(x, w_qkv, w_o, sink, rel_bias):
    B, L, _ = x.shape
    S = L - N_META
    nb = S // BLOCK
    dq = SWA_HEADS * SWA_HEAD_DIM
    dk = SWA_KV_HEADS * SWA_HEAD_DIM
    qkv = x @ w_qkv
    q = qkv[..., :dq].reshape(B, L, SWA_KV_HEADS, SWA_GROUP, SWA_HEAD_DIM)
    k = qkv[..., dq:dq + dk].reshape(B, L, SWA_KV_HEADS, SWA_HEAD_DIM)
    v = qkv[..., dq + dk:].reshape(B, L, SWA_KV_HEADS, SWA_HEAD_DIM)
    n_lead = N_META + BLOCK
    pos_lead = jnp.arange(n_lead)
    qpos_m = jnp.arange(N_META)
    valid_m = (pos_lead[None, :] < N_META) | (jnp.abs(pos_lead[None, :] - qpos_m[:, None]) <= WINDOW)
    o_meta = _swa_attend(q[:, :N_META], k[:, :n_lead], v[:, :n_lead], qpos_m, pos_lead, valid_m,
                         rel_bias, sink)
    pad = ((0, 0), (BLOCK, BLOCK), (0, 0), (0, 0))
    k_pad = jnp.pad(k[:, N_META:], pad)
    v_pad = jnp.pad(v[:, N_META:], pad)
    k_meta, v_meta = k[:, :N_META], v[:, :N_META]
    q_blocks = jnp.moveaxis(
        q[:, N_META:].reshape(B, nb, BLOCK, SWA_KV_HEADS, SWA_GROUP, SWA_HEAD_DIM), 1, 0)
    offs = jnp.arange(BLOCK)
    band = jnp.arange(3 * BLOCK)
    meta_pos = jnp.arange(N_META)

    def step(args):
        qb, b = args
        start = b * BLOCK
        kb = jnp.concatenate([k_meta, lax.dynamic_slice_in_dim(k_pad, start, 3 * BLOCK, axis=1)], axis=1)
        vb = jnp.concatenate([v_meta, lax.dynamic_slice_in_dim(v_pad, start, 3 * BLOCK, axis=1)], axis=1)
        qpos = N_META + start + offs
        ridx = start - BLOCK + band
        kpos = jnp.concatenate([meta_pos, N_META + ridx])
        in_band = ((ridx[None, :] >= 0) & (ridx[None, :] < S)
                   & (jnp.abs(N_META + ridx[None, :] - qpos[:, None]) <= WINDOW))
        valid = jnp.concatenate([jnp.ones((BLOCK, N_META), dtype=bool), in_band], axis=1)
        return _swa_attend(qb, kb, vb, qpos, kpos, valid, rel_bias, sink)

    o_real = lax.map(step, (q_blocks, jnp.arange(nb)))
    o_real = jnp.moveaxis(o_real, 0, 1).reshape(B, S, dq)
    o = jnp.concatenate([o_meta.reshape(B, N_META, dq), o_real], axis=1)
    return o @ w_o


def _mla_attend(q_nope, q_rope, k_nope, k_rope, v):
    s = (jnp.einsum('bqhd,bkhd->bhqk', q_nope, k_nope)
         + jnp.einsum('bqhr,bkr->bhqk', q_rope, k_rope)).astype(jnp.float32)
    p = jax.nn.softmax(s * ((MLA_NOPE + MLA_ROPE) ** -0.5), axis=-1)
    return jnp.einsum('bhqk,bkhd->bqhd', p.astype(v.dtype), v)


def _mla_mixer(x, w_dq, q_norm, w_uq, w_dkv, kv_norm, w_ukv, w_o):
    B, L, _ = x.shape
    S = L - N_META
    nb = S // BLOCK
    pos = jnp.arange(L)
    c_q = _rmsnorm(x @ w_dq, q_norm)
    q = (c_q @ w_uq).reshape(B, L, MLA_HEADS, MLA_NOPE + MLA_ROPE)
    q_nope = q[..., :MLA_NOPE]
    q_rope = _rope(q[..., MLA_NOPE:], pos)
    kv_a = x @ w_dkv
    c_kv = _rmsnorm(kv_a[..., :MLA_KV_LORA], kv_norm)
    k_rope = _rope(kv_a[..., None, MLA_KV_LORA:], pos)[:, :, 0]
    kv = (c_kv @ w_ukv).reshape(B, L, MLA_HEADS, MLA_NOPE + MLA_V)
    k_nope = kv[..., :MLA_NOPE]
    v = kv[..., MLA_NOPE:]
    o_meta = _mla_attend(q_nope[:, :N_META], q_rope[:, :N_META], k_nope, k_rope, v)

    def blocks(t):
        return jnp.moveaxis(t[:, N_META:].reshape(B, nb, BLOCK, *t.shape[2:]), 1, 0)

    o_real = lax.map(lambda a: _mla_attend(a[0], a[1], k_nope, k_rope, v),
                     (blocks(q_nope), blocks(q_rope)))
    o_real = jnp.moveaxis(o_real, 0, 1).reshape(B, S, MLA_HEADS * MLA_V)
    o = jnp.concatenate([o_meta.reshape(B, N_META, MLA_HEADS * MLA_V), o_real], axis=1)
    return o @ w_o


def _mlp(x, w_in, w_out):
    h = jax.nn.relu(x @ w_in)
    return (h * h) @ w_out


def _trunk(x, meta_tokens, rel_bias, mla_w_dq, mla_q_norm, mla_w_uq, mla_w_dkv, mla_kv_norm,
           mla_w_ukv, mla_w_o, swa_w_qkv, swa_w_o, swa_sink, mlp_w_in, mlp_w_out,
           norm_mix_pre, norm_mix_post, norm_mlp_pre, norm_mlp_post):
    B = x.shape[0]
    meta = jnp.broadcast_to(meta_tokens[None].astype(x.dtype), (B, N_META, x.shape[-1]))
    h = jnp.concatenate([meta, x], axis=1)
    for i in range(DEPTH):
        j = i // N_MIXERS
        hn = _rmsnorm(h, norm_mix_pre[i])
        if i % N_MIXERS == 0:
            m = _mla_mixer(hn, mla_w_dq[j], mla_q_norm[j], mla_w_uq[j], mla_w_dkv[j], mla_kv_norm[j],
                           mla_w_ukv[j], mla_w_o[j])
        else:
            m = _swa_mixer(hn, swa_w_qkv[j], swa_w_o[j], swa_sink[j], rel_bias)
        h = h + _rmsnorm(m, norm_mix_post[i])
        f = _mlp(_rmsnorm(h, norm_mlp_pre[i]), mlp_w_in[i], mlp_w_out[i])
        h = h + _rmsnorm(f, norm_mlp_post[i])
    return h[:, N_META:]


def _dense(k, shape, fan_in):
    return jax.random.normal(k, shape, dtype=jnp.float32) * (fan_in ** -0.5)


def _gain(k, shape):
    return 1.0 + 0.05 * jax.random.normal(k, shape, dtype=jnp.float32)


def setup_inputs(seed: int = 0) -> dict:
    key = jax.random.key(seed)
    ks = jax.random.split(key, 20)
    nm, ns = N_MLA_LAYERS, N_SWA_LAYERS
    dqkv = (SWA_HEADS + 2 * SWA_KV_HEADS) * SWA_HEAD_DIM
    return {
        "x_prompt": jax.random.normal(ks[0], (BATCH, SEQ, D_MODEL), dtype=jnp.float32),
        "x_sample": jax.random.normal(ks[1], (DEC_BATCH, DEC_SEQ, D_MODEL), dtype=jnp.float32),
        "meta_tokens": jax.random.normal(ks[2], (N_META, D_MODEL), dtype=jnp.float32),
        "rel_bias": 0.5 * jax.random.normal(ks[3], (N_BUCKETS, SWA_HEADS), dtype=jnp.float32),
        "mla_w_dq": _dense(ks[4], (nm, D_MODEL, MLA_Q_LORA), D_MODEL),
        "mla_q_norm": _gain(ks[5], (nm, MLA_Q_LORA)),
        "mla_w_uq": _dense(ks[6], (nm, MLA_Q_LORA, MLA_HEADS * (MLA_NOPE + MLA_ROPE)), MLA_Q_LORA),
        "mla_w_dkv": _dense(ks[7], (nm, D_MODEL, MLA_KV_LORA + MLA_ROPE), D_MODEL),
        "mla_kv_norm": _gain(ks[8], (nm, MLA_KV_LORA)),
        "mla_w_ukv": _dense(ks[9], (nm, MLA_KV_LORA, MLA_HEADS * (MLA_NOPE + MLA_V)), MLA_KV_LORA),
        "mla_w_o": _dense(ks[10], (nm, MLA_HEADS * MLA_V, D_MODEL), MLA_HEADS * MLA_V),
        "swa_w_qkv": _dense(ks[11], (ns, D_MODEL, dqkv), D_MODEL),
        "swa_w_o": _dense(ks[12], (ns, SWA_HEADS * SWA_HEAD_DIM, D_MODEL), SWA_HEADS * SWA_HEAD_DIM),
        "swa_sink": 0.5 * jax.random.normal(ks[13], (ns, SWA_HEADS), dtype=jnp.float32),
        "mlp_w_in": _dense(ks[14], (DEPTH, D_MODEL, D_FF), D_MODEL),
        "mlp_w_out": _dense(ks[15], (DEPTH, D_FF, D_MODEL), D_FF),
        "norm_mix_pre": _gain(ks[16], (DEPTH, D_MODEL)),
        "norm_mix_post": _gain(ks[17], (DEPTH, D_MODEL)),
        "norm_mlp_pre": _gain(ks[18], (DEPTH, D_MODEL)),
        "norm_mlp_post": _gain(ks[19], (DEPTH, D_MODEL)),
    }


def reference(x_prompt, x_sample, meta_tokens, rel_bias, mla_w_dq, mla_q_norm, mla_w_uq, mla_w_dkv,
              mla_kv_norm, mla_w_ukv, mla_w_o, swa_w_qkv, swa_w_o, swa_sink, mlp_w_in, mlp_w_out,
              norm_mix_pre, norm_mix_post, norm_mlp_pre, norm_mlp_post):
    weights = (meta_tokens, rel_bias, mla_w_dq, mla_q_norm, mla_w_uq, mla_w_dkv, mla_kv_norm,
               mla_w_ukv, mla_w_o, swa_w_qkv, swa_w_o, swa_sink, mlp_w_in, mlp_w_out,
               norm_mix_pre, norm_mix_post, norm_mlp_pre, norm_mlp_post)
    y_prompt = _trunk(x_prompt, *weights)
    y_sample = _trunk(x_sample, *weights)
    return (y_prompt, y_sample)
```

```python
import functools
import math

import jax
import jax.numpy as jnp
import numpy as np
from jax import lax
from jax.experimental import pallas as pl
from jax.experimental.pallas import tpu as pltpu

D_MODEL = 2048
DEPTH = 4
N_META = 16
MLA_HEADS = 16
MLA_Q_LORA = 512
MLA_KV_LORA = 512
MLA_NOPE = 128
MLA_ROPE = 64
MLA_V = 128
MLA_QK_PAD = 256
ROPE_THETA = 10000.0
SWA_HEADS = 32
SWA_KV_HEADS = 4
SWA_GROUP = SWA_HEADS // SWA_KV_HEADS
SWA_HEAD_DIM = 64
WINDOW = 128
BLOCK = 128
N_BUCKETS = 32
MAX_DISTANCE = 128
D_FF = 4 * D_MODEL
EPS = 1e-6

SWA_BAND = 3 * BLOCK
SWA_KEYS = 512
SINK_COL = SWA_BAND + N_META
NEG = -1e30

VMEM_LIMIT_V7X = 56 * 1024 * 1024

BF16 = jnp.bfloat16
F32 = jnp.float32


def _cparams(sem):
    return pltpu.CompilerParams(dimension_semantics=sem, vmem_limit_bytes=VMEM_LIMIT_V7X)


def _rms(x, g):
    r = lax.rsqrt(jnp.mean(x * x, axis=-1, keepdims=True) + EPS)
    return x * r * g


def _dot(a, b):
    return jnp.dot(a, b, preferred_element_type=F32)


def _dot_t(a, b):
    return lax.dot_general(a, b, (((1,), (1,)), ((), ())), preferred_element_type=F32)


def _rope128(y, cos, sin):
    return y * cos + pltpu.roll(y, 32, axis=1) * sin


def _mla_proj_kernel(h_ref, g_ref, wd_ref, qn_ref, kvn_ref, wuq_ref, wuk_ref, wuv_ref,
                     cos_ref, sin_ref, q_ref, kc_ref, v_ref):
    xn = _rms(h_ref[...], g_ref[...]).astype(BF16)
    a = _dot(xn, wd_ref[...])
    cq = _rms(a[:, :MLA_Q_LORA], qn_ref[...]).astype(BF16)
    ckv = _rms(a[:, MLA_Q_LORA:MLA_Q_LORA + MLA_KV_LORA], kvn_ref[...]).astype(BF16)
    cos = cos_ref[...]
    sin = sin_ref[...]
    kr = _rope128(a[:, MLA_Q_LORA + MLA_KV_LORA:], cos, sin).astype(BF16)
    scale = (MLA_NOPE + MLA_ROPE) ** -0.5
    v_ref[...] = _dot(ckv, wuv_ref[...]).astype(BF16)
    for hd in range(MLA_HEADS):
        lo = hd * MLA_QK_PAD
        qh = _dot(cq, wuq_ref[:, lo:lo + MLA_QK_PAD])
        q_ref[:, lo:lo + MLA_NOPE] = (qh[:, :MLA_NOPE] * scale).astype(BF16)
        q_ref[:, lo + MLA_NOPE:lo + MLA_QK_PAD] = (
            _rope128(qh[:, MLA_NOPE:], cos, sin) * scale).astype(BF16)
        kc_ref[:, lo:lo + MLA_NOPE] = _dot(
            ckv, wuk_ref[:, hd * MLA_NOPE:(hd + 1) * MLA_NOPE]).astype(BF16)
        kc_ref[:, lo + MLA_NOPE:lo + MLA_QK_PAD] = kr


def _mla_proj(h, g, w, cos, sin, tm, n_pos_blocks):
    rows = h.shape[0]
    full = lambda shape: pl.BlockSpec(shape, lambda i: (0,) * len(shape))
    row = lambda width: pl.BlockSpec((tm, width), lambda i: (i, 0))
    tab = pl.BlockSpec((tm, 128), lambda i: (i % n_pos_blocks, 0))
    wd, qn, kvn, wuq, wuk, wuv = w
    return pl.pallas_call(
        _mla_proj_kernel,
        grid=(rows // tm,),
        in_specs=[row(D_MODEL), full(g.shape), full(wd.shape), full(qn.shape), full(kvn.shape),
                  full(wuq.shape), full(wuk.shape), full(wuv.shape), tab, tab],
        out_specs=[row(MLA_HEADS * MLA_QK_PAD), row(MLA_HEADS * MLA_QK_PAD),
                   row(MLA_HEADS * MLA_V)],
        out_shape=[jax.ShapeDtypeStruct((rows, MLA_HEADS * MLA_QK_PAD), BF16),
                   jax.ShapeDtypeStruct((rows, MLA_HEADS * MLA_QK_PAD), BF16),
                   jax.ShapeDtypeStruct((rows, MLA_HEADS * MLA_V), BF16)],
        compiler_params=_cparams(("parallel",)),
        name="mla_proj",
    )(h, g, wd, qn, kvn, wuq, wuk, wuv, cos, sin)


def _mla_attn_kernel(q_ref, k_ref, v_ref, km_ref, vm_ref, o_ref, *, tk):
    q = q_ref[...]
    tq = q.shape[0]
    n_chunks = k_ref.shape[0] // tk

    def chunk(k, v, m, l, acc):
        s = _dot_t(q, k)
        m_new = jnp.maximum(m, jnp.max(s, axis=-1, keepdims=True))
        alpha = jnp.exp(m - m_new)
        p = jnp.exp(s - m_new)
        l = alpha * l + jnp.sum(p, axis=-1, keepdims=True)
        acc = alpha * acc + _dot(p.astype(BF16), v)
        return m_new, l, acc

    m = jnp.full((tq, 1), NEG, F32)
    l = jnp.zeros((tq, 1), F32)
    acc = jnp.zeros((tq, MLA_V), F32)
    m, l, acc = chunk(km_ref[...], vm_ref[...], m, l, acc)
    for c in range(n_chunks):
        m, l, acc = chunk(k_ref[c * tk:(c + 1) * tk, :], v_ref[c * tk:(c + 1) * tk, :], m, l, acc)
    o_ref[...] = (acc / l).astype(BF16)


def _mla_attn(q, kc, v, kc_meta, v_meta, batch, tq, tk):
    rows = q.shape[0]
    seq = kc.shape[0] // batch
    nq = rows // batch // tq
    return pl.pallas_call(
        functools.partial(_mla_attn_kernel, tk=tk),
        grid=(batch, MLA_HEADS, nq),
        in_specs=[
            pl.BlockSpec((tq, MLA_QK_PAD), lambda s, h, i: (s * nq + i, h)),
            pl.BlockSpec((seq, MLA_QK_PAD), lambda s, h, i: (s, h)),
            pl.BlockSpec((seq, MLA_V), lambda s, h, i: (s, h)),
            pl.BlockSpec((N_META, MLA_QK_PAD), lambda s, h, i: (s, h)),
            pl.BlockSpec((N_META, MLA_V), lambda s, h, i: (s, h)),
        ],
        out_specs=pl.BlockSpec((tq, MLA_V), lambda s, h, i: (s * nq + i, h)),
        out_shape=jax.ShapeDtypeStruct((rows, MLA_HEADS * MLA_V), BF16),
        compiler_params=_cparams(("parallel", "parallel", "arbitrary")),
        name="mla_attn",
    )(q, kc, v, kc_meta, v_meta)


def _swa_proj_kernel(h_ref, g_ref, w_ref, q_ref, k_ref, v_ref):
    xn = _rms(h_ref[...], g_ref[...]).astype(BF16)
    dq = SWA_HEADS * SWA_HEAD_DIM
    dk = SWA_KV_HEADS * SWA_HEAD_DIM
    scale = SWA_HEAD_DIM ** -0.5
    for kvh in range(SWA_KV_HEADS):
        lo = kvh * SWA_GROUP * SWA_HEAD_DIM
        qk = _dot(xn, w_ref[:, lo:lo + SWA_GROUP * SWA_HEAD_DIM]) * scale
        for g in range(SWA_GROUP):
            q_ref[kvh * SWA_GROUP + g] = qk[:, g * SWA_HEAD_DIM:(g + 1) * SWA_HEAD_DIM].astype(BF16)
    kv = _dot(xn, w_ref[:, dq:dq + 2 * dk])
    for kvh in range(SWA_KV_HEADS):
        k_ref[kvh] = kv[:, kvh * SWA_HEAD_DIM:(kvh + 1) * SWA_HEAD_DIM].astype(BF16)
        v_ref[kvh] = kv[:, dk + kvh * SWA_HEAD_DIM:dk + (kvh + 1) * SWA_HEAD_DIM].astype(BF16)


def _swa_proj(h, g, w, tm):
    rows = h.shape[0]
    hm = lambda n: pl.BlockSpec((n, tm, SWA_HEAD_DIM), lambda i: (0, i, 0))
    return pl.pallas_call(
        _swa_proj_kernel,
        grid=(rows // tm,),
        in_specs=[pl.BlockSpec((tm, D_MODEL), lambda i: (i, 0)),
                  pl.BlockSpec(g.shape, lambda i: (0, 0)),
                  pl.BlockSpec(w.shape, lambda i: (0, 0))],
        out_specs=[hm(SWA_HEADS), hm(SWA_KV_HEADS), hm(SWA_KV_HEADS)],
        out_shape=[jax.ShapeDtypeStruct((SWA_HEADS, rows, SWA_HEAD_DIM), BF16),
                   jax.ShapeDtypeStruct((SWA_KV_HEADS, rows, SWA_HEAD_DIM), BF16),
                   jax.ShapeDtypeStruct((SWA_KV_HEADS, rows, SWA_HEAD_DIM), BF16)],
        compiler_params=_cparams(("parallel",)),
        name="swa_proj",
    )(h, g, w)


def _t5_bucket_exact(rel):
    nb = N_BUCKETS // 2
    max_exact = nb // 2
    table = []
    for n in range(int(np.abs(rel).max()) + 1):
        if n < max_exact:
            table.append(n)
        else:
            k = 0
            while 64 * 2 ** (k + 1) <= n * n:
                k += 1
            table.append(min(max_exact + k, nb - 1))
    return (np.asarray(table, np.int32)[np.abs(rel)] + np.where(rel > 0, nb, 0)).astype(np.int32)


def _swa_tile_codes(qpos, band_start):
    nq = qpos.shape[0]
    codes = np.full((nq, SWA_KEYS), -1, np.int32)
    kpos_band = N_META + band_start + np.arange(SWA_BAND)
    rel = kpos_band[None, :] - qpos[:, None]
    ok = np.abs(rel) <= WINDOW
    codes[:, :SWA_BAND] = np.where(ok, _t5_bucket_exact(rel), -1)
    rel_m = np.arange(N_META)[None, :] - qpos[:, None]
    codes[:, SWA_BAND:SWA_BAND + N_META] = _t5_bucket_exact(rel_m)
    codes[:, SINK_COL] = N_BUCKETS
    return codes


def _swa_bias_kernel(code_ref, rb_ref, sink_ref, o_ref):
    hd = pl.program_id(1)
    code = code_ref[0]
    t = jnp.full(code.shape, NEG, F32)
    for b in range(N_BUCKETS):
        t = jnp.where(code == b, rb_ref[b, hd], t)
    t = jnp.where(code == N_BUCKETS, sink_ref[hd], t)
    o_ref[0, 0] = t


def _swa_bias(codes, rel_bias, sink):
    nv, nq, nk = codes.shape
    return pl.pallas_call(
        _swa_bias_kernel,
        grid=(nv, SWA_HEADS),
        in_specs=[pl.BlockSpec((1, nq, nk), lambda v, h: (v, 0, 0)),
                  pl.BlockSpec(memory_space=pltpu.SMEM),
                  pl.BlockSpec(memory_space=pltpu.SMEM)],
        out_specs=pl.BlockSpec((1, 1, nq, nk), lambda v, h: (v, h, 0, 0)),
        out_shape=jax.ShapeDtypeStruct((nv, SWA_HEADS, nq, nk), F32),
        compiler_params=_cparams(("parallel", "parallel")),
        name="swa_bias",
    )(codes, rel_bias, sink)


def _swa_attn_kernel(q_ref, k_ref, v_ref, km_ref, vm_ref, b_ref, o_ref, *, sub, real):
    nsub = q_ref.shape[1] // sub
    seq = k_ref.shape[1]
    n_blocks = seq // BLOCK
    pad = jnp.zeros((SWA_KEYS - SWA_BAND - N_META, SWA_HEAD_DIM), BF16)
    for j in range(nsub):
        if real:
            b = pl.program_id(2) * nsub + j
            start = pl.multiple_of(jnp.clip((b - 1) * BLOCK, 0, seq - SWA_BAND), BLOCK)
            variant = jnp.where(b == 0, 1, jnp.where(b == n_blocks - 1, 2, 0))
        else:
            start = 0
            variant = 0
        kt = jnp.concatenate([k_ref[0, pl.ds(start, SWA_BAND), :], km_ref[0], pad], axis=0)
        vt = jnp.concatenate([v_ref[0, pl.ds(start, SWA_BAND), :], vm_ref[0], pad], axis=0)
        q = q_ref[:, j * sub:(j + 1) * sub, :].reshape(SWA_GROUP * sub, SWA_HEAD_DIM)
        s = _dot_t(q, kt) + b_ref[variant].reshape(SWA_GROUP * sub, SWA_KEYS)
        m = jnp.max(s, axis=-1, keepdims=True)
        e = jnp.exp(s - m)
        l = jnp.sum(e, axis=-1, keepdims=True)
        o = _dot(e.astype(BF16), vt) / l
        o = o.reshape(SWA_GROUP, sub, SWA_HEAD_DIM)
        o_ref[j * sub:(j + 1) * sub, :] = jnp.concatenate(
            [o[g] for g in range(SWA_GROUP)], axis=1).astype(BF16)


def _swa_attn(q, k, v, k_meta, v_meta, bias, batch, tq, sub, real):
    rows = q.shape[1]
    seq = k.shape[1] // batch
    nq = rows // batch // tq
    nv = bias.shape[0]
    return pl.pallas_call(
        functools.partial(_swa_attn_kernel, sub=sub, real=real),
        grid=(SWA_KV_HEADS, batch, nq),
        in_specs=[
            pl.BlockSpec((SWA_GROUP, tq, SWA_HEAD_DIM), lambda kv, s, i: (kv, s * nq + i, 0)),
            pl.BlockSpec((1, seq, SWA_HEAD_DIM), lambda kv, s, i: (kv, s, 0)),
            pl.BlockSpec((1, seq, SWA_HEAD_DIM), lambda kv, s, i: (kv, s, 0)),
            pl.BlockSpec((1, N_META, SWA_HEAD_DIM), lambda kv, s, i: (kv, s, 0)),
            pl.BlockSpec((1, N_META, SWA_HEAD_DIM), lambda kv, s, i: (kv, s, 0)),
            pl.BlockSpec((nv, SWA_GROUP, sub, SWA_KEYS), lambda kv, s, i: (0, kv, 0, 0)),
        ],
        out_specs=pl.BlockSpec((tq, SWA_GROUP * SWA_HEAD_DIM), lambda kv, s, i: (s * nq + i, kv)),
        out_shape=jax.ShapeDtypeStruct((rows, SWA_HEADS * SWA_HEAD_DIM), BF16),
        compiler_params=_cparams(("parallel", "parallel", "arbitrary")),
        name="swa_attn",
    )(q, k, v, k_meta, v_meta, bias)


def _attn_out_kernel(o_ref, w_ref, g_ref, h_ref, out_ref):
    m = _dot(o_ref[...], w_ref[...])
    out_ref[...] = h_ref[...] + _rms(m, g_ref[...])


def _attn_out(o, w, g, h, tm):
    rows = h.shape[0]
    return pl.pallas_call(
        _attn_out_kernel,
        grid=(rows // tm,),
        in_specs=[pl.BlockSpec((tm, o.shape[1]), lambda i: (i, 0)),
                  pl.BlockSpec(w.shape, lambda i: (0, 0)),
                  pl.BlockSpec(g.shape, lambda i: (0, 0)),
                  pl.BlockSpec((tm, D_MODEL), lambda i: (i, 0))],
        out_specs=pl.BlockSpec((tm, D_MODEL), lambda i: (i, 0)),
        out_shape=jax.ShapeDtypeStruct((rows, D_MODEL), F32),
        compiler_params=_cparams(("parallel",)),
        name="attn_out",
    )(o, w, g, h)


def _mlp_kernel(h_ref, gpre_ref, win_ref, wout_ref, gpost_ref, out_ref, xn_ref, acc_ref):
    j = pl.program_id(1)

    @pl.when(j == 0)
    def _():
        xn_ref[...] = _rms(h_ref[...], gpre_ref[...]).astype(BF16)
        acc_ref[...] = jnp.zeros_like(acc_ref)

    u = jnp.maximum(_dot(xn_ref[...], win_ref[...]), 0.0)
    acc_ref[...] += _dot((u * u).astype(BF16), wout_ref[...])

    @pl.when(j == pl.num_programs(1) - 1)
    def _():
        out_ref[...] = h_ref[...] + _rms(acc_ref[...], gpost_ref[...])


def _mlp(h, gpre, w_in, w_out, gpost, tm, tf):
    rows = h.shape[0]
    return pl.pallas_call(
        _mlp_kernel,
        grid=(rows // tm, D_FF // tf),
        in_specs=[pl.BlockSpec((tm, D_MODEL), lambda i, j: (i, 0)),
                  pl.BlockSpec(gpre.shape, lambda i, j: (0, 0)),
                  pl.BlockSpec((D_MODEL, tf), lambda i, j: (0, j)),
                  pl.BlockSpec((tf, D_MODEL), lambda i, j: (j, 0)),
                  pl.BlockSpec(gpost.shape, lambda i, j: (0, 0))],
        out_specs=pl.BlockSpec((tm, D_MODEL), lambda i, j: (i, 0)),
        out_shape=jax.ShapeDtypeStruct((rows, D_MODEL), F32),
        scratch_shapes=[pltpu.VMEM((tm, D_MODEL), BF16), pltpu.VMEM((tm, D_MODEL), F32)],
        compiler_params=_cparams(("parallel", "arbitrary")),
        name="mlp",
    )(h, gpre, w_in, w_out, gpost)


def _rope_dup(w):
    return jnp.concatenate([w, w], axis=-1)


def _prep_mla(w_dq, q_norm, w_uq, w_dkv, kv_norm, w_ukv):
    wd = jnp.concatenate([w_dq, w_dkv[:, :MLA_KV_LORA], _rope_dup(w_dkv[:, MLA_KV_LORA:])], axis=1)
    uq = w_uq.reshape(MLA_Q_LORA, MLA_HEADS, MLA_NOPE + MLA_ROPE)
    wuq = jnp.concatenate([uq[..., :MLA_NOPE], _rope_dup(uq[..., MLA_NOPE:])], axis=-1)
    wuq = wuq.reshape(MLA_Q_LORA, MLA_HEADS * MLA_QK_PAD)
    ukv = w_ukv.reshape(MLA_KV_LORA, MLA_HEADS, MLA_NOPE + MLA_V)
    wuk = ukv[..., :MLA_NOPE].reshape(MLA_KV_LORA, MLA_HEADS * MLA_NOPE)
    wuv = ukv[..., MLA_NOPE:].reshape(MLA_KV_LORA, MLA_HEADS * MLA_V)
    return (wd.astype(BF16), q_norm[None], kv_norm[None], wuq.astype(BF16), wuk.astype(BF16),
            wuv.astype(BF16))


def _rope_tables(pos):
    half = MLA_ROPE // 2
    inv = ROPE_THETA ** (-(jnp.arange(half, dtype=F32) / half))
    ang = pos.astype(F32)[:, None] * inv[None, :]
    cos, sin = jnp.cos(ang), jnp.sin(ang)
    zero = jnp.zeros((pos.shape[0], 128 - MLA_ROPE), F32)
    return (jnp.concatenate([cos, cos, zero], axis=1), jnp.concatenate([-sin, sin, zero], axis=1))


def _trunk(x, meta_tokens, p):
    batch, seq, _ = x.shape
    h = x.reshape(batch * seq, D_MODEL)
    hm = jnp.broadcast_to(meta_tokens[None], (batch, N_META, D_MODEL)).reshape(batch * N_META, D_MODEL)
    rows_m = batch * N_META
    tq = 512
    cos_r, sin_r = _rope_tables(N_META + jnp.arange(seq))
    cos_m, sin_m = (jnp.tile(t, (batch, 1)) for t in _rope_tables(jnp.arange(N_META)))
    for i in range(DEPTH):
        j = i // 2
        g_pre = p["norm_mix_pre"][i][None]
        g_post = p["norm_mix_post"][i][None]
        if i % 2 == 0:
            w = p["mla"][j]
            q, kc, v = _mla_proj(h, g_pre, w, cos_r, sin_r, 256, seq // 256)
            q_m, kc_m, v_m = _mla_proj(hm, g_pre, w, cos_m, sin_m, rows_m, 1)
            o = _mla_attn(q, kc, v, kc_m, v_m, batch, tq, 512)
            o_m = _mla_attn(q_m, kc, v, kc_m, v_m, batch, N_META, 512)
            w_o = p["mla_w_o"][j]
        else:
            q, k, v = _swa_proj(h, g_pre, p["swa_w_qkv"][j], 512)
            q_m, k_m, v_m = _swa_proj(hm, g_pre, p["swa_w_qkv"][j], rows_m)
            o = _swa_attn(q, k, v, k_m, v_m, p["swa_bias"][j], batch, tq, BLOCK, True)
            o_m = _swa_attn(q_m, k, v, k_m, v_m, p["swa_bias_meta"][j], batch, N_META, N_META, False)
            w_o = p["swa_w_o"][j]
        h = _attn_out(o, w_o, g_post, h, 512)
        hm = _attn_out(o_m, w_o, g_post, hm, rows_m)
        mlp_args = (p["norm_mlp_pre"][i][None], p["mlp_w_in"][i], p["mlp_w_out"][i],
                    p["norm_mlp_post"][i][None])
        h = _mlp(h, *mlp_args, 512, 512)
        hm = _mlp(hm, *mlp_args, rows_m, 512)
    return h.reshape(batch, seq, D_MODEL)


def kernel(x_prompt, x_sample, meta_tokens, rel_bias, mla_w_dq, mla_q_norm, mla_w_uq, mla_w_dkv, mla_kv_norm, mla_w_ukv, mla_w_o, swa_w_qkv, swa_w_o, swa_sink, mlp_w_in, mlp_w_out, norm_mix_pre, norm_mix_post, norm_mlp_pre, norm_mlp_post):
    seq = x_prompt.shape[1]
    assert x_sample.shape[1] == seq and seq % 512 == 0
    n_mla = mla_w_dq.shape[0]
    n_swa = swa_w_qkv.shape[0]
    qpos = N_META + np.arange(BLOCK)
    codes = np.stack([
        _swa_tile_codes(qpos + BLOCK, 0),
        _swa_tile_codes(qpos, 0),
        _swa_tile_codes(qpos + 2 * BLOCK, 0),
    ])
    codes_meta = _swa_tile_codes(np.arange(N_META), 0)[None]
    p = {
        "mla": [_prep_mla(mla_w_dq[j], mla_q_norm[j], mla_w_uq[j], mla_w_dkv[j], mla_kv_norm[j],
                          mla_w_ukv[j]) for j in range(n_mla)],
        "mla_w_o": mla_w_o.astype(BF16),
        "swa_w_qkv": swa_w_qkv.astype(BF16),
        "swa_w_o": swa_w_o.astype(BF16),
        "swa_bias": [_swa_bias(jnp.asarray(codes), rel_bias, swa_sink[j]) for j in range(n_swa)],
        "swa_bias_meta": [_swa_bias(jnp.asarray(codes_meta), rel_bias, swa_sink[j])
                          for j in range(n_swa)],
        "mlp_w_in": mlp_w_in.astype(BF16),
        "mlp_w_out": mlp_w_out.astype(BF16),
        "norm_mix_pre": norm_mix_pre, "norm_mix_post": norm_mix_post,
        "norm_mlp_pre": norm_mlp_pre, "norm_mlp_post": norm_mlp_post,
    }
    return (_trunk(x_prompt, meta_tokens, p), _trunk(x_sample, meta_tokens, p))
```

```python
import functools
import math

import jax
import jax.numpy as jnp
import numpy as np
from jax import lax
from jax.experimental import pallas as pl
from jax.experimental.pallas import tpu as pltpu

D_MODEL = 2048
DEPTH = 4
N_META = 16
MLA_HEADS = 16
MLA_Q_LORA = 512
MLA_KV_LORA = 512
MLA_NOPE = 128
MLA_ROPE = 64
MLA_V = 128
MLA_QK_PAD = 256
ROPE_THETA = 10000.0
SWA_HEADS = 32
SWA_KV_HEADS = 4
SWA_GROUP = SWA_HEADS // SWA_KV_HEADS
SWA_HEAD_DIM = 64
WINDOW = 128
BLOCK = 128
N_BUCKETS = 32
MAX_DISTANCE = 128
D_FF = 4 * D_MODEL
EPS = 1e-6

SWA_BAND = 3 * BLOCK
SWA_KEYS = 512
SINK_COL = SWA_BAND + N_META
NEG = -1e30

VMEM_LIMIT_V7X = 56 * 1024 * 1024

BF16 = jnp.bfloat16
F32 = jnp.float32


def _cparams(sem):
    return pltpu.CompilerParams(dimension_semantics=sem, vmem_limit_bytes=VMEM_LIMIT_V7X)


def _rms(x, g):
    r = lax.rsqrt(jnp.mean(x * x, axis=-1, keepdims=True) + EPS)
    return x * r * g


def _dot(a, b):
    return jnp.dot(a, b, preferred_element_type=F32)


def _dot_t(a, b):
    return lax.dot_general(a, b, (((1,), (1,)), ((), ())), preferred_element_type=F32)


def _rope128(y, cos, sin):
    return y * cos + pltpu.roll(y, 32, axis=1) * sin


MLA_SCALE = (MLA_NOPE + MLA_ROPE) ** -0.5 * math.log2(math.e)


def _mla_latents(h_ref, g_ref, wd_ref, qn_ref, kvn_ref):
    xn = _rms(h_ref[...], g_ref[...]).astype(BF16)
    a = _dot(xn, wd_ref[...])
    cq = _rms(a[:, :MLA_Q_LORA], qn_ref[...]).astype(BF16)
    ckv = _rms(a[:, MLA_Q_LORA:MLA_Q_LORA + MLA_KV_LORA], kvn_ref[...]).astype(BF16)
    return cq, ckv, a[:, MLA_Q_LORA + MLA_KV_LORA:]


def _store_keys(kc_ref, ckv, wuk_ref, kr):
    for hd in range(MLA_HEADS):
        lo = hd * MLA_QK_PAD
        kc_ref[:, lo:lo + MLA_NOPE] = _dot(
            ckv, wuk_ref[:, hd * MLA_NOPE:(hd + 1) * MLA_NOPE]).astype(BF16)
        kc_ref[:, lo + MLA_NOPE:lo + MLA_QK_PAD] = kr


def _mla_proj_rows_kernel(h_ref, g_ref, wd_ref, qn_ref, kvn_ref, wuq_ref, wuk_ref, wuv_ref,
                          cos_ref, sin_ref, q_ref, kc_ref, v_ref):
    cq, ckv, kr = _mla_latents(h_ref, g_ref, wd_ref, qn_ref, kvn_ref)
    cos = cos_ref[...]
    sin = sin_ref[...]
    _store_keys(kc_ref, ckv, wuk_ref, _rope128(kr, cos, sin).astype(BF16))
    v_ref[...] = _dot(ckv, wuv_ref[...]).astype(BF16)
    for hd in range(MLA_HEADS):
        lo = hd * MLA_QK_PAD
        qh = _dot(cq, wuq_ref[:, lo:lo + MLA_QK_PAD])
        q_ref[:, lo:lo + MLA_NOPE] = (qh[:, :MLA_NOPE] * MLA_SCALE).astype(BF16)
        q_ref[:, lo + MLA_NOPE:lo + MLA_QK_PAD] = (
            _rope128(qh[:, MLA_NOPE:], cos, sin) * MLA_SCALE).astype(BF16)


def _mla_proj_cols_kernel(h_ref, g_ref, wd_ref, qn_ref, kvn_ref, wuqt_ref, wuk_ref, wuvt_ref,
                          cos_ref, sin_ref, cost_ref, sint_ref, qt_ref, kc_ref, vt_ref):
    cq, ckv, kr = _mla_latents(h_ref, g_ref, wd_ref, qn_ref, kvn_ref)
    _store_keys(kc_ref, ckv, wuk_ref, _rope128(kr, cos_ref[...], sin_ref[...]).astype(BF16))
    vt_ref[...] = _dot_t(wuvt_ref[...], ckv).astype(BF16)
    qt = _dot_t(wuqt_ref[...], cq)
    c = cost_ref[...]
    s = sint_ref[...]
    half = MLA_ROPE // 2
    for hd in range(MLA_HEADS):
        lo = hd * MLA_QK_PAD
        qt_ref[lo:lo + MLA_NOPE, :] = (qt[lo:lo + MLA_NOPE] * MLA_SCALE).astype(BF16)
        x1 = qt[lo + MLA_NOPE:lo + MLA_NOPE + half]
        x2 = qt[lo + MLA_NOPE + half:lo + MLA_NOPE + MLA_ROPE]
        qt_ref[lo + MLA_NOPE:lo + MLA_NOPE + half, :] = ((x1 * c - x2 * s) * MLA_SCALE).astype(BF16)
        qt_ref[lo + MLA_NOPE + half:lo + MLA_NOPE + MLA_ROPE, :] = (
            (x2 * c + x1 * s) * MLA_SCALE).astype(BF16)
        qt_ref[lo + MLA_NOPE + MLA_ROPE:lo + MLA_QK_PAD, :] = jnp.zeros(
            (MLA_QK_PAD - MLA_NOPE - MLA_ROPE, qt.shape[1]), BF16)


def _full_spec(a):
    return pl.BlockSpec(a.shape, lambda i: (0,) * a.ndim)


def _mla_proj_rows(h, g, w, cos, sin):
    rows = h.shape[0]
    wd, qn, kvn, wuq, wuk, wuv = w
    args = (h, g, wd, qn, kvn, wuq, wuk, wuv, cos, sin)
    widths = (MLA_HEADS * MLA_QK_PAD, MLA_HEADS * MLA_QK_PAD, MLA_HEADS * MLA_V)
    return pl.pallas_call(
        _mla_proj_rows_kernel,
        grid=(1,),
        in_specs=[_full_spec(a) for a in args],
        out_specs=[pl.BlockSpec((rows, n), lambda i: (0, 0)) for n in widths],
        out_shape=[jax.ShapeDtypeStruct((rows, n), BF16) for n in widths],
        compiler_params=_cparams(("arbitrary",)),
        name="mla_proj_rows",
    )(*args)


def _mla_proj_cols(h, g, w, tabs, tm, seq):
    rows = h.shape[0]
    wd, qn, kvn, wuqt, wuk, wuvt = w
    cos, sin, cost, sint = tabs
    nblk = seq // tm
    row = lambda width: pl.BlockSpec((tm, width), lambda i: (i, 0))
    col = lambda height: pl.BlockSpec((height, tm), lambda i: (0, i))
    tab = pl.BlockSpec((tm, 128), lambda i: (i % nblk, 0))
    tabt = pl.BlockSpec((MLA_ROPE // 2, tm), lambda i: (0, i % nblk))
    return pl.pallas_call(
        _mla_proj_cols_kernel,
        grid=(rows // tm,),
        in_specs=[row(D_MODEL)] + [_full_spec(a) for a in (g, wd, qn, kvn, wuqt, wuk, wuvt)]
        + [tab, tab, tabt, tabt],
        out_specs=[col(MLA_HEADS * MLA_QK_PAD), row(MLA_HEADS * MLA_QK_PAD), col(MLA_HEADS * MLA_V)],
        out_shape=[jax.ShapeDtypeStruct((MLA_HEADS * MLA_QK_PAD, rows), BF16),
                   jax.ShapeDtypeStruct((rows, MLA_HEADS * MLA_QK_PAD), BF16),
                   jax.ShapeDtypeStruct((MLA_HEADS * MLA_V, rows), BF16)],
        compiler_params=_cparams(("parallel",)),
        name="mla_proj_cols",
    )(h, g, wd, qn, kvn, wuqt, wuk, wuvt, cos, sin, cost, sint)


def _mla_attn_kernel(qt_ref, k_ref, vt_ref, qm_ref, km_ref, vm_ref, ki_ref, vti_ref,
                     o_ref, om_ref, mm_ref, lm_ref, am_ref, *, tk):
    i = pl.program_id(2)
    n_chunks = k_ref.shape[0] // tk
    qm = qm_ref[...]

    @pl.when(i == 0)
    def _():
        s = _dot_t(qm, km_ref[...])
        m = jnp.max(s, axis=-1, keepdims=True)
        p = jnp.exp2(s - m)
        mm_ref[...] = m
        lm_ref[...] = jnp.sum(p, axis=-1, keepdims=True)
        am_ref[...] = _dot(p.astype(BF16), vm_ref[...])

    qt = qt_ref[...]
    n = qt.shape[1]
    score = lambda k: _dot(k, qt)
    s_next = score(k_ref[0:tk, :])
    meta = {"s": _dot_t(qm, ki_ref[...])}

    def meta_softmax():
        m_old = mm_ref[...]
        m = jnp.maximum(m_old, jnp.max(meta["s"], axis=-1, keepdims=True))
        meta["m"] = m
        meta["alpha"] = jnp.exp2(m_old - m)
        meta["p"] = jnp.exp2(meta["s"] - m)

    def meta_values():
        meta["o"] = _dot_t(meta["p"].astype(BF16), vti_ref[...])

    def meta_update():
        mm_ref[...] = meta["m"]
        lm_ref[...] = meta["alpha"] * lm_ref[...] + jnp.sum(meta["p"], axis=-1, keepdims=True)
        am_ref[...] = meta["alpha"] * am_ref[...] + meta["o"]

    stages = [(min(1, n_chunks - 1), meta_softmax), (min(2, n_chunks - 1), meta_values),
              (min(4, n_chunks - 1), meta_update)]

    pad = 128 - N_META
    vmt = jnp.concatenate([vm_ref[...].astype(F32), jnp.zeros((pad, MLA_V), F32)], axis=0).T
    s = score(km_ref[...])
    m = jnp.max(s, axis=0, keepdims=True)
    p = jnp.exp2(s - m)
    l = jnp.sum(p, axis=0, keepdims=True)
    pm = jnp.concatenate([p.astype(BF16), jnp.zeros((pad, n), BF16)], axis=0)
    o_prev = _dot(vmt.astype(BF16), pm)
    acc = jnp.zeros((MLA_V, n), F32)
    for c in range(n_chunks):
        s = s_next
        if c + 1 < n_chunks:
            s_next = score(k_ref[(c + 1) * tk:(c + 2) * tk, :])
        m_new = jnp.maximum(m, jnp.max(s, axis=0, keepdims=True))
        alpha = jnp.exp2(m - m_new)
        p = jnp.exp2(s - m_new)
        l = alpha * l + jnp.sum(p, axis=0, keepdims=True)
        o_c = _dot(vt_ref[:, c * tk:(c + 1) * tk], p.astype(BF16))
        acc = alpha * (acc + o_prev)
        o_prev = o_c
        m = m_new
        for at, stage in stages:
            if at == c:
                stage()
    o_ref[...] = ((acc + o_prev) * (1.0 / l)).T.astype(BF16)

    @pl.when(i == pl.num_programs(2) - 1)
    def _():
        om_ref[...] = (am_ref[...] / lm_ref[...]).astype(BF16)


def _mla_attn(qt, kc, vt, q_meta, kc_meta, v_meta, batch, tq, tk):
    rows = kc.shape[0]
    seq = rows // batch
    nq = seq // tq
    meta = lambda width: pl.BlockSpec((N_META, width), lambda s, h, i: (s, h))
    return pl.pallas_call(
        functools.partial(_mla_attn_kernel, tk=tk),
        grid=(batch, MLA_HEADS, nq),
        in_specs=[
            pl.BlockSpec((MLA_QK_PAD, tq), lambda s, h, i: (h, s * nq + i)),
            pl.BlockSpec((seq, MLA_QK_PAD), lambda s, h, i: (s, h)),
            pl.BlockSpec((MLA_V, seq), lambda s, h, i: (h, s)),
            meta(MLA_QK_PAD), meta(MLA_QK_PAD), meta(MLA_V),
            pl.BlockSpec((tq, MLA_QK_PAD), lambda s, h, i: (s * nq + i, h)),
            pl.BlockSpec((MLA_V, tq), lambda s, h, i: (h, s * nq + i)),
        ],
        out_specs=[pl.BlockSpec((tq, MLA_V), lambda s, h, i: (s * nq + i, h)), meta(MLA_V)],
        out_shape=[jax.ShapeDtypeStruct((rows, MLA_HEADS * MLA_V), BF16),
                   jax.ShapeDtypeStruct((batch * N_META, MLA_HEADS * MLA_V), BF16)],
        scratch_shapes=[pltpu.VMEM((N_META, 1), F32), pltpu.VMEM((N_META, 1), F32),
                        pltpu.VMEM((N_META, MLA_V), F32)],
        compiler_params=_cparams(("parallel", "parallel", "arbitrary")),
        name="mla_attn",
    )(qt, kc, vt, q_meta, kc_meta, v_meta, kc, vt)


def _swa_proj_kernel(h_ref, g_ref, w_ref, q_ref, k_ref, v_ref):
    xn = _rms(h_ref[...], g_ref[...]).astype(BF16)
    dq = SWA_HEADS * SWA_HEAD_DIM
    dk = SWA_KV_HEADS * SWA_HEAD_DIM
    scale = SWA_HEAD_DIM ** -0.5
    for kvh in range(SWA_KV_HEADS):
        lo = kvh * SWA_GROUP * SWA_HEAD_DIM
        qk = _dot(xn, w_ref[:, lo:lo + SWA_GROUP * SWA_HEAD_DIM]) * scale
        for g in range(SWA_GROUP):
            q_ref[kvh * SWA_GROUP + g] = qk[:, g * SWA_HEAD_DIM:(g + 1) * SWA_HEAD_DIM].astype(BF16)
    kv = _dot(xn, w_ref[:, dq:dq + 2 * dk])
    for kvh in range(SWA_KV_HEADS):
        k_ref[kvh] = kv[:, kvh * SWA_HEAD_DIM:(kvh + 1) * SWA_HEAD_DIM].astype(BF16)
        v_ref[kvh] = kv[:, dk + kvh * SWA_HEAD_DIM:dk + (kvh + 1) * SWA_HEAD_DIM].astype(BF16)


def _swa_proj(h, g, w, tm):
    rows = h.shape[0]
    hm = lambda n: pl.BlockSpec((n, tm, SWA_HEAD_DIM), lambda i: (0, i, 0))
    return pl.pallas_call(
        _swa_proj_kernel,
        grid=(rows // tm,),
        in_specs=[pl.BlockSpec((tm, D_MODEL), lambda i: (i, 0)),
                  pl.BlockSpec(g.shape, lambda i: (0, 0)),
                  pl.BlockSpec(w.shape, lambda i: (0, 0))],
        out_specs=[hm(SWA_HEADS), hm(SWA_KV_HEADS), hm(SWA_KV_HEADS)],
        out_shape=[jax.ShapeDtypeStruct((SWA_HEADS, rows, SWA_HEAD_DIM), BF16),
                   jax.ShapeDtypeStruct((SWA_KV_HEADS, rows, SWA_HEAD_DIM), BF16),
                   jax.ShapeDtypeStruct((SWA_KV_HEADS, rows, SWA_HEAD_DIM), BF16)],
        compiler_params=_cparams(("parallel",)),
        name="swa_proj",
    )(h, g, w)


def _t5_bucket_exact(rel):
    nb = N_BUCKETS // 2
    max_exact = nb // 2
    table = []
    for n in range(int(np.abs(rel).max()) + 1):
        if n < max_exact:
            table.append(n)
        else:
            k = 0
            while 64 * 2 ** (k + 1) <= n * n:
                k += 1
            table.append(min(max_exact + k, nb - 1))
    return (np.asarray(table, np.int32)[np.abs(rel)] + np.where(rel > 0, nb, 0)).astype(np.int32)


def _swa_tile_codes(qpos, band_start):
    nq = qpos.shape[0]
    codes = np.full((nq, SWA_KEYS), -1, np.int32)
    kpos_band = N_META + band_start + np.arange(SWA_BAND)
    rel = kpos_band[None, :] - qpos[:, None]
    ok = np.abs(rel) <= WINDOW
    codes[:, :SWA_BAND] = np.where(ok, _t5_bucket_exact(rel), -1)
    rel_m = np.arange(N_META)[None, :] - qpos[:, None]
    codes[:, SWA_BAND:SWA_BAND + N_META] = _t5_bucket_exact(rel_m)
    codes[:, SINK_COL] = N_BUCKETS
    return codes


def _swa_bias_kernel(code_ref, rb_ref, sink_ref, o_ref):
    hd = pl.program_id(1)
    code = code_ref[0]
    t = jnp.full(code.shape, NEG, F32)
    for b in range(N_BUCKETS):
        t = jnp.where(code == b, rb_ref[b, hd], t)
    t = jnp.where(code == N_BUCKETS, sink_ref[hd], t)
    o_ref[0, 0] = t


def _swa_bias(codes, rel_bias, sink):
    nv, nq, nk = codes.shape
    return pl.pallas_call(
        _swa_bias_kernel,
        grid=(nv, SWA_HEADS),
        in_specs=[pl.BlockSpec((1, nq, nk), lambda v, h: (v, 0, 0)),
                  pl.BlockSpec(memory_space=pltpu.SMEM),
                  pl.BlockSpec(memory_space=pltpu.SMEM)],
        out_specs=pl.BlockSpec((1, 1, nq, nk), lambda v, h: (v, h, 0, 0)),
        out_shape=jax.ShapeDtypeStruct((nv, SWA_HEADS, nq, nk), F32),
        compiler_params=_cparams(("parallel", "parallel")),
        name="swa_bias",
    )(codes, rel_bias, sink)


def _swa_attn_kernel(q_ref, k_ref, v_ref, km_ref, vm_ref, b_ref, o_ref, *, sub, real):
    nsub = q_ref.shape[1] // sub
    seq = k_ref.shape[1]
    n_blocks = seq // BLOCK
    pad = jnp.zeros((SWA_KEYS - SWA_BAND - N_META, SWA_HEAD_DIM), BF16)
    for j in range(nsub):
        if real:
            b = pl.program_id(2) * nsub + j
            start = pl.multiple_of(jnp.clip((b - 1) * BLOCK, 0, seq - SWA_BAND), BLOCK)
            variant = jnp.where(b == 0, 1, jnp.where(b == n_blocks - 1, 2, 0))
        else:
            start = 0
            variant = 0
        kt = jnp.concatenate([k_ref[0, pl.ds(start, SWA_BAND), :], km_ref[0], pad], axis=0)
        vt = jnp.concatenate([v_ref[0, pl.ds(start, SWA_BAND), :], vm_ref[0], pad], axis=0)
        q = q_ref[:, j * sub:(j + 1) * sub, :].reshape(SWA_GROUP * sub, SWA_HEAD_DIM)
        s = _dot_t(q, kt) + b_ref[variant].reshape(SWA_GROUP * sub, SWA_KEYS)
        m = jnp.max(s, axis=-1, keepdims=True)
        e = jnp.exp(s - m)
        l = jnp.sum(e, axis=-1, keepdims=True)
        o = _dot(e.astype(BF16), vt) / l
        o = o.reshape(SWA_GROUP, sub, SWA_HEAD_DIM)
        o_ref[j * sub:(j + 1) * sub, :] = jnp.concatenate(
            [o[g] for g in range(SWA_GROUP)], axis=1).astype(BF16)


def _swa_attn(q, k, v, k_meta, v_meta, bias, batch, tq, sub, real):
    rows = q.shape[1]
    seq = k.shape[1] // batch
    nq = rows // batch // tq
    nv = bias.shape[0]
    return pl.pallas_call(
        functools.partial(_swa_attn_kernel, sub=sub, real=real),
        grid=(SWA_KV_HEADS, batch, nq),
        in_specs=[
            pl.BlockSpec((SWA_GROUP, tq, SWA_HEAD_DIM), lambda kv, s, i: (kv, s * nq + i, 0)),
            pl.BlockSpec((1, seq, SWA_HEAD_DIM), lambda kv, s, i: (kv, s, 0)),
            pl.BlockSpec((1, seq, SWA_HEAD_DIM), lambda kv, s, i: (kv, s, 0)),
            pl.BlockSpec((1, N_META, SWA_HEAD_DIM), lambda kv, s, i: (kv, s, 0)),
            pl.BlockSpec((1, N_META, SWA_HEAD_DIM), lambda kv, s, i: (kv, s, 0)),
            pl.BlockSpec((nv, SWA_GROUP, sub, SWA_KEYS), lambda kv, s, i: (0, kv, 0, 0)),
        ],
        out_specs=pl.BlockSpec((tq, SWA_GROUP * SWA_HEAD_DIM), lambda kv, s, i: (s * nq + i, kv)),
        out_shape=jax.ShapeDtypeStruct((rows, SWA_HEADS * SWA_HEAD_DIM), BF16),
        compiler_params=_cparams(("parallel", "parallel", "arbitrary")),
        name="swa_attn",
    )(q, k, v, k_meta, v_meta, bias)


def _attn_out_kernel(o_ref, w_ref, g_ref, h_ref, out_ref):
    m = _dot(o_ref[...], w_ref[...])
    out_ref[...] = h_ref[...] + _rms(m, g_ref[...])


def _attn_out(o, w, g, h, tm):
    rows = h.shape[0]
    return pl.pallas_call(
        _attn_out_kernel,
        grid=(rows // tm,),
        in_specs=[pl.BlockSpec((tm, o.shape[1]), lambda i: (i, 0)),
                  pl.BlockSpec(w.shape, lambda i: (0, 0)),
                  pl.BlockSpec(g.shape, lambda i: (0, 0)),
                  pl.BlockSpec((tm, D_MODEL), lambda i: (i, 0))],
        out_specs=pl.BlockSpec((tm, D_MODEL), lambda i: (i, 0)),
        out_shape=jax.ShapeDtypeStruct((rows, D_MODEL), F32),
        compiler_params=_cparams(("parallel",)),
        name="attn_out",
    )(o, w, g, h)


def _mlp_kernel(h_ref, gpre_ref, win_ref, wout_ref, gpost_ref, out_ref, xn_ref):
    j = pl.program_id(1)

    @pl.when(j == 0)
    def _():
        xn_ref[...] = _rms(h_ref[...], gpre_ref[...]).astype(BF16)
        out_ref[...] = jnp.zeros_like(out_ref)

    u = jnp.maximum(_dot(xn_ref[...], win_ref[...]), 0.0)
    out_ref[...] += _dot((u * u).astype(BF16), wout_ref[...])

    @pl.when(j == pl.num_programs(1) - 1)
    def _():
        out_ref[...] = h_ref[...] + _rms(out_ref[...], gpost_ref[...])


def _mlp(h, gpre, w_in, w_out, gpost, tm, tf):
    rows = h.shape[0]
    return pl.pallas_call(
        _mlp_kernel,
        grid=(rows // tm, D_FF // tf),
        in_specs=[pl.BlockSpec((tm, D_MODEL), lambda i, j: (i, 0)),
                  pl.BlockSpec(gpre.shape, lambda i, j: (0, 0)),
                  pl.BlockSpec((D_MODEL, tf), lambda i, j: (0, j)),
                  pl.BlockSpec((tf, D_MODEL), lambda i, j: (j, 0)),
                  pl.BlockSpec(gpost.shape, lambda i, j: (0, 0))],
        out_specs=pl.BlockSpec((tm, D_MODEL), lambda i, j: (i, 0)),
        out_shape=jax.ShapeDtypeStruct((rows, D_MODEL), F32),
        scratch_shapes=[pltpu.VMEM((tm, D_MODEL), BF16)],
        compiler_params=_cparams(("parallel", "arbitrary")),
        name="mlp",
    )(h, gpre, w_in, w_out, gpost)


def _rope_dup(w):
    return jnp.concatenate([w, w], axis=-1)


def _prep_mla(w_dq, q_norm, w_uq, w_dkv, kv_norm, w_ukv):
    wd = jnp.concatenate([w_dq, w_dkv[:, :MLA_KV_LORA], _rope_dup(w_dkv[:, MLA_KV_LORA:])], axis=1)
    uq = w_uq.reshape(MLA_Q_LORA, MLA_HEADS, MLA_NOPE + MLA_ROPE)
    wuq = jnp.concatenate([uq[..., :MLA_NOPE], _rope_dup(uq[..., MLA_NOPE:])], axis=-1)
    wuq = wuq.reshape(MLA_Q_LORA, MLA_HEADS * MLA_QK_PAD)
    ukv = w_ukv.reshape(MLA_KV_LORA, MLA_HEADS, MLA_NOPE + MLA_V)
    wuk = ukv[..., :MLA_NOPE].reshape(MLA_KV_LORA, MLA_HEADS * MLA_NOPE)
    wuv = ukv[..., MLA_NOPE:].reshape(MLA_KV_LORA, MLA_HEADS * MLA_V)
    wd, wuq, wuk, wuv = (t.astype(BF16) for t in (wd, wuq, wuk, wuv))
    rows = (wd, q_norm[None], kv_norm[None], wuq, wuk, wuv)
    cols = (wd, q_norm[None], kv_norm[None], wuq.T, wuk, wuv.T)
    return rows, cols


def _rope_tables(pos):
    half = MLA_ROPE // 2
    inv = ROPE_THETA ** (-(jnp.arange(half, dtype=F32) / half))
    ang = pos.astype(F32)[:, None] * inv[None, :]
    cos, sin = jnp.cos(ang), jnp.sin(ang)
    zero = jnp.zeros((pos.shape[0], 128 - MLA_ROPE), F32)
    return (jnp.concatenate([cos, cos, zero], axis=1), jnp.concatenate([-sin, sin, zero], axis=1),
            cos.T, sin.T)


def _trunk(x, meta_tokens, p):
    batch, seq, _ = x.shape
    h = x.reshape(batch * seq, D_MODEL)
    hm = jnp.broadcast_to(meta_tokens[None], (batch, N_META, D_MODEL)).reshape(batch * N_META, D_MODEL)
    rows_m = batch * N_META
    tq = 512
    tabs_r = _rope_tables(N_META + jnp.arange(seq))
    cos_m, sin_m = (jnp.tile(t, (batch, 1)) for t in _rope_tables(jnp.arange(N_META))[:2])
    for i in range(DEPTH):
        j = i // 2
        g_pre = p["norm_mix_pre"][i][None]
        g_post = p["norm_mix_post"][i][None]
        if i % 2 == 0:
            w_rows, w_cols = p["mla"][j]
            qt, kc, vt = _mla_proj_cols(h, g_pre, w_cols, tabs_r, 256, seq)
            q_m, kc_m, v_m = _mla_proj_rows(hm, g_pre, w_rows, cos_m, sin_m)
            o, o_m = _mla_attn(qt, kc, vt, q_m, kc_m, v_m, batch, tq, 512)
            w_o = p["mla_w_o"][j]
        else:
            q, k, v = _swa_proj(h, g_pre, p["swa_w_qkv"][j], 512)
            q_m, k_m, v_m = _swa_proj(hm, g_pre, p["swa_w_qkv"][j], rows_m)
            o = _swa_attn(q, k, v, k_m, v_m, p["swa_bias"][j], batch, tq, BLOCK, True)
            o_m = _swa_attn(q_m, k, v, k_m, v_m, p["swa_bias_meta"][j], batch, N_META, N_META, False)
            w_o = p["swa_w_o"][j]
        h = _attn_out(o, w_o, g_post, h, 512)
        hm = _attn_out(o_m, w_o, g_post, hm, rows_m)
        mlp_args = (p["norm_mlp_pre"][i][None], p["mlp_w_in"][i], p["mlp_w_out"][i],
                    p["norm_mlp_post"][i][None])
        h = _mlp(h, *mlp_args, min(1024, h.shape[0]), 512)
        hm = _mlp(hm, *mlp_args, rows_m, 512)
    return h.reshape(batch, seq, D_MODEL)


def kernel(x_prompt, x_sample, meta_tokens, rel_bias, mla_w_dq, mla_q_norm, mla_w_uq, mla_w_dkv, mla_kv_norm, mla_w_ukv, mla_w_o, swa_w_qkv, swa_w_o, swa_sink, mlp_w_in, mlp_w_out, norm_mix_pre, norm_mix_post, norm_mlp_pre, norm_mlp_post):
    seq = x_prompt.shape[1]
    assert x_sample.shape[1] == seq and seq % 512 == 0
    n_mla = mla_w_dq.shape[0]
    n_swa = swa_w_qkv.shape[0]
    qpos = N_META + np.arange(BLOCK)
    codes = np.stack([
        _swa_tile_codes(qpos + BLOCK, 0),
        _swa_tile_codes(qpos, 0),
        _swa_tile_codes(qpos + 2 * BLOCK, 0),
    ])
    codes_meta = _swa_tile_codes(np.arange(N_META), 0)[None]
    p = {
        "mla": [_prep_mla(mla_w_dq[j], mla_q_norm[j], mla_w_uq[j], mla_w_dkv[j], mla_kv_norm[j],
                          mla_w_ukv[j]) for j in range(n_mla)],
        "mla_w_o": mla_w_o.astype(BF16),
        "swa_w_qkv": swa_w_qkv.astype(BF16),
        "swa_w_o": swa_w_o.astype(BF16),
        "swa_bias": [_swa_bias(jnp.asarray(codes), rel_bias, swa_sink[j]) for j in range(n_swa)],
        "swa_bias_meta": [_swa_bias(jnp.asarray(codes_meta), rel_bias, swa_sink[j])
                          for j in range(n_swa)],
        "mlp_w_in": mlp_w_in.astype(BF16),
        "mlp_w_out": mlp_w_out.astype(BF16),
        "norm_mix_pre": norm_mix_pre, "norm_mix_post": norm_mix_post,
        "norm_mlp_pre": norm_mlp_pre, "norm_mlp_post": norm_mlp_post,
    }
    return (_trunk(x_prompt, meta_tokens, p), _trunk(x_sample, meta_tokens, p))
```

```python
import functools
import math

import jax
import jax.numpy as jnp
import numpy as np
from jax import lax
from jax.experimental import pallas as pl
from jax.experimental.pallas import tpu as pltpu

D_MODEL = 2048
DEPTH = 4
N_META = 16
MLA_HEADS = 16
MLA_Q_LORA = 512
MLA_KV_LORA = 512
MLA_NOPE = 128
MLA_ROPE = 64
MLA_V = 128
MLA_VX = MLA_V + 16
MLA_QK_PAD = 256
ROPE_THETA = 10000.0
SWA_HEADS = 32
SWA_KV_HEADS = 4
SWA_GROUP = SWA_HEADS // SWA_KV_HEADS
SWA_HEAD_DIM = 64
WINDOW = 128
BLOCK = 128
N_BUCKETS = 32
MAX_DISTANCE = 128
D_FF = 4 * D_MODEL
EPS = 1e-6

SWA_BAND = 3 * BLOCK
SWA_KEYS = 512
SINK_COL = SWA_BAND + N_META
NEG = -1e30

VMEM_LIMIT_V7X = 56 * 1024 * 1024

BF16 = jnp.bfloat16
F32 = jnp.float32


def _cparams(sem):
    return pltpu.CompilerParams(dimension_semantics=sem, vmem_limit_bytes=VMEM_LIMIT_V7X)


def _rms(x, g):
    r = lax.rsqrt(jnp.mean(x * x, axis=-1, keepdims=True) + EPS)
    return x * r * g


def _dot(a, b):
    return jnp.dot(a, b, preferred_element_type=F32)


def _dot_t(a, b):
    return lax.dot_general(a, b, (((1,), (1,)), ((), ())), preferred_element_type=F32)


def _rope128(y, cos, sin):
    return y * cos + pltpu.roll(y, 32, axis=1) * sin


MLA_SCALE = (MLA_NOPE + MLA_ROPE) ** -0.5 * math.log2(math.e)


def _mla_latents(h_ref, g_ref, wd_ref, qn_ref, kvn_ref):
    xn = _rms(h_ref[...], g_ref[...]).astype(BF16)
    a = _dot(xn, wd_ref[...])
    cq = _rms(a[:, :MLA_Q_LORA], qn_ref[...]).astype(BF16)
    ckv = _rms(a[:, MLA_Q_LORA:MLA_Q_LORA + MLA_KV_LORA], kvn_ref[...]).astype(BF16)
    return cq, ckv, a[:, MLA_Q_LORA + MLA_KV_LORA:]


def _store_keys(kc_ref, ckv, wuk_ref, kr):
    for hd in range(MLA_HEADS):
        lo = hd * MLA_QK_PAD
        kc_ref[:, lo:lo + MLA_NOPE] = _dot(
            ckv, wuk_ref[:, hd * MLA_NOPE:(hd + 1) * MLA_NOPE]).astype(BF16)
        kc_ref[:, lo + MLA_NOPE:lo + MLA_QK_PAD] = kr


def _mla_proj_rows_kernel(h_ref, g_ref, wd_ref, qn_ref, kvn_ref, wuq_ref, wuk_ref, wuv_ref,
                          cos_ref, sin_ref, q_ref, kc_ref, v_ref):
    cq, ckv, kr = _mla_latents(h_ref, g_ref, wd_ref, qn_ref, kvn_ref)
    cos = cos_ref[...]
    sin = sin_ref[...]
    _store_keys(kc_ref, ckv, wuk_ref, _rope128(kr, cos, sin).astype(BF16))
    v_ref[...] = _dot(ckv, wuv_ref[...]).astype(BF16)
    for hd in range(MLA_HEADS):
        lo = hd * MLA_QK_PAD
        qh = _dot(cq, wuq_ref[:, lo:lo + MLA_QK_PAD])
        q_ref[:, lo:lo + MLA_NOPE] = (qh[:, :MLA_NOPE] * MLA_SCALE).astype(BF16)
        q_ref[:, lo + MLA_NOPE:lo + MLA_QK_PAD] = (
            _rope128(qh[:, MLA_NOPE:], cos, sin) * MLA_SCALE).astype(BF16)


def _mla_proj_cols_kernel(h_ref, g_ref, wd_ref, qn_ref, kvn_ref, wuqt_ref, wuk_ref, wuvt_ref,
                          cos_ref, sin_ref, cost_ref, sint_ref, qt_ref, kc_ref, vt_ref):
    cq, ckv, kr = _mla_latents(h_ref, g_ref, wd_ref, qn_ref, kvn_ref)
    _store_keys(kc_ref, ckv, wuk_ref, _rope128(kr, cos_ref[...], sin_ref[...]).astype(BF16))
    vt = _dot_t(wuvt_ref[...], ckv).astype(BF16)
    ones = jnp.ones((MLA_VX - MLA_V, vt.shape[1]), BF16)
    for hd in range(MLA_HEADS):
        vt_ref[hd * MLA_VX:hd * MLA_VX + MLA_V, :] = vt[hd * MLA_V:(hd + 1) * MLA_V]
        vt_ref[hd * MLA_VX + MLA_V:(hd + 1) * MLA_VX, :] = ones
    qt = _dot_t(wuqt_ref[...], cq)
    c = cost_ref[...]
    s = sint_ref[...]
    half = MLA_ROPE // 2
    for hd in range(MLA_HEADS):
        lo = hd * MLA_QK_PAD
        qt_ref[lo:lo + MLA_NOPE, :] = (qt[lo:lo + MLA_NOPE] * MLA_SCALE).astype(BF16)
        x1 = qt[lo + MLA_NOPE:lo + MLA_NOPE + half]
        x2 = qt[lo + MLA_NOPE + half:lo + MLA_NOPE + MLA_ROPE]
        qt_ref[lo + MLA_NOPE:lo + MLA_NOPE + half, :] = ((x1 * c - x2 * s) * MLA_SCALE).astype(BF16)
        qt_ref[lo + MLA_NOPE + half:lo + MLA_NOPE + MLA_ROPE, :] = (
            (x2 * c + x1 * s) * MLA_SCALE).astype(BF16)
        qt_ref[lo + MLA_NOPE + MLA_ROPE:lo + MLA_QK_PAD, :] = jnp.zeros(
            (MLA_QK_PAD - MLA_NOPE - MLA_ROPE, qt.shape[1]), BF16)


def _full_spec(a):
    return pl.BlockSpec(a.shape, lambda i: (0,) * a.ndim)


def _mla_proj_rows(h, g, w, cos, sin):
    rows = h.shape[0]
    wd, qn, kvn, wuq, wuk, wuv = w
    args = (h, g, wd, qn, kvn, wuq, wuk, wuv, cos, sin)
    widths = (MLA_HEADS * MLA_QK_PAD, MLA_HEADS * MLA_QK_PAD, MLA_HEADS * MLA_V)
    return pl.pallas_call(
        _mla_proj_rows_kernel,
        grid=(1,),
        in_specs=[_full_spec(a) for a in args],
        out_specs=[pl.BlockSpec((rows, n), lambda i: (0, 0)) for n in widths],
        out_shape=[jax.ShapeDtypeStruct((rows, n), BF16) for n in widths],
        compiler_params=_cparams(("arbitrary",)),
        name="mla_proj_rows",
    )(*args)


def _mla_proj_cols(h, g, w, tabs, tm, seq):
    rows = h.shape[0]
    wd, qn, kvn, wuqt, wuk, wuvt = w
    cos, sin, cost, sint = tabs
    nblk = seq // tm
    row = lambda width: pl.BlockSpec((tm, width), lambda i: (i, 0))
    col = lambda height: pl.BlockSpec((height, tm), lambda i: (0, i))
    tab = pl.BlockSpec((tm, 128), lambda i: (i % nblk, 0))
    tabt = pl.BlockSpec((MLA_ROPE // 2, tm), lambda i: (0, i % nblk))
    return pl.pallas_call(
        _mla_proj_cols_kernel,
        grid=(rows // tm,),
        in_specs=[row(D_MODEL)] + [_full_spec(a) for a in (g, wd, qn, kvn, wuqt, wuk, wuvt)]
        + [tab, tab, tabt, tabt],
        out_specs=[col(MLA_HEADS * MLA_QK_PAD), row(MLA_HEADS * MLA_QK_PAD), col(MLA_HEADS * MLA_VX)],
        out_shape=[jax.ShapeDtypeStruct((MLA_HEADS * MLA_QK_PAD, rows), BF16),
                   jax.ShapeDtypeStruct((rows, MLA_HEADS * MLA_QK_PAD), BF16),
                   jax.ShapeDtypeStruct((MLA_HEADS * MLA_VX, rows), BF16)],
        compiler_params=_cparams(("parallel",)),
        name="mla_proj_cols",
    )(h, g, wd, qn, kvn, wuqt, wuk, wuvt, cos, sin, cost, sint)


def _mla_attn_kernel(qt_ref, k_ref, vt_ref, qm_ref, km_ref, vm_ref, ki_ref, vti_ref,
                     o_ref, om_ref, mm_ref, lm_ref, am_ref, *, tk):
    i = pl.program_id(2)
    n_chunks = k_ref.shape[0] // tk
    qm = qm_ref[...]

    @pl.when(i == 0)
    def _():
        s = _dot_t(qm, km_ref[...])
        m = jnp.max(s, axis=-1, keepdims=True)
        p = jnp.exp2(s - m)
        mm_ref[...] = m
        lm_ref[...] = jnp.sum(p, axis=-1, keepdims=True)
        am_ref[...] = _dot(p.astype(BF16), vm_ref[...])

    qt = qt_ref[...]
    n = qt.shape[1]
    score = lambda k: _dot(k, qt)
    s_next = score(k_ref[0:tk, :])
    meta = {"s": _dot_t(qm, ki_ref[...])}

    def meta_softmax():
        m_old = mm_ref[...]
        m = jnp.maximum(m_old, jnp.max(meta["s"], axis=-1, keepdims=True))
        meta["m"] = m
        meta["alpha"] = jnp.exp2(m_old - m)
        meta["p"] = jnp.exp2(meta["s"] - m)

    def meta_values():
        meta["o"] = _dot_t(meta["p"].astype(BF16), vti_ref[0:MLA_V, :])

    def meta_update():
        mm_ref[...] = meta["m"]
        lm_ref[...] = meta["alpha"] * lm_ref[...] + jnp.sum(meta["p"], axis=-1, keepdims=True)
        am_ref[...] = meta["alpha"] * am_ref[...] + meta["o"]

    stages = [(min(1, n_chunks - 1), meta_softmax), (min(2, n_chunks - 1), meta_values),
              (min(4, n_chunks - 1), meta_update)]

    pad = 128 - N_META
    vmt = jnp.concatenate([vm_ref[...].astype(F32), jnp.zeros((pad, MLA_V), F32)], axis=0).T
    vmt = jnp.concatenate([vmt, jnp.ones((MLA_VX - MLA_V, 128), F32)], axis=0).astype(BF16)
    s = score(km_ref[...])
    m = jnp.max(s, axis=0, keepdims=True)
    p = jnp.exp2(s - m)
    pm = jnp.concatenate([p.astype(BF16), jnp.zeros((pad, n), BF16)], axis=0)
    o_prev = _dot(vmt, pm)
    acc = jnp.zeros((MLA_VX, n), F32)
    for c in range(n_chunks):
        s = s_next
        if c + 1 < n_chunks:
            s_next = score(k_ref[(c + 1) * tk:(c + 2) * tk, :])
        m_new = jnp.maximum(m, jnp.max(s, axis=0, keepdims=True))
        alpha = jnp.exp2(m - m_new)
        p = jnp.exp2(s - m_new)
        o_c = _dot(vt_ref[:, c * tk:(c + 1) * tk], p.astype(BF16))
        acc = alpha * (acc + o_prev)
        o_prev = o_c
        m = m_new
        for at, stage in stages:
            if at == c:
                stage()
    acc = acc + o_prev
    o_ref[...] = (acc[:MLA_V] * (1.0 / acc[MLA_V:MLA_V + 1])).T.astype(BF16)

    @pl.when(i == pl.num_programs(2) - 1)
    def _():
        om_ref[...] = (am_ref[...] / lm_ref[...]).astype(BF16)


def _mla_attn(qt, kc, vt, q_meta, kc_meta, v_meta, batch, tq, tk):
    rows = kc.shape[0]
    seq = rows // batch
    nq = seq // tq
    meta = lambda width: pl.BlockSpec((N_META, width), lambda s, h, i: (s, h))
    return pl.pallas_call(
        functools.partial(_mla_attn_kernel, tk=tk),
        grid=(batch, MLA_HEADS, nq),
        in_specs=[
            pl.BlockSpec((MLA_QK_PAD, tq), lambda s, h, i: (h, s * nq + i)),
            pl.BlockSpec((seq, MLA_QK_PAD), lambda s, h, i: (s, h)),
            pl.BlockSpec((MLA_VX, seq), lambda s, h, i: (h, s)),
            meta(MLA_QK_PAD), meta(MLA_QK_PAD), meta(MLA_V),
            pl.BlockSpec((tq, MLA_QK_PAD), lambda s, h, i: (s * nq + i, h)),
            pl.BlockSpec((MLA_VX, tq), lambda s, h, i: (h, s * nq + i)),
        ],
        out_specs=[pl.BlockSpec((tq, MLA_V), lambda s, h, i: (s * nq + i, h)), meta(MLA_V)],
        out_shape=[jax.ShapeDtypeStruct((rows, MLA_HEADS * MLA_V), BF16),
                   jax.ShapeDtypeStruct((batch * N_META, MLA_HEADS * MLA_V), BF16)],
        scratch_shapes=[pltpu.VMEM((N_META, 1), F32), pltpu.VMEM((N_META, 1), F32),
                        pltpu.VMEM((N_META, MLA_V), F32)],
        compiler_params=_cparams(("parallel", "parallel", "arbitrary")),
        name="mla_attn",
    )(qt, kc, vt, q_meta, kc_meta, v_meta, kc, vt)


def _swa_proj_kernel(h_ref, g_ref, w_ref, q_ref, k_ref, v_ref):
    xn = _rms(h_ref[...], g_ref[...]).astype(BF16)
    dq = SWA_HEADS * SWA_HEAD_DIM
    dk = SWA_KV_HEADS * SWA_HEAD_DIM
    scale = SWA_HEAD_DIM ** -0.5
    for kvh in range(SWA_KV_HEADS):
        lo = kvh * SWA_GROUP * SWA_HEAD_DIM
        qk = _dot(xn, w_ref[:, lo:lo + SWA_GROUP * SWA_HEAD_DIM]) * scale
        for g in range(SWA_GROUP):
            q_ref[kvh * SWA_GROUP + g] = qk[:, g * SWA_HEAD_DIM:(g + 1) * SWA_HEAD_DIM].astype(BF16)
    kv = _dot(xn, w_ref[:, dq:dq + 2 * dk])
    for kvh in range(SWA_KV_HEADS):
        k_ref[kvh] = kv[:, kvh * SWA_HEAD_DIM:(kvh + 1) * SWA_HEAD_DIM].astype(BF16)
        v_ref[kvh] = kv[:, dk + kvh * SWA_HEAD_DIM:dk + (kvh + 1) * SWA_HEAD_DIM].astype(BF16)


def _swa_proj(h, g, w, tm):
    rows = h.shape[0]
    hm = lambda n: pl.BlockSpec((n, tm, SWA_HEAD_DIM), lambda i: (0, i, 0))
    return pl.pallas_call(
        _swa_proj_kernel,
        grid=(rows // tm,),
        in_specs=[pl.BlockSpec((tm, D_MODEL), lambda i: (i, 0)),
                  pl.BlockSpec(g.shape, lambda i: (0, 0)),
                  pl.BlockSpec(w.shape, lambda i: (0, 0))],
        out_specs=[hm(SWA_HEADS), hm(SWA_KV_HEADS), hm(SWA_KV_HEADS)],
        out_shape=[jax.ShapeDtypeStruct((SWA_HEADS, rows, SWA_HEAD_DIM), BF16),
                   jax.ShapeDtypeStruct((SWA_KV_HEADS, rows, SWA_HEAD_DIM), BF16),
                   jax.ShapeDtypeStruct((SWA_KV_HEADS, rows, SWA_HEAD_DIM), BF16)],
        compiler_params=_cparams(("parallel",)),
        name="swa_proj",
    )(h, g, w)


def _t5_bucket_exact(rel):
    nb = N_BUCKETS // 2
    max_exact = nb // 2
    table = []
    for n in range(int(np.abs(rel).max()) + 1):
        if n < max_exact:
            table.append(n)
        else:
            k = 0
            while 64 * 2 ** (k + 1) <= n * n:
                k += 1
            table.append(min(max_exact + k, nb - 1))
    return (np.asarray(table, np.int32)[np.abs(rel)] + np.where(rel > 0, nb, 0)).astype(np.int32)


def _swa_tile_codes(qpos, band_start):
    nq = qpos.shape[0]
    codes = np.full((nq, SWA_KEYS), -1, np.int32)
    kpos_band = N_META + band_start + np.arange(SWA_BAND)
    rel = kpos_band[None, :] - qpos[:, None]
    ok = np.abs(rel) <= WINDOW
    codes[:, :SWA_BAND] = np.where(ok, _t5_bucket_exact(rel), -1)
    rel_m = np.arange(N_META)[None, :] - qpos[:, None]
    codes[:, SWA_BAND:SWA_BAND + N_META] = _t5_bucket_exact(rel_m)
    codes[:, SINK_COL] = N_BUCKETS
    return codes


def _swa_bias_kernel(code_ref, rb_ref, sink_ref, o_ref):
    hd = pl.program_id(1)
    code = code_ref[0]
    t = jnp.full(code.shape, NEG, F32)
    for b in range(N_BUCKETS):
        t = jnp.where(code == b, rb_ref[b, hd], t)
    t = jnp.where(code == N_BUCKETS, sink_ref[hd], t)
    o_ref[0, 0] = t


def _swa_bias(codes, rel_bias, sink):
    nv, nq, nk = codes.shape
    return pl.pallas_call(
        _swa_bias_kernel,
        grid=(nv, SWA_HEADS),
        in_specs=[pl.BlockSpec((1, nq, nk), lambda v, h: (v, 0, 0)),
                  pl.BlockSpec(memory_space=pltpu.SMEM),
                  pl.BlockSpec(memory_space=pltpu.SMEM)],
        out_specs=pl.BlockSpec((1, 1, nq, nk), lambda v, h: (v, h, 0, 0)),
        out_shape=jax.ShapeDtypeStruct((nv, SWA_HEADS, nq, nk), F32),
        compiler_params=_cparams(("parallel", "parallel")),
        name="swa_bias",
    )(codes, rel_bias, sink)


def _swa_attn_kernel(q_ref, k_ref, v_ref, km_ref, vm_ref, b_ref, o_ref, *, sub, real):
    nsub = q_ref.shape[1] // sub
    seq = k_ref.shape[1]
    n_blocks = seq // BLOCK
    pad = jnp.zeros((SWA_KEYS - SWA_BAND - N_META, SWA_HEAD_DIM), BF16)
    for j in range(nsub):
        if real:
            b = pl.program_id(2) * nsub + j
            start = pl.multiple_of(jnp.clip((b - 1) * BLOCK, 0, seq - SWA_BAND), BLOCK)
            variant = jnp.where(b == 0, 1, jnp.where(b == n_blocks - 1, 2, 0))
        else:
            start = 0
            variant = 0
        kt = jnp.concatenate([k_ref[0, pl.ds(start, SWA_BAND), :], km_ref[0], pad], axis=0)
        vt = jnp.concatenate([v_ref[0, pl.ds(start, SWA_BAND), :], vm_ref[0], pad], axis=0)
        q = q_ref[:, j * sub:(j + 1) * sub, :].reshape(SWA_GROUP * sub, SWA_HEAD_DIM)
        s = _dot_t(q, kt) + b_ref[variant].reshape(SWA_GROUP * sub, SWA_KEYS)
        m = jnp.max(s, axis=-1, keepdims=True)
        e = jnp.exp(s - m)
        l = jnp.sum(e, axis=-1, keepdims=True)
        o = _dot(e.astype(BF16), vt) / l
        o = o.reshape(SWA_GROUP, sub, SWA_HEAD_DIM)
        o_ref[j * sub:(j + 1) * sub, :] = jnp.concatenate(
            [o[g] for g in range(SWA_GROUP)], axis=1).astype(BF16)


def _swa_attn(q, k, v, k_meta, v_meta, bias, batch, tq, sub, real):
    rows = q.shape[1]
    seq = k.shape[1] // batch
    nq = rows // batch // tq
    nv = bias.shape[0]
    return pl.pallas_call(
        functools.partial(_swa_attn_kernel, sub=sub, real=real),
        grid=(SWA_KV_HEADS, batch, nq),
        in_specs=[
            pl.BlockSpec((SWA_GROUP, tq, SWA_HEAD_DIM), lambda kv, s, i: (kv, s * nq + i, 0)),
            pl.BlockSpec((1, seq, SWA_HEAD_DIM), lambda kv, s, i: (kv, s, 0)),
            pl.BlockSpec((1, seq, SWA_HEAD_DIM), lambda kv, s, i: (kv, s, 0)),
            pl.BlockSpec((1, N_META, SWA_HEAD_DIM), lambda kv, s, i: (kv, s, 0)),
            pl.BlockSpec((1, N_META, SWA_HEAD_DIM), lambda kv, s, i: (kv, s, 0)),
            pl.BlockSpec((nv, SWA_GROUP, sub, SWA_KEYS), lambda kv, s, i: (0, kv, 0, 0)),
        ],
        out_specs=pl.BlockSpec((tq, SWA_GROUP * SWA_HEAD_DIM), lambda kv, s, i: (s * nq + i, kv)),
        out_shape=jax.ShapeDtypeStruct((rows, SWA_HEADS * SWA_HEAD_DIM), BF16),
        compiler_params=_cparams(("parallel", "parallel", "arbitrary")),
        name="swa_attn",
    )(q, k, v, k_meta, v_meta, bias)


def _attn_out_kernel(o_ref, w_ref, g_ref, h_ref, out_ref):
    m = _dot(o_ref[...], w_ref[...])
    out_ref[...] = h_ref[...] + _rms(m, g_ref[...])


def _attn_out(o, w, g, h, tm):
    rows = h.shape[0]
    return pl.pallas_call(
        _attn_out_kernel,
        grid=(rows // tm,),
        in_specs=[pl.BlockSpec((tm, o.shape[1]), lambda i: (i, 0)),
                  pl.BlockSpec(w.shape, lambda i: (0, 0)),
                  pl.BlockSpec(g.shape, lambda i: (0, 0)),
                  pl.BlockSpec((tm, D_MODEL), lambda i: (i, 0))],
        out_specs=pl.BlockSpec((tm, D_MODEL), lambda i: (i, 0)),
        out_shape=jax.ShapeDtypeStruct((rows, D_MODEL), F32),
        compiler_params=_cparams(("parallel",)),
        name="attn_out",
    )(o, w, g, h)


def _mlp_kernel(h_ref, gpre_ref, win_ref, wout_ref, gpost_ref, out_ref, xn_ref):
    j = pl.program_id(1)

    @pl.when(j == 0)
    def _():
        xn_ref[...] = _rms(h_ref[...], gpre_ref[...]).astype(BF16)
        out_ref[...] = jnp.zeros_like(out_ref)

    u = jnp.maximum(_dot(xn_ref[...], win_ref[...]), 0.0)
    out_ref[...] += _dot((u * u).astype(BF16), wout_ref[...])

    @pl.when(j == pl.num_programs(1) - 1)
    def _():
        out_ref[...] = h_ref[...] + _rms(out_ref[...], gpost_ref[...])


def _mlp(h, gpre, w_in, w_out, gpost, tm, tf):
    rows = h.shape[0]
    return pl.pallas_call(
        _mlp_kernel,
        grid=(rows // tm, D_FF // tf),
        in_specs=[pl.BlockSpec((tm, D_MODEL), lambda i, j: (i, 0)),
                  pl.BlockSpec(gpre.shape, lambda i, j: (0, 0)),
                  pl.BlockSpec((D_MODEL, tf), lambda i, j: (0, j)),
                  pl.BlockSpec((tf, D_MODEL), lambda i, j: (j, 0)),
                  pl.BlockSpec(gpost.shape, lambda i, j: (0, 0))],
        out_specs=pl.BlockSpec((tm, D_MODEL), lambda i, j: (i, 0)),
        out_shape=jax.ShapeDtypeStruct((rows, D_MODEL), F32),
        scratch_shapes=[pltpu.VMEM((tm, D_MODEL), BF16)],
        compiler_params=_cparams(("parallel", "arbitrary")),
        name="mlp",
    )(h, gpre, w_in, w_out, gpost)


def _rope_dup(w):
    return jnp.concatenate([w, w], axis=-1)


def _prep_mla(w_dq, q_norm, w_uq, w_dkv, kv_norm, w_ukv):
    wd = jnp.concatenate([w_dq, w_dkv[:, :MLA_KV_LORA], _rope_dup(w_dkv[:, MLA_KV_LORA:])], axis=1)
    uq = w_uq.reshape(MLA_Q_LORA, MLA_HEADS, MLA_NOPE + MLA_ROPE)
    wuq = jnp.concatenate([uq[..., :MLA_NOPE], _rope_dup(uq[..., MLA_NOPE:])], axis=-1)
    wuq = wuq.reshape(MLA_Q_LORA, MLA_HEADS * MLA_QK_PAD)
    ukv = w_ukv.reshape(MLA_KV_LORA, MLA_HEADS, MLA_NOPE + MLA_V)
    wuk = ukv[..., :MLA_NOPE].reshape(MLA_KV_LORA, MLA_HEADS * MLA_NOPE)
    wuv = ukv[..., MLA_NOPE:].reshape(MLA_KV_LORA, MLA_HEADS * MLA_V)
    wd, wuq, wuk, wuv = (t.astype(BF16) for t in (wd, wuq, wuk, wuv))
    rows = (wd, q_norm[None], kv_norm[None], wuq, wuk, wuv)
    cols = (wd, q_norm[None], kv_norm[None], wuq.T, wuk, wuv.T)
    return rows, cols


def _rope_tables(pos):
    half = MLA_ROPE // 2
    inv = ROPE_THETA ** (-(jnp.arange(half, dtype=F32) / half))
    ang = pos.astype(F32)[:, None] * inv[None, :]
    cos, sin = jnp.cos(ang), jnp.sin(ang)
    zero = jnp.zeros((pos.shape[0], 128 - MLA_ROPE), F32)
    return (jnp.concatenate([cos, cos, zero], axis=1), jnp.concatenate([-sin, sin, zero], axis=1),
            cos.T, sin.T)


def _trunk(x, meta_tokens, p):
    batch, seq, _ = x.shape
    h = x.reshape(batch * seq, D_MODEL)
    hm = jnp.broadcast_to(meta_tokens[None], (batch, N_META, D_MODEL)).reshape(batch * N_META, D_MODEL)
    rows_m = batch * N_META
    tq = 512
    tabs_r = _rope_tables(N_META + jnp.arange(seq))
    cos_m, sin_m = (jnp.tile(t, (batch, 1)) for t in _rope_tables(jnp.arange(N_META))[:2])
    for i in range(DEPTH):
        j = i // 2
        g_pre = p["norm_mix_pre"][i][None]
        g_post = p["norm_mix_post"][i][None]
        if i % 2 == 0:
            w_rows, w_cols = p["mla"][j]
            qt, kc, vt = _mla_proj_cols(h, g_pre, w_cols, tabs_r, 256, seq)
            q_m, kc_m, v_m = _mla_proj_rows(hm, g_pre, w_rows, cos_m, sin_m)
            o, o_m = _mla_attn(qt, kc, vt, q_m, kc_m, v_m, batch, min(2048, seq), 512)
            w_o = p["mla_w_o"][j]
        else:
            q, k, v = _swa_proj(h, g_pre, p["swa_w_qkv"][j], 512)
            q_m, k_m, v_m = _swa_proj(hm, g_pre, p["swa_w_qkv"][j], rows_m)
            o = _swa_attn(q, k, v, k_m, v_m, p["swa_bias"][j], batch, tq, BLOCK, True)
            o_m = _swa_attn(q_m, k, v, k_m, v_m, p["swa_bias_meta"][j], batch, N_META, N_META, False)
            w_o = p["swa_w_o"][j]
        h = _attn_out(o, w_o, g_post, h, 512)
        hm = _attn_out(o_m, w_o, g_post, hm, rows_m)
        mlp_args = (p["norm_mlp_pre"][i][None], p["mlp_w_in"][i], p["mlp_w_out"][i],
                    p["norm_mlp_post"][i][None])
        h = _mlp(h, *mlp_args, min(1024, h.shape[0]), 512)
        hm = _mlp(hm, *mlp_args, rows_m, 512)
    return h.reshape(batch, seq, D_MODEL)


def kernel(x_prompt, x_sample, meta_tokens, rel_bias, mla_w_dq, mla_q_norm, mla_w_uq, mla_w_dkv, mla_kv_norm, mla_w_ukv, mla_w_o, swa_w_qkv, swa_w_o, swa_sink, mlp_w_in, mlp_w_out, norm_mix_pre, norm_mix_post, norm_mlp_pre, norm_mlp_post):
    seq = x_prompt.shape[1]
    assert x_sample.shape[1] == seq and seq % 512 == 0
    n_mla = mla_w_dq.shape[0]
    n_swa = swa_w_qkv.shape[0]
    qpos = N_META + np.arange(BLOCK)
    codes = np.stack([
        _swa_tile_codes(qpos + BLOCK, 0),
        _swa_tile_codes(qpos, 0),
        _swa_tile_codes(qpos + 2 * BLOCK, 0),
    ])
    codes_meta = _swa_tile_codes(np.arange(N_META), 0)[None]
    p = {
        "mla": [_prep_mla(mla_w_dq[j], mla_q_norm[j], mla_w_uq[j], mla_w_dkv[j], mla_kv_norm[j],
                          mla_w_ukv[j]) for j in range(n_mla)],
        "mla_w_o": mla_w_o.astype(BF16),
        "swa_w_qkv": swa_w_qkv.astype(BF16),
        "swa_w_o": swa_w_o.astype(BF16),
        "swa_bias": [_swa_bias(jnp.asarray(codes), rel_bias, swa_sink[j]) for j in range(n_swa)],
        "swa_bias_meta": [_swa_bias(jnp.asarray(codes_meta), rel_bias, swa_sink[j])
                          for j in range(n_swa)],
        "mlp_w_in": mlp_w_in.astype(BF16),
        "mlp_w_out": mlp_w_out.astype(BF16),
        "norm_mix_pre": norm_mix_pre, "norm_mix_post": norm_mix_post,
        "norm_mlp_pre": norm_mlp_pre, "norm_mlp_post": norm_mlp_post,
    }
    return (_trunk(x_prompt, meta_tokens, p), _trunk(x_sample, meta_tokens, p))
```

```python
import functools
import math

import jax
import jax.numpy as jnp
import numpy as np
from jax import lax
from jax.experimental import pallas as pl
from jax.experimental.pallas import tpu as pltpu

D_MODEL = 2048
DEPTH = 4
N_META = 16
MLA_HEADS = 16
MLA_Q_LORA = 512
MLA_KV_LORA = 512
MLA_NOPE = 128
MLA_ROPE = 64
MLA_V = 128
MLA_VX = MLA_V + 16
MLA_QK_PAD = 256
MLA_STREAMS = 2
ROPE_THETA = 10000.0
SWA_HEADS = 32
SWA_KV_HEADS = 4
SWA_GROUP = SWA_HEADS // SWA_KV_HEADS
SWA_HEAD_DIM = 64
WINDOW = 128
BLOCK = 128
N_BUCKETS = 32
MAX_DISTANCE = 128
D_FF = 4 * D_MODEL
EPS = 1e-6

SWA_BAND = 3 * BLOCK
SWA_KEYS = 512
SINK_COL = SWA_BAND + N_META
NEG = -1e30

VMEM_LIMIT_V7X = 56 * 1024 * 1024

BF16 = jnp.bfloat16
F32 = jnp.float32


def _cparams(sem):
    return pltpu.CompilerParams(dimension_semantics=sem, vmem_limit_bytes=VMEM_LIMIT_V7X)


def _rms(x, g):
    r = lax.rsqrt(jnp.mean(x * x, axis=-1, keepdims=True) + EPS)
    return x * r * g


def _dot(a, b):
    return jnp.dot(a, b, preferred_element_type=F32)


def _dot_t(a, b):
    return lax.dot_general(a, b, (((1,), (1,)), ((), ())), preferred_element_type=F32)


def _rope128(y, cos, sin):
    return y * cos + pltpu.roll(y, 32, axis=1) * sin


MLA_SCALE = (MLA_NOPE + MLA_ROPE) ** -0.5 * math.log2(math.e)


def _mla_latents(h_ref, g_ref, wd_ref, qn_ref, kvn_ref):
    xn = _rms(h_ref[...], g_ref[...]).astype(BF16)
    a = _dot(xn, wd_ref[...])
    cq = _rms(a[:, :MLA_Q_LORA], qn_ref[...]).astype(BF16)
    ckv = _rms(a[:, MLA_Q_LORA:MLA_Q_LORA + MLA_KV_LORA], kvn_ref[...]).astype(BF16)
    return cq, ckv, a[:, MLA_Q_LORA + MLA_KV_LORA:]


def _store_keys(kc_ref, ckv, wuk_ref, kr):
    for hd in range(MLA_HEADS):
        lo = hd * MLA_QK_PAD
        kc_ref[:, lo:lo + MLA_NOPE] = _dot(
            ckv, wuk_ref[:, hd * MLA_NOPE:(hd + 1) * MLA_NOPE]).astype(BF16)
        kc_ref[:, lo + MLA_NOPE:lo + MLA_QK_PAD] = kr


def _mla_proj_rows_kernel(h_ref, g_ref, wd_ref, qn_ref, kvn_ref, wuq_ref, wuk_ref, wuv_ref,
                          cos_ref, sin_ref, q_ref, kc_ref, v_ref):
    cq, ckv, kr = _mla_latents(h_ref, g_ref, wd_ref, qn_ref, kvn_ref)
    cos = cos_ref[...]
    sin = sin_ref[...]
    _store_keys(kc_ref, ckv, wuk_ref, _rope128(kr, cos, sin).astype(BF16))
    v_ref[...] = _dot(ckv, wuv_ref[...]).astype(BF16)
    for hd in range(MLA_HEADS):
        lo = hd * MLA_QK_PAD
        qh = _dot(cq, wuq_ref[:, lo:lo + MLA_QK_PAD])
        q_ref[:, lo:lo + MLA_NOPE] = (qh[:, :MLA_NOPE] * MLA_SCALE).astype(BF16)
        q_ref[:, lo + MLA_NOPE:lo + MLA_QK_PAD] = (
            _rope128(qh[:, MLA_NOPE:], cos, sin) * MLA_SCALE).astype(BF16)


def _mla_proj_cols_kernel(h_ref, g_ref, wd_ref, qn_ref, kvn_ref, wuqt_ref, wuk_ref, wuvt_ref,
                          cos_ref, sin_ref, cost_ref, sint_ref, qt_ref, kc_ref, vt_ref):
    cq, ckv, kr = _mla_latents(h_ref, g_ref, wd_ref, qn_ref, kvn_ref)
    _store_keys(kc_ref, ckv, wuk_ref, _rope128(kr, cos_ref[...], sin_ref[...]).astype(BF16))
    vt = _dot_t(wuvt_ref[...], ckv).astype(BF16)
    ones = jnp.ones((MLA_VX - MLA_V, vt.shape[1]), BF16)
    for hd in range(MLA_HEADS):
        vt_ref[hd * MLA_VX:hd * MLA_VX + MLA_V, :] = vt[hd * MLA_V:(hd + 1) * MLA_V]
        vt_ref[hd * MLA_VX + MLA_V:(hd + 1) * MLA_VX, :] = ones
    qt = _dot_t(wuqt_ref[...], cq)
    c = cost_ref[...]
    s = sint_ref[...]
    half = MLA_ROPE // 2
    for hd in range(MLA_HEADS):
        lo = hd * MLA_QK_PAD
        qt_ref[lo:lo + MLA_NOPE, :] = (qt[lo:lo + MLA_NOPE] * MLA_SCALE).astype(BF16)
        x1 = qt[lo + MLA_NOPE:lo + MLA_NOPE + half]
        x2 = qt[lo + MLA_NOPE + half:lo + MLA_NOPE + MLA_ROPE]
        qt_ref[lo + MLA_NOPE:lo + MLA_NOPE + half, :] = ((x1 * c - x2 * s) * MLA_SCALE).astype(BF16)
        qt_ref[lo + MLA_NOPE + half:lo + MLA_NOPE + MLA_ROPE, :] = (
            (x2 * c + x1 * s) * MLA_SCALE).astype(BF16)
        qt_ref[lo + MLA_NOPE + MLA_ROPE:lo + MLA_QK_PAD, :] = jnp.zeros(
            (MLA_QK_PAD - MLA_NOPE - MLA_ROPE, qt.shape[1]), BF16)


def _full_spec(a):
    return pl.BlockSpec(a.shape, lambda i: (0,) * a.ndim)


def _mla_proj_rows(h, g, w, cos, sin):
    rows = h.shape[0]
    wd, qn, kvn, wuq, wuk, wuv = w
    args = (h, g, wd, qn, kvn, wuq, wuk, wuv, cos, sin)
    widths = (MLA_HEADS * MLA_QK_PAD, MLA_HEADS * MLA_QK_PAD, MLA_HEADS * MLA_V)
    return pl.pallas_call(
        _mla_proj_rows_kernel,
        grid=(1,),
        in_specs=[_full_spec(a) for a in args],
        out_specs=[pl.BlockSpec((rows, n), lambda i: (0, 0)) for n in widths],
        out_shape=[jax.ShapeDtypeStruct((rows, n), BF16) for n in widths],
        compiler_params=_cparams(("arbitrary",)),
        name="mla_proj_rows",
    )(*args)


def _mla_proj_cols(h, g, w, tabs, tm, seq):
    rows = h.shape[0]
    wd, qn, kvn, wuqt, wuk, wuvt = w
    cos, sin, cost, sint = tabs
    nblk = seq // tm
    row = lambda width: pl.BlockSpec((tm, width), lambda i: (i, 0))
    col = lambda height: pl.BlockSpec((height, tm), lambda i: (0, i))
    tab = pl.BlockSpec((tm, 128), lambda i: (i % nblk, 0))
    tabt = pl.BlockSpec((MLA_ROPE // 2, tm), lambda i: (0, i % nblk))
    return pl.pallas_call(
        _mla_proj_cols_kernel,
        grid=(rows // tm,),
        in_specs=[row(D_MODEL)] + [_full_spec(a) for a in (g, wd, qn, kvn, wuqt, wuk, wuvt)]
        + [tab, tab, tabt, tabt],
        out_specs=[col(MLA_HEADS * MLA_QK_PAD), row(MLA_HEADS * MLA_QK_PAD), col(MLA_HEADS * MLA_VX)],
        out_shape=[jax.ShapeDtypeStruct((MLA_HEADS * MLA_QK_PAD, rows), BF16),
                   jax.ShapeDtypeStruct((rows, MLA_HEADS * MLA_QK_PAD), BF16),
                   jax.ShapeDtypeStruct((MLA_HEADS * MLA_VX, rows), BF16)],
        compiler_params=_cparams(("parallel",)),
        name="mla_proj_cols",
    )(h, g, wd, qn, kvn, wuqt, wuk, wuvt, cos, sin, cost, sint)


def _mla_attn_kernel(qt_ref, k_ref, vt_ref, qm_ref, km_ref, vm_ref, ki_ref, vti_ref,
                     o_ref, om_ref, mm_ref, lm_ref, am_ref, *, tk):
    i = pl.program_id(2)
    n_chunks = k_ref.shape[0] // tk
    qm = qm_ref[...]

    @pl.when(i == 0)
    def _():
        s = _dot_t(qm, km_ref[...])
        m = jnp.max(s, axis=-1, keepdims=True)
        p = jnp.exp2(s - m)
        mm_ref[...] = m
        lm_ref[...] = jnp.sum(p, axis=-1, keepdims=True)
        am_ref[...] = _dot(p.astype(BF16), vm_ref[...])

    n = qt_ref.shape[1] // MLA_STREAMS
    qts = [qt_ref[:, h * n:(h + 1) * n] for h in range(MLA_STREAMS)]
    score = lambda h, k: _dot(k, qts[h])
    meta = {"s": _dot_t(qm, ki_ref[...])}

    def meta_softmax():
        m_old = mm_ref[...]
        m = jnp.maximum(m_old, jnp.max(meta["s"], axis=-1, keepdims=True))
        meta["m"] = m
        meta["alpha"] = jnp.exp2(m_old - m)
        meta["p"] = jnp.exp2(meta["s"] - m)

    def meta_values():
        meta["o"] = _dot_t(meta["p"].astype(BF16), vti_ref[0:MLA_V, :])

    def meta_update():
        mm_ref[...] = meta["m"]
        lm_ref[...] = meta["alpha"] * lm_ref[...] + jnp.sum(meta["p"], axis=-1, keepdims=True)
        am_ref[...] = meta["alpha"] * am_ref[...] + meta["o"]

    stages = [(min(1, n_chunks - 1), meta_softmax), (min(2, n_chunks - 1), meta_values),
              (min(4, n_chunks - 1), meta_update)]

    pad = 128 - N_META
    vmt = jnp.concatenate([vm_ref[...].astype(F32), jnp.zeros((pad, MLA_V), F32)], axis=0).T
    vmt = jnp.concatenate([vmt, jnp.ones((MLA_VX - MLA_V, 128), F32)], axis=0).astype(BF16)
    km = km_ref[...]
    st = []
    for h in range(MLA_STREAMS):
        s = score(h, km)
        m = jnp.max(s, axis=0, keepdims=True)
        p = jnp.exp2(s - m)
        pm = jnp.concatenate([p.astype(BF16), jnp.zeros((pad, n), BF16)], axis=0)
        st.append({"m": m, "o_prev": _dot(vmt, pm), "acc": jnp.zeros((MLA_VX, n), F32),
                   "s_next": score(h, k_ref[0:tk, :])})
    for c in range(n_chunks):
        for h in range(MLA_STREAMS):
            t = st[h]
            s = t["s_next"]
            if c + 1 < n_chunks:
                t["s_next"] = score(h, k_ref[(c + 1) * tk:(c + 2) * tk, :])
            m_new = jnp.maximum(t["m"], jnp.max(s, axis=0, keepdims=True))
            alpha = jnp.exp2(t["m"] - m_new)
            p = jnp.exp2(s - m_new)
            o_c = _dot(vt_ref[:, c * tk:(c + 1) * tk], p.astype(BF16))
            t["acc"] = alpha * (t["acc"] + t["o_prev"])
            t["o_prev"] = o_c
            t["m"] = m_new
        for at, stage in stages:
            if at == c:
                stage()
    for h in range(MLA_STREAMS):
        acc = st[h]["acc"] + st[h]["o_prev"]
        o_ref[h * n:(h + 1) * n, :] = (
            acc[:MLA_V] * (1.0 / acc[MLA_V:MLA_V + 1])).T.astype(BF16)

    @pl.when(i == pl.num_programs(2) - 1)
    def _():
        om_ref[...] = (am_ref[...] / lm_ref[...]).astype(BF16)


def _mla_attn(qt, kc, vt, q_meta, kc_meta, v_meta, batch, tq, tk):
    rows = kc.shape[0]
    seq = rows // batch
    nq = seq // tq
    meta = lambda width: pl.BlockSpec((N_META, width), lambda s, h, i: (s, h))
    return pl.pallas_call(
        functools.partial(_mla_attn_kernel, tk=tk),
        grid=(batch, MLA_HEADS, nq),
        in_specs=[
            pl.BlockSpec((MLA_QK_PAD, tq), lambda s, h, i: (h, s * nq + i)),
            pl.BlockSpec((seq, MLA_QK_PAD), lambda s, h, i: (s, h)),
            pl.BlockSpec((MLA_VX, seq), lambda s, h, i: (h, s)),
            meta(MLA_QK_PAD), meta(MLA_QK_PAD), meta(MLA_V),
            pl.BlockSpec((tq, MLA_QK_PAD), lambda s, h, i: (s * nq + i, h)),
            pl.BlockSpec((MLA_VX, tq), lambda s, h, i: (h, s * nq + i)),
        ],
        out_specs=[pl.BlockSpec((tq, MLA_V), lambda s, h, i: (s * nq + i, h)), meta(MLA_V)],
        out_shape=[jax.ShapeDtypeStruct((rows, MLA_HEADS * MLA_V), BF16),
                   jax.ShapeDtypeStruct((batch * N_META, MLA_HEADS * MLA_V), BF16)],
        scratch_shapes=[pltpu.VMEM((N_META, 1), F32), pltpu.VMEM((N_META, 1), F32),
                        pltpu.VMEM((N_META, MLA_V), F32)],
        compiler_params=_cparams(("parallel", "parallel", "arbitrary")),
        name="mla_attn",
    )(qt, kc, vt, q_meta, kc_meta, v_meta, kc, vt)


def _swa_proj_kernel(h_ref, g_ref, w_ref, q_ref, k_ref, v_ref):
    xn = _rms(h_ref[...], g_ref[...]).astype(BF16)
    dq = SWA_HEADS * SWA_HEAD_DIM
    dk = SWA_KV_HEADS * SWA_HEAD_DIM
    scale = SWA_HEAD_DIM ** -0.5
    for kvh in range(SWA_KV_HEADS):
        lo = kvh * SWA_GROUP * SWA_HEAD_DIM
        qk = _dot(xn, w_ref[:, lo:lo + SWA_GROUP * SWA_HEAD_DIM]) * scale
        for g in range(SWA_GROUP):
            q_ref[kvh * SWA_GROUP + g] = qk[:, g * SWA_HEAD_DIM:(g + 1) * SWA_HEAD_DIM].astype(BF16)
    kv = _dot(xn, w_ref[:, dq:dq + 2 * dk])
    for kvh in range(SWA_KV_HEADS):
        k_ref[kvh] = kv[:, kvh * SWA_HEAD_DIM:(kvh + 1) * SWA_HEAD_DIM].astype(BF16)
        v_ref[kvh] = kv[:, dk + kvh * SWA_HEAD_DIM:dk + (kvh + 1) * SWA_HEAD_DIM].astype(BF16)


def _swa_proj(h, g, w, tm):
    rows = h.shape[0]
    hm = lambda n: pl.BlockSpec((n, tm, SWA_HEAD_DIM), lambda i: (0, i, 0))
    return pl.pallas_call(
        _swa_proj_kernel,
        grid=(rows // tm,),
        in_specs=[pl.BlockSpec((tm, D_MODEL), lambda i: (i, 0)),
                  pl.BlockSpec(g.shape, lambda i: (0, 0)),
                  pl.BlockSpec(w.shape, lambda i: (0, 0))],
        out_specs=[hm(SWA_HEADS), hm(SWA_KV_HEADS), hm(SWA_KV_HEADS)],
        out_shape=[jax.ShapeDtypeStruct((SWA_HEADS, rows, SWA_HEAD_DIM), BF16),
                   jax.ShapeDtypeStruct((SWA_KV_HEADS, rows, SWA_HEAD_DIM), BF16),
                   jax.ShapeDtypeStruct((SWA_KV_HEADS, rows, SWA_HEAD_DIM), BF16)],
        compiler_params=_cparams(("parallel",)),
        name="swa_proj",
    )(h, g, w)


def _t5_bucket_exact(rel):
    nb = N_BUCKETS // 2
    max_exact = nb // 2
    table = []
    for n in range(int(np.abs(rel).max()) + 1):
        if n < max_exact:
            table.append(n)
        else:
            k = 0
            while 64 * 2 ** (k + 1) <= n * n:
                k += 1
            table.append(min(max_exact + k, nb - 1))
    return (np.asarray(table, np.int32)[np.abs(rel)] + np.where(rel > 0, nb, 0)).astype(np.int32)


def _swa_tile_codes(qpos, band_start):
    nq = qpos.shape[0]
    codes = np.full((nq, SWA_KEYS), -1, np.int32)
    kpos_band = N_META + band_start + np.arange(SWA_BAND)
    rel = kpos_band[None, :] - qpos[:, None]
    ok = np.abs(rel) <= WINDOW
    codes[:, :SWA_BAND] = np.where(ok, _t5_bucket_exact(rel), -1)
    rel_m = np.arange(N_META)[None, :] - qpos[:, None]
    codes[:, SWA_BAND:SWA_BAND + N_META] = _t5_bucket_exact(rel_m)
    codes[:, SINK_COL] = N_BUCKETS
    return codes


def _swa_bias_kernel(code_ref, rb_ref, sink_ref, o_ref):
    hd = pl.program_id(1)
    code = code_ref[0]
    t = jnp.full(code.shape, NEG, F32)
    for b in range(N_BUCKETS):
        t = jnp.where(code == b, rb_ref[b, hd], t)
    t = jnp.where(code == N_BUCKETS, sink_ref[hd], t)
    o_ref[0, 0] = t


def _swa_bias(codes, rel_bias, sink):
    nv, nq, nk = codes.shape
    return pl.pallas_call(
        _swa_bias_kernel,
        grid=(nv, SWA_HEADS),
        in_specs=[pl.BlockSpec((1, nq, nk), lambda v, h: (v, 0, 0)),
                  pl.BlockSpec(memory_space=pltpu.SMEM),
                  pl.BlockSpec(memory_space=pltpu.SMEM)],
        out_specs=pl.BlockSpec((1, 1, nq, nk), lambda v, h: (v, h, 0, 0)),
        out_shape=jax.ShapeDtypeStruct((nv, SWA_HEADS, nq, nk), F32),
        compiler_params=_cparams(("parallel", "parallel")),
        name="swa_bias",
    )(codes, rel_bias, sink)


def _swa_attn_kernel(q_ref, k_ref, v_ref, km_ref, vm_ref, b_ref, o_ref, *, sub, real):
    nsub = q_ref.shape[1] // sub
    seq = k_ref.shape[1]
    n_blocks = seq // BLOCK
    pad = jnp.zeros((SWA_KEYS - SWA_BAND - N_META, SWA_HEAD_DIM), BF16)
    for j in range(nsub):
        if real:
            b = pl.program_id(2) * nsub + j
            start = pl.multiple_of(jnp.clip((b - 1) * BLOCK, 0, seq - SWA_BAND), BLOCK)
            variant = jnp.where(b == 0, 1, jnp.where(b == n_blocks - 1, 2, 0))
        else:
            start = 0
            variant = 0
        kt = jnp.concatenate([k_ref[0, pl.ds(start, SWA_BAND), :], km_ref[0], pad], axis=0)
        vt = jnp.concatenate([v_ref[0, pl.ds(start, SWA_BAND), :], vm_ref[0], pad], axis=0)
        q = q_ref[:, j * sub:(j + 1) * sub, :].reshape(SWA_GROUP * sub, SWA_HEAD_DIM)
        s = _dot_t(q, kt) + b_ref[variant].reshape(SWA_GROUP * sub, SWA_KEYS)
        m = jnp.max(s, axis=-1, keepdims=True)
        e = jnp.exp(s - m)
        l = jnp.sum(e, axis=-1, keepdims=True)
        o = _dot(e.astype(BF16), vt) / l
        o = o.reshape(SWA_GROUP, sub, SWA_HEAD_DIM)
        o_ref[j * sub:(j + 1) * sub, :] = jnp.concatenate(
            [o[g] for g in range(SWA_GROUP)], axis=1).astype(BF16)


def _swa_attn(q, k, v, k_meta, v_meta, bias, batch, tq, sub, real):
    rows = q.shape[1]
    seq = k.shape[1] // batch
    nq = rows // batch // tq
    nv = bias.shape[0]
    return pl.pallas_call(
        functools.partial(_swa_attn_kernel, sub=sub, real=real),
        grid=(SWA_KV_HEADS, batch, nq),
        in_specs=[
            pl.BlockSpec((SWA_GROUP, tq, SWA_HEAD_DIM), lambda kv, s, i: (kv, s * nq + i, 0)),
            pl.BlockSpec((1, seq, SWA_HEAD_DIM), lambda kv, s, i: (kv, s, 0)),
            pl.BlockSpec((1, seq, SWA_HEAD_DIM), lambda kv, s, i: (kv, s, 0)),
            pl.BlockSpec((1, N_META, SWA_HEAD_DIM), lambda kv, s, i: (kv, s, 0)),
            pl.BlockSpec((1, N_META, SWA_HEAD_DIM), lambda kv, s, i: (kv, s, 0)),
            pl.BlockSpec((nv, SWA_GROUP, sub, SWA_KEYS), lambda kv, s, i: (0, kv, 0, 0)),
        ],
        out_specs=pl.BlockSpec((tq, SWA_GROUP * SWA_HEAD_DIM), lambda kv, s, i: (s * nq + i, kv)),
        out_shape=jax.ShapeDtypeStruct((rows, SWA_HEADS * SWA_HEAD_DIM), BF16),
        compiler_params=_cparams(("parallel", "parallel", "arbitrary")),
        name="swa_attn",
    )(q, k, v, k_meta, v_meta, bias)


def _attn_out_kernel(o_ref, w_ref, g_ref, h_ref, out_ref):
    m = _dot(o_ref[...], w_ref[...])
    out_ref[...] = h_ref[...] + _rms(m, g_ref[...])


def _attn_out(o, w, g, h, tm):
    rows = h.shape[0]
    return pl.pallas_call(
        _attn_out_kernel,
        grid=(rows // tm,),
        in_specs=[pl.BlockSpec((tm, o.shape[1]), lambda i: (i, 0)),
                  pl.BlockSpec(w.shape, lambda i: (0, 0)),
                  pl.BlockSpec(g.shape, lambda i: (0, 0)),
                  pl.BlockSpec((tm, D_MODEL), lambda i: (i, 0))],
        out_specs=pl.BlockSpec((tm, D_MODEL), lambda i: (i, 0)),
        out_shape=jax.ShapeDtypeStruct((rows, D_MODEL), F32),
        compiler_params=_cparams(("parallel",)),
        name="attn_out",
    )(o, w, g, h)


def _mlp_kernel(h_ref, gpre_ref, win_ref, wout_ref, gpost_ref, out_ref, xn_ref):
    j = pl.program_id(1)
    last = pl.num_programs(1) - 1

    def ffn(xn):
        u = jnp.maximum(_dot(xn, win_ref[...]), 0.0)
        return _dot((u * u).astype(BF16), wout_ref[...])

    @pl.when(j == 0)
    def _():
        xn = _rms(h_ref[...], gpre_ref[...]).astype(BF16)
        xn_ref[...] = xn
        out_ref[...] = ffn(xn)

    @pl.when(jnp.logical_and(j > 0, j < last))
    def _():
        out_ref[...] += ffn(xn_ref[...])

    @pl.when(j == last)
    def _():
        f = out_ref[...] + ffn(xn_ref[...])
        out_ref[...] = h_ref[...] + _rms(f, gpost_ref[...])


def _mlp(h, gpre, w_in, w_out, gpost, tm, tf):
    rows = h.shape[0]
    return pl.pallas_call(
        _mlp_kernel,
        grid=(rows // tm, D_FF // tf),
        in_specs=[pl.BlockSpec((tm, D_MODEL), lambda i, j: (i, 0)),
                  pl.BlockSpec(gpre.shape, lambda i, j: (0, 0)),
                  pl.BlockSpec((D_MODEL, tf), lambda i, j: (0, j)),
                  pl.BlockSpec((tf, D_MODEL), lambda i, j: (j, 0)),
                  pl.BlockSpec(gpost.shape, lambda i, j: (0, 0))],
        out_specs=pl.BlockSpec((tm, D_MODEL), lambda i, j: (i, 0)),
        out_shape=jax.ShapeDtypeStruct((rows, D_MODEL), F32),
        scratch_shapes=[pltpu.VMEM((tm, D_MODEL), BF16)],
        compiler_params=_cparams(("parallel", "arbitrary")),
        name="mlp",
    )(h, gpre, w_in, w_out, gpost)


def _rope_dup(w):
    return jnp.concatenate([w, w], axis=-1)


def _prep_mla(w_dq, q_norm, w_uq, w_dkv, kv_norm, w_ukv):
    wd = jnp.concatenate([w_dq, w_dkv[:, :MLA_KV_LORA], _rope_dup(w_dkv[:, MLA_KV_LORA:])], axis=1)
    uq = w_uq.reshape(MLA_Q_LORA, MLA_HEADS, MLA_NOPE + MLA_ROPE)
    wuq = jnp.concatenate([uq[..., :MLA_NOPE], _rope_dup(uq[..., MLA_NOPE:])], axis=-1)
    wuq = wuq.reshape(MLA_Q_LORA, MLA_HEADS * MLA_QK_PAD)
    ukv = w_ukv.reshape(MLA_KV_LORA, MLA_HEADS, MLA_NOPE + MLA_V)
    wuk = ukv[..., :MLA_NOPE].reshape(MLA_KV_LORA, MLA_HEADS * MLA_NOPE)
    wuv = ukv[..., MLA_NOPE:].reshape(MLA_KV_LORA, MLA_HEADS * MLA_V)
    wd, wuq, wuk, wuv = (t.astype(BF16) for t in (wd, wuq, wuk, wuv))
    rows = (wd, q_norm[None], kv_norm[None], wuq, wuk, wuv)
    cols = (wd, q_norm[None], kv_norm[None], wuq.T, wuk, wuv.T)
    return rows, cols


def _rope_tables(pos):
    half = MLA_ROPE // 2
    inv = ROPE_THETA ** (-(jnp.arange(half, dtype=F32) / half))
    ang = pos.astype(F32)[:, None] * inv[None, :]
    cos, sin = jnp.cos(ang), jnp.sin(ang)
    zero = jnp.zeros((pos.shape[0], 128 - MLA_ROPE), F32)
    return (jnp.concatenate([cos, cos, zero], axis=1), jnp.concatenate([-sin, sin, zero], axis=1),
            cos.T, sin.T)


def _trunk(x, meta_tokens, p):
    batch, seq, _ = x.shape
    h = x.reshape(batch * seq, D_MODEL)
    hm = jnp.broadcast_to(meta_tokens[None], (batch, N_META, D_MODEL)).reshape(batch * N_META, D_MODEL)
    rows_m = batch * N_META
    tq = 512
    tabs_r = _rope_tables(N_META + jnp.arange(seq))
    cos_m, sin_m = (jnp.tile(t, (batch, 1)) for t in _rope_tables(jnp.arange(N_META))[:2])
    for i in range(DEPTH):
        j = i // 2
        g_pre = p["norm_mix_pre"][i][None]
        g_post = p["norm_mix_post"][i][None]
        if i % 2 == 0:
            w_rows, w_cols = p["mla"][j]
            qt, kc, vt = _mla_proj_cols(h, g_pre, w_cols, tabs_r, 256, seq)
            q_m, kc_m, v_m = _mla_proj_rows(hm, g_pre, w_rows, cos_m, sin_m)
            o, o_m = _mla_attn(qt, kc, vt, q_m, kc_m, v_m, batch, min(2048, seq), 512)
            w_o = p["mla_w_o"][j]
        else:
            q, k, v = _swa_proj(h, g_pre, p["swa_w_qkv"][j], 512)
            q_m, k_m, v_m = _swa_proj(hm, g_pre, p["swa_w_qkv"][j], rows_m)
            o = _swa_attn(q, k, v, k_m, v_m, p["swa_bias"][j], batch, tq, BLOCK, True)
            o_m = _swa_attn(q_m, k, v, k_m, v_m, p["swa_bias_meta"][j], batch, N_META, N_META, False)
            w_o = p["swa_w_o"][j]
        h = _attn_out(o, w_o, g_post, h, 512)
        hm = _attn_out(o_m, w_o, g_post, hm, rows_m)
        mlp_args = (p["norm_mlp_pre"][i][None], p["mlp_w_in"][i], p["mlp_w_out"][i],
                    p["norm_mlp_post"][i][None])
        h = _mlp(h, *mlp_args, min(1024, h.shape[0]), 512)
        hm = _mlp(hm, *mlp_args, rows_m, 512)
    return h.reshape(batch, seq, D_MODEL)


def kernel(x_prompt, x_sample, meta_tokens, rel_bias, mla_w_dq, mla_q_norm, mla_w_uq, mla_w_dkv, mla_kv_norm, mla_w_ukv, mla_w_o, swa_w_qkv, swa_w_o, swa_sink, mlp_w_in, mlp_w_out, norm_mix_pre, norm_mix_post, norm_mlp_pre, norm_mlp_post):
    seq = x_prompt.shape[1]
    assert x_sample.shape[1] == seq and seq % 512 == 0
    n_mla = mla_w_dq.shape[0]
    n_swa = swa_w_qkv.shape[0]
    qpos = N_META + np.arange(BLOCK)
    codes = np.stack([
        _swa_tile_codes(qpos + BLOCK, 0),
        _swa_tile_codes(qpos, 0),
        _swa_tile_codes(qpos + 2 * BLOCK, 0),
    ])
    codes_meta = _swa_tile_codes(np.arange(N_META), 0)[None]
    p = {
        "mla": [_prep_mla(mla_w_dq[j], mla_q_norm[j], mla_w_uq[j], mla_w_dkv[j], mla_kv_norm[j],
                          mla_w_ukv[j]) for j in range(n_mla)],
        "mla_w_o": mla_w_o.astype(BF16),
        "swa_w_qkv": swa_w_qkv.astype(BF16),
        "swa_w_o": swa_w_o.astype(BF16),
        "swa_bias": [_swa_bias(jnp.asarray(codes), rel_bias, swa_sink[j]) for j in range(n_swa)],
        "swa_bias_meta": [_swa_bias(jnp.asarray(codes_meta), rel_bias, swa_sink[j])
                          for j in range(n_swa)],
        "mlp_w_in": mlp_w_in.astype(BF16),
        "mlp_w_out": mlp_w_out.astype(BF16),
        "norm_mix_pre": norm_mix_pre, "norm_mix_post": norm_mix_post,
        "norm_mlp_pre": norm_mlp_pre, "norm_mlp_post": norm_mlp_post,
    }
    return (_trunk(x_prompt, meta_tokens, p), _trunk(x_sample, meta_tokens, p))
```

```python
import functools
import math

import jax
import jax.numpy as jnp
import numpy as np
from jax import lax
from jax.experimental import pallas as pl
from jax.experimental.pallas import tpu as pltpu

D_MODEL = 2048
DEPTH = 4
N_META = 16
MLA_HEADS = 16
MLA_Q_LORA = 512
MLA_KV_LORA = 512
MLA_NOPE = 128
MLA_ROPE = 64
MLA_V = 128
MLA_VX = MLA_V + 16
MLA_QK_PAD = 256
MLA_STREAMS = 2
ROPE_THETA = 10000.0
SWA_HEADS = 32
SWA_KV_HEADS = 4
SWA_GROUP = SWA_HEADS // SWA_KV_HEADS
SWA_HEAD_DIM = 64
WINDOW = 128
BLOCK = 128
N_BUCKETS = 32
MAX_DISTANCE = 128
D_FF = 4 * D_MODEL
EPS = 1e-6

SWA_BAND = 3 * BLOCK
SWA_KEYS = 512
SINK_COL = SWA_BAND + N_META
NEG = -1e30

LOG2E = math.log2(math.e)
VMEM_LIMIT_V7X = 56 * 1024 * 1024

BF16 = jnp.bfloat16
F32 = jnp.float32


def _cparams(sem):
    return pltpu.CompilerParams(dimension_semantics=sem, vmem_limit_bytes=VMEM_LIMIT_V7X)


def _rms(x, g):
    r = lax.rsqrt(jnp.mean(x * x, axis=-1, keepdims=True) + EPS)
    return x * r * g


def _dot(a, b):
    return jnp.dot(a, b, preferred_element_type=F32)


def _dot_t(a, b):
    return lax.dot_general(a, b, (((1,), (1,)), ((), ())), preferred_element_type=F32)


def _rope128(y, cos, sin):
    return y * cos + pltpu.roll(y, 32, axis=1) * sin


MLA_SCALE = (MLA_NOPE + MLA_ROPE) ** -0.5 * LOG2E


def _mla_latents(h_ref, g_ref, wd_ref, qn_ref, kvn_ref):
    xn = _rms(h_ref[...], g_ref[...]).astype(BF16)
    a = _dot(xn, wd_ref[...])
    cq = _rms(a[:, :MLA_Q_LORA], qn_ref[...]).astype(BF16)
    ckv = _rms(a[:, MLA_Q_LORA:MLA_Q_LORA + MLA_KV_LORA], kvn_ref[...]).astype(BF16)
    return cq, ckv, a[:, MLA_Q_LORA + MLA_KV_LORA:]


def _store_keys(kc_ref, ckv, wuk_ref, kr):
    for hd in range(MLA_HEADS):
        lo = hd * MLA_QK_PAD
        kc_ref[:, lo:lo + MLA_NOPE] = _dot(
            ckv, wuk_ref[:, hd * MLA_NOPE:(hd + 1) * MLA_NOPE]).astype(BF16)
        kc_ref[:, lo + MLA_NOPE:lo + MLA_QK_PAD] = kr


def _mla_proj_rows_kernel(h_ref, g_ref, wd_ref, qn_ref, kvn_ref, wuq_ref, wuk_ref, wuv_ref,
                          cos_ref, sin_ref, q_ref, kc_ref, v_ref):
    cq, ckv, kr = _mla_latents(h_ref, g_ref, wd_ref, qn_ref, kvn_ref)
    cos = cos_ref[...]
    sin = sin_ref[...]
    _store_keys(kc_ref, ckv, wuk_ref, _rope128(kr, cos, sin).astype(BF16))
    v_ref[...] = _dot(ckv, wuv_ref[...]).astype(BF16)
    for hd in range(MLA_HEADS):
        lo = hd * MLA_QK_PAD
        qh = _dot(cq, wuq_ref[:, lo:lo + MLA_QK_PAD])
        q_ref[:, lo:lo + MLA_NOPE] = (qh[:, :MLA_NOPE] * MLA_SCALE).astype(BF16)
        q_ref[:, lo + MLA_NOPE:lo + MLA_QK_PAD] = (
            _rope128(qh[:, MLA_NOPE:], cos, sin) * MLA_SCALE).astype(BF16)


def _mla_proj_cols_kernel(h_ref, g_ref, wd_ref, qn_ref, kvn_ref, wuqt_ref, wuk_ref, wuvt_ref,
                          cos_ref, sin_ref, cost_ref, sint_ref, qt_ref, kc_ref, vt_ref):
    cq, ckv, kr = _mla_latents(h_ref, g_ref, wd_ref, qn_ref, kvn_ref)
    _store_keys(kc_ref, ckv, wuk_ref, _rope128(kr, cos_ref[...], sin_ref[...]).astype(BF16))
    vt = _dot_t(wuvt_ref[...], ckv).astype(BF16)
    ones = jnp.ones((MLA_VX - MLA_V, vt.shape[1]), BF16)
    for hd in range(MLA_HEADS):
        vt_ref[hd * MLA_VX:hd * MLA_VX + MLA_V, :] = vt[hd * MLA_V:(hd + 1) * MLA_V]
        vt_ref[hd * MLA_VX + MLA_V:(hd + 1) * MLA_VX, :] = ones
    qt = _dot_t(wuqt_ref[...], cq)
    c = cost_ref[...]
    s = sint_ref[...]
    half = MLA_ROPE // 2
    for hd in range(MLA_HEADS):
        lo = hd * MLA_QK_PAD
        qt_ref[lo:lo + MLA_NOPE, :] = (qt[lo:lo + MLA_NOPE] * MLA_SCALE).astype(BF16)
        x1 = qt[lo + MLA_NOPE:lo + MLA_NOPE + half]
        x2 = qt[lo + MLA_NOPE + half:lo + MLA_NOPE + MLA_ROPE]
        qt_ref[lo + MLA_NOPE:lo + MLA_NOPE + half, :] = ((x1 * c - x2 * s) * MLA_SCALE).astype(BF16)
        qt_ref[lo + MLA_NOPE + half:lo + MLA_NOPE + MLA_ROPE, :] = (
            (x2 * c + x1 * s) * MLA_SCALE).astype(BF16)
        qt_ref[lo + MLA_NOPE + MLA_ROPE:lo + MLA_QK_PAD, :] = jnp.zeros(
            (MLA_QK_PAD - MLA_NOPE - MLA_ROPE, qt.shape[1]), BF16)


def _full_spec(a):
    return pl.BlockSpec(a.shape, lambda i: (0,) * a.ndim)


def _mla_proj_rows(h, g, w, cos, sin):
    rows = h.shape[0]
    wd, qn, kvn, wuq, wuk, wuv = w
    args = (h, g, wd, qn, kvn, wuq, wuk, wuv, cos, sin)
    widths = (MLA_HEADS * MLA_QK_PAD, MLA_HEADS * MLA_QK_PAD, MLA_HEADS * MLA_V)
    return pl.pallas_call(
        _mla_proj_rows_kernel,
        grid=(1,),
        in_specs=[_full_spec(a) for a in args],
        out_specs=[pl.BlockSpec((rows, n), lambda i: (0, 0)) for n in widths],
        out_shape=[jax.ShapeDtypeStruct((rows, n), BF16) for n in widths],
        compiler_params=_cparams(("arbitrary",)),
        name="mla_proj_rows",
    )(*args)


def _mla_proj_cols(h, g, w, tabs, tm, seq):
    rows = h.shape[0]
    wd, qn, kvn, wuqt, wuk, wuvt = w
    cos, sin, cost, sint = tabs
    nblk = seq // tm
    row = lambda width: pl.BlockSpec((tm, width), lambda i: (i, 0))
    col = lambda height: pl.BlockSpec((height, tm), lambda i: (0, i))
    tab = pl.BlockSpec((tm, 128), lambda i: (i % nblk, 0))
    tabt = pl.BlockSpec((MLA_ROPE // 2, tm), lambda i: (0, i % nblk))
    return pl.pallas_call(
        _mla_proj_cols_kernel,
        grid=(rows // tm,),
        in_specs=[row(D_MODEL)] + [_full_spec(a) for a in (g, wd, qn, kvn, wuqt, wuk, wuvt)]
        + [tab, tab, tabt, tabt],
        out_specs=[col(MLA_HEADS * MLA_QK_PAD), row(MLA_HEADS * MLA_QK_PAD), col(MLA_HEADS * MLA_VX)],
        out_shape=[jax.ShapeDtypeStruct((MLA_HEADS * MLA_QK_PAD, rows), BF16),
                   jax.ShapeDtypeStruct((rows, MLA_HEADS * MLA_QK_PAD), BF16),
                   jax.ShapeDtypeStruct((MLA_HEADS * MLA_VX, rows), BF16)],
        compiler_params=_cparams(("parallel",)),
        name="mla_proj_cols",
    )(h, g, wd, qn, kvn, wuqt, wuk, wuvt, cos, sin, cost, sint)


def _mla_attn_kernel(qt_ref, k_ref, vt_ref, qm_ref, km_ref, vm_ref, ki_ref, vti_ref,
                     o_ref, om_ref, mm_ref, lm_ref, am_ref, *, tk):
    i = pl.program_id(2)
    n_chunks = k_ref.shape[0] // tk
    qm = qm_ref[...]

    @pl.when(i == 0)
    def _():
        s = _dot_t(qm, km_ref[...])
        m = jnp.max(s, axis=-1, keepdims=True)
        p = jnp.exp2(s - m)
        mm_ref[...] = m
        lm_ref[...] = jnp.sum(p, axis=-1, keepdims=True)
        am_ref[...] = _dot(p.astype(BF16), vm_ref[...])

    n = qt_ref.shape[1] // MLA_STREAMS
    qts = [qt_ref[:, h * n:(h + 1) * n] for h in range(MLA_STREAMS)]
    score = lambda h, k: _dot(k, qts[h])
    meta = {"s": _dot_t(qm, ki_ref[...])}

    def meta_softmax():
        m_old = mm_ref[...]
        m = jnp.maximum(m_old, jnp.max(meta["s"], axis=-1, keepdims=True))
        meta["m"] = m
        meta["alpha"] = jnp.exp2(m_old - m)
        meta["p"] = jnp.exp2(meta["s"] - m)

    def meta_values():
        meta["o"] = _dot_t(meta["p"].astype(BF16), vti_ref[0:MLA_V, :])

    def meta_update():
        mm_ref[...] = meta["m"]
        lm_ref[...] = meta["alpha"] * lm_ref[...] + jnp.sum(meta["p"], axis=-1, keepdims=True)
        am_ref[...] = meta["alpha"] * am_ref[...] + meta["o"]

    stages = [(min(1, n_chunks - 1), meta_softmax), (min(2, n_chunks - 1), meta_values),
              (min(4, n_chunks - 1), meta_update)]

    pad = 128 - N_META
    vmt = jnp.concatenate([vm_ref[...].astype(F32), jnp.zeros((pad, MLA_V), F32)], axis=0).T
    vmt = jnp.concatenate([vmt, jnp.ones((MLA_VX - MLA_V, 128), F32)], axis=0).astype(BF16)
    km = km_ref[...]
    st = []
    for h in range(MLA_STREAMS):
        s = score(h, km)
        m = jnp.max(s, axis=0, keepdims=True)
        p = jnp.exp2(s - m)
        pm = jnp.concatenate([p.astype(BF16), jnp.zeros((pad, n), BF16)], axis=0)
        st.append({"m": m, "o_prev": _dot(vmt, pm), "acc": jnp.zeros((MLA_VX, n), F32),
                   "s_next": score(h, k_ref[0:tk, :])})
    for c in range(n_chunks):
        for h in range(MLA_STREAMS):
            t = st[h]
            s = t["s_next"]
            if c + 1 < n_chunks:
                t["s_next"] = score(h, k_ref[(c + 1) * tk:(c + 2) * tk, :])
            m_new = jnp.maximum(t["m"], jnp.max(s, axis=0, keepdims=True))
            alpha = jnp.exp2(t["m"] - m_new)
            p = jnp.exp2(s - m_new)
            o_c = _dot(vt_ref[:, c * tk:(c + 1) * tk], p.astype(BF16))
            t["acc"] = alpha * (t["acc"] + t["o_prev"])
            t["o_prev"] = o_c
            t["m"] = m_new
        for at, stage in stages:
            if at == c:
                stage()
    for h in range(MLA_STREAMS):
        acc = st[h]["acc"] + st[h]["o_prev"]
        o_ref[h * n:(h + 1) * n, :] = (
            acc[:MLA_V] * (1.0 / acc[MLA_V:MLA_V + 1])).T.astype(BF16)

    @pl.when(i == pl.num_programs(2) - 1)
    def _():
        om_ref[...] = (am_ref[...] / lm_ref[...]).astype(BF16)


def _mla_attn(qt, kc, vt, q_meta, kc_meta, v_meta, batch, tq, tk):
    rows = kc.shape[0]
    seq = rows // batch
    nq = seq // tq
    meta = lambda width: pl.BlockSpec((N_META, width), lambda s, h, i: (s, h))
    return pl.pallas_call(
        functools.partial(_mla_attn_kernel, tk=tk),
        grid=(batch, MLA_HEADS, nq),
        in_specs=[
            pl.BlockSpec((MLA_QK_PAD, tq), lambda s, h, i: (h, s * nq + i)),
            pl.BlockSpec((seq, MLA_QK_PAD), lambda s, h, i: (s, h)),
            pl.BlockSpec((MLA_VX, seq), lambda s, h, i: (h, s)),
            meta(MLA_QK_PAD), meta(MLA_QK_PAD), meta(MLA_V),
            pl.BlockSpec((tq, MLA_QK_PAD), lambda s, h, i: (s * nq + i, h)),
            pl.BlockSpec((MLA_VX, tq), lambda s, h, i: (h, s * nq + i)),
        ],
        out_specs=[pl.BlockSpec((tq, MLA_V), lambda s, h, i: (s * nq + i, h)), meta(MLA_V)],
        out_shape=[jax.ShapeDtypeStruct((rows, MLA_HEADS * MLA_V), BF16),
                   jax.ShapeDtypeStruct((batch * N_META, MLA_HEADS * MLA_V), BF16)],
        scratch_shapes=[pltpu.VMEM((N_META, 1), F32), pltpu.VMEM((N_META, 1), F32),
                        pltpu.VMEM((N_META, MLA_V), F32)],
        compiler_params=_cparams(("parallel", "parallel", "arbitrary")),
        name="mla_attn",
    )(qt, kc, vt, q_meta, kc_meta, v_meta, kc, vt)


def _swa_proj_kernel(h_ref, g_ref, w_ref, q_ref, k_ref, v_ref):
    xn = _rms(h_ref[...], g_ref[...]).astype(BF16)
    dq = SWA_HEADS * SWA_HEAD_DIM
    dk = SWA_KV_HEADS * SWA_HEAD_DIM
    scale = SWA_HEAD_DIM ** -0.5
    for kvh in range(SWA_KV_HEADS):
        lo = kvh * SWA_GROUP * SWA_HEAD_DIM
        qk = _dot(xn, w_ref[:, lo:lo + SWA_GROUP * SWA_HEAD_DIM]) * scale
        for g in range(SWA_GROUP):
            q_ref[kvh * SWA_GROUP + g] = qk[:, g * SWA_HEAD_DIM:(g + 1) * SWA_HEAD_DIM].astype(BF16)
    kv = _dot(xn, w_ref[:, dq:dq + 2 * dk])
    for kvh in range(SWA_KV_HEADS):
        k_ref[kvh] = kv[:, kvh * SWA_HEAD_DIM:(kvh + 1) * SWA_HEAD_DIM].astype(BF16)
        v_ref[kvh] = kv[:, dk + kvh * SWA_HEAD_DIM:dk + (kvh + 1) * SWA_HEAD_DIM].astype(BF16)


def _swa_proj(h, g, w, tm):
    rows = h.shape[0]
    hm = lambda n: pl.BlockSpec((n, tm, SWA_HEAD_DIM), lambda i: (0, i, 0))
    return pl.pallas_call(
        _swa_proj_kernel,
        grid=(rows // tm,),
        in_specs=[pl.BlockSpec((tm, D_MODEL), lambda i: (i, 0)),
                  pl.BlockSpec(g.shape, lambda i: (0, 0)),
                  pl.BlockSpec(w.shape, lambda i: (0, 0))],
        out_specs=[hm(SWA_HEADS), hm(SWA_KV_HEADS), hm(SWA_KV_HEADS)],
        out_shape=[jax.ShapeDtypeStruct((SWA_HEADS, rows, SWA_HEAD_DIM), BF16),
                   jax.ShapeDtypeStruct((SWA_KV_HEADS, rows, SWA_HEAD_DIM), BF16),
                   jax.ShapeDtypeStruct((SWA_KV_HEADS, rows, SWA_HEAD_DIM), BF16)],
        compiler_params=_cparams(("parallel",)),
        name="swa_proj",
    )(h, g, w)


def _swa_proj_cols_kernel(h_ref, g_ref, wqt_ref, wkv_ref, qt_ref, k_ref, v_ref):
    xn = _rms(h_ref[...], g_ref[...]).astype(BF16)
    qt = (_dot_t(wqt_ref[...], xn) * (SWA_HEAD_DIM ** -0.5 * LOG2E)).astype(BF16)
    for kvh in range(SWA_KV_HEADS):
        for g in range(SWA_GROUP):
            r0 = (kvh * SWA_GROUP + g) * SWA_HEAD_DIM
            for blk in range(qt.shape[1] // BLOCK):
                c0 = (blk * SWA_GROUP + g) * BLOCK
                qt_ref[kvh, :, c0:c0 + BLOCK] = qt[r0:r0 + SWA_HEAD_DIM, blk * BLOCK:(blk + 1) * BLOCK]
    dk = SWA_KV_HEADS * SWA_HEAD_DIM
    kv = _dot(xn, wkv_ref[...])
    for kvh in range(SWA_KV_HEADS):
        k_ref[kvh] = kv[:, kvh * SWA_HEAD_DIM:(kvh + 1) * SWA_HEAD_DIM].astype(BF16)
        v_ref[kvh] = kv[:, dk + kvh * SWA_HEAD_DIM:dk + (kvh + 1) * SWA_HEAD_DIM].astype(BF16)


def _swa_proj_cols(h, g, wqt, wkv, tm):
    rows = h.shape[0]
    hm = pl.BlockSpec((SWA_KV_HEADS, tm, SWA_HEAD_DIM), lambda i: (0, i, 0))
    return pl.pallas_call(
        _swa_proj_cols_kernel,
        grid=(rows // tm,),
        in_specs=[pl.BlockSpec((tm, D_MODEL), lambda i: (i, 0)),
                  _full_spec(g), _full_spec(wqt), _full_spec(wkv)],
        out_specs=[pl.BlockSpec((SWA_KV_HEADS, SWA_HEAD_DIM, tm * SWA_GROUP), lambda i: (0, 0, i)),
                   hm, hm],
        out_shape=[jax.ShapeDtypeStruct((SWA_KV_HEADS, SWA_HEAD_DIM, rows * SWA_GROUP), BF16),
                   jax.ShapeDtypeStruct((SWA_KV_HEADS, rows, SWA_HEAD_DIM), BF16),
                   jax.ShapeDtypeStruct((SWA_KV_HEADS, rows, SWA_HEAD_DIM), BF16)],
        compiler_params=_cparams(("parallel",)),
        name="swa_proj_cols",
    )(h, g, wqt, wkv)


def _t5_bucket_exact(rel):
    nb = N_BUCKETS // 2
    max_exact = nb // 2
    table = []
    for n in range(int(np.abs(rel).max()) + 1):
        if n < max_exact:
            table.append(n)
        else:
            k = 0
            while 64 * 2 ** (k + 1) <= n * n:
                k += 1
            table.append(min(max_exact + k, nb - 1))
    return (np.asarray(table, np.int32)[np.abs(rel)] + np.where(rel > 0, nb, 0)).astype(np.int32)


def _swa_tile_codes(qpos, band_start):
    nq = qpos.shape[0]
    codes = np.full((nq, SWA_KEYS), -1, np.int32)
    kpos_band = N_META + band_start + np.arange(SWA_BAND)
    rel = kpos_band[None, :] - qpos[:, None]
    ok = np.abs(rel) <= WINDOW
    codes[:, :SWA_BAND] = np.where(ok, _t5_bucket_exact(rel), -1)
    rel_m = np.arange(N_META)[None, :] - qpos[:, None]
    codes[:, SWA_BAND:SWA_BAND + N_META] = _t5_bucket_exact(rel_m)
    codes[:, SINK_COL] = N_BUCKETS
    return codes


def _swa_bias_kernel(code_ref, rb_ref, sink_ref, o_ref, *, scale):
    hd = pl.program_id(1)
    code = code_ref[0]
    t = jnp.full(code.shape, NEG, F32)
    for b in range(N_BUCKETS):
        t = jnp.where(code == b, rb_ref[b, hd] * scale, t)
    t = jnp.where(code == N_BUCKETS, sink_ref[hd] * scale, t)
    o_ref[0, 0] = t


def _swa_bias(codes, rel_bias, sink, transposed):
    nv, a, b = codes.shape
    if transposed:
        out_spec = pl.BlockSpec((1, 1, a, b), lambda v, h: (v, h // SWA_GROUP, 0, h % SWA_GROUP))
        out_shape = (nv, SWA_KV_HEADS, a, SWA_GROUP * b)
    else:
        out_spec = pl.BlockSpec((1, 1, a, b), lambda v, h: (v, h, 0, 0))
        out_shape = (nv, SWA_HEADS, a, b)
    return pl.pallas_call(
        functools.partial(_swa_bias_kernel, scale=LOG2E if transposed else 1.0),
        grid=(nv, SWA_HEADS),
        in_specs=[pl.BlockSpec((1, a, b), lambda v, h: (v, 0, 0)),
                  pl.BlockSpec(memory_space=pltpu.SMEM),
                  pl.BlockSpec(memory_space=pltpu.SMEM)],
        out_specs=out_spec,
        out_shape=jax.ShapeDtypeStruct(out_shape, F32),
        compiler_params=_cparams(("parallel", "parallel")),
        name="swa_bias",
    )(codes, rel_bias, sink)


def _swa_attn_kernel(q_ref, k_ref, v_ref, km_ref, vm_ref, b_ref, o_ref, *, sub, real):
    nsub = q_ref.shape[1] // sub
    seq = k_ref.shape[1]
    n_blocks = seq // BLOCK
    pad = jnp.zeros((SWA_KEYS - SWA_BAND - N_META, SWA_HEAD_DIM), BF16)
    for j in range(nsub):
        if real:
            b = pl.program_id(2) * nsub + j
            start = pl.multiple_of(jnp.clip((b - 1) * BLOCK, 0, seq - SWA_BAND), BLOCK)
            variant = jnp.where(b == 0, 1, jnp.where(b == n_blocks - 1, 2, 0))
        else:
            start = 0
            variant = 0
        kt = jnp.concatenate([k_ref[0, pl.ds(start, SWA_BAND), :], km_ref[0], pad], axis=0)
        vt = jnp.concatenate([v_ref[0, pl.ds(start, SWA_BAND), :], vm_ref[0], pad], axis=0)
        q = q_ref[:, j * sub:(j + 1) * sub, :].reshape(SWA_GROUP * sub, SWA_HEAD_DIM)
        s = _dot_t(q, kt) + b_ref[variant].reshape(SWA_GROUP * sub, SWA_KEYS)
        m = jnp.max(s, axis=-1, keepdims=True)
        e = jnp.exp(s - m)
        l = jnp.sum(e, axis=-1, keepdims=True)
        o = _dot(e.astype(BF16), vt) / l
        o = o.reshape(SWA_GROUP, sub, SWA_HEAD_DIM)
        o_ref[j * sub:(j + 1) * sub, :] = jnp.concatenate(
            [o[g] for g in range(SWA_GROUP)], axis=1).astype(BF16)


def _swa_attn(q, k, v, k_meta, v_meta, bias, batch, tq, sub, real):
    rows = q.shape[1]
    seq = k.shape[1] // batch
    nq = rows // batch // tq
    nv = bias.shape[0]
    return pl.pallas_call(
        functools.partial(_swa_attn_kernel, sub=sub, real=real),
        grid=(SWA_KV_HEADS, batch, nq),
        in_specs=[
            pl.BlockSpec((SWA_GROUP, tq, SWA_HEAD_DIM), lambda kv, s, i: (kv, s * nq + i, 0)),
            pl.BlockSpec((1, seq, SWA_HEAD_DIM), lambda kv, s, i: (kv, s, 0)),
            pl.BlockSpec((1, seq, SWA_HEAD_DIM), lambda kv, s, i: (kv, s, 0)),
            pl.BlockSpec((1, N_META, SWA_HEAD_DIM), lambda kv, s, i: (kv, s, 0)),
            pl.BlockSpec((1, N_META, SWA_HEAD_DIM), lambda kv, s, i: (kv, s, 0)),
            pl.BlockSpec((nv, SWA_GROUP, sub, SWA_KEYS), lambda kv, s, i: (0, kv, 0, 0)),
        ],
        out_specs=pl.BlockSpec((tq, SWA_GROUP * SWA_HEAD_DIM), lambda kv, s, i: (s * nq + i, kv)),
        out_shape=jax.ShapeDtypeStruct((rows, SWA_HEADS * SWA_HEAD_DIM), BF16),
        compiler_params=_cparams(("parallel", "parallel", "arbitrary")),
        name="swa_attn",
    )(q, k, v, k_meta, v_meta, bias)


def _swa_attn_cols_kernel(qt_ref, k_ref, v_ref, km_ref, vm_ref, b_ref, o_ref):
    lanes = SWA_GROUP * BLOCK
    nsub = qt_ref.shape[2] // lanes
    seq = k_ref.shape[1]
    n_blocks = seq // BLOCK
    used = SWA_BAND + 2 * N_META
    kpad = jnp.zeros((used - SWA_BAND - N_META, SWA_HEAD_DIM), BF16)
    vpad = jnp.zeros((SWA_KEYS - SWA_BAND - N_META, SWA_HEAD_DIM), BF16)
    ppad = jnp.zeros((SWA_KEYS - used, lanes), BF16)
    ext = jnp.concatenate([jnp.ones((SWA_KEYS, 16), F32),
                           jnp.zeros((SWA_KEYS, 128 - SWA_HEAD_DIM - 16), F32)], axis=1)

    def window(j):
        b = pl.program_id(2) * nsub + j
        start = pl.multiple_of(jnp.clip((b - 1) * BLOCK, 0, seq - SWA_BAND), BLOCK)
        variant = jnp.where(b == 0, 1, jnp.where(b == n_blocks - 1, 2, 0))
        return start, variant

    def scores(j):
        start, variant = window(j)
        kt = jnp.concatenate([k_ref[0, pl.ds(start, SWA_BAND), :], km_ref[0], kpad], axis=0)
        return (_dot(kt, qt_ref[0, :, j * lanes:(j + 1) * lanes])
                + b_ref[variant, 0, 0:used, :])

    s_next = scores(0)
    for j in range(nsub):
        s = s_next
        if j + 1 < nsub:
            s_next = scores(j + 1)
        start, _ = window(j)
        vrows = jnp.concatenate([v_ref[0, pl.ds(start, SWA_BAND), :], vm_ref[0], vpad], axis=0)
        vt = jnp.concatenate([vrows.astype(F32), ext], axis=1).T.astype(BF16)
        m = jnp.max(s, axis=0, keepdims=True)
        p = jnp.concatenate([jnp.exp2(s - m).astype(BF16), ppad], axis=0)
        acc = _dot(vt, p)
        acc = acc * (1.0 / acc[SWA_HEAD_DIM:SWA_HEAD_DIM + 1])
        o = acc.T[:, :SWA_HEAD_DIM].reshape(SWA_GROUP, BLOCK, SWA_HEAD_DIM)
        o_ref[j * BLOCK:(j + 1) * BLOCK, :] = jnp.concatenate(
            [o[g] for g in range(SWA_GROUP)], axis=1).astype(BF16)


def _swa_attn_cols(qt, k, v, k_meta, v_meta, bias, batch, tq):
    rows = k.shape[1]
    seq = rows // batch
    nq = seq // tq
    kv_spec = pl.BlockSpec((1, seq, SWA_HEAD_DIM), lambda kv, s, i: (kv, s, 0))
    meta_spec = pl.BlockSpec((1, N_META, SWA_HEAD_DIM), lambda kv, s, i: (kv, s, 0))
    return pl.pallas_call(
        _swa_attn_cols_kernel,
        grid=(SWA_KV_HEADS, batch, nq),
        in_specs=[
            pl.BlockSpec((1, SWA_HEAD_DIM, tq * SWA_GROUP), lambda kv, s, i: (kv, 0, s * nq + i)),
            kv_spec, kv_spec, meta_spec, meta_spec,
            pl.BlockSpec((bias.shape[0], 1) + bias.shape[2:], lambda kv, s, i: (0, kv, 0, 0)),
        ],
        out_specs=pl.BlockSpec((tq, SWA_GROUP * SWA_HEAD_DIM), lambda kv, s, i: (s * nq + i, kv)),
        out_shape=jax.ShapeDtypeStruct((rows, SWA_HEADS * SWA_HEAD_DIM), BF16),
        compiler_params=_cparams(("parallel", "parallel", "arbitrary")),
        name="swa_attn_cols",
    )(qt, k, v, k_meta, v_meta, bias)


def _attn_out_kernel(o_ref, w_ref, g_ref, h_ref, out_ref):
    m = _dot(o_ref[...], w_ref[...])
    out_ref[...] = h_ref[...] + _rms(m, g_ref[...])


def _attn_out(o, w, g, h, tm):
    rows = h.shape[0]
    return pl.pallas_call(
        _attn_out_kernel,
        grid=(rows // tm,),
        in_specs=[pl.BlockSpec((tm, o.shape[1]), lambda i: (i, 0)),
                  pl.BlockSpec(w.shape, lambda i: (0, 0)),
                  pl.BlockSpec(g.shape, lambda i: (0, 0)),
                  pl.BlockSpec((tm, D_MODEL), lambda i: (i, 0))],
        out_specs=pl.BlockSpec((tm, D_MODEL), lambda i: (i, 0)),
        out_shape=jax.ShapeDtypeStruct((rows, D_MODEL), F32),
        compiler_params=_cparams(("parallel",)),
        name="attn_out",
    )(o, w, g, h)


def _mlp_kernel(h_ref, gpre_ref, win_ref, wout_ref, gpost_ref, out_ref, xn_ref):
    j = pl.program_id(1)
    last = pl.num_programs(1) - 1

    def ffn(xn):
        u = jnp.maximum(_dot(xn, win_ref[...]), 0.0)
        return _dot((u * u).astype(BF16), wout_ref[...])

    @pl.when(j == 0)
    def _():
        xn = _rms(h_ref[...], gpre_ref[...]).astype(BF16)
        xn_ref[...] = xn
        out_ref[...] = ffn(xn)

    @pl.when(jnp.logical_and(j > 0, j < last))
    def _():
        out_ref[...] += ffn(xn_ref[...])

    @pl.when(j == last)
    def _():
        f = out_ref[...] + ffn(xn_ref[...])
        out_ref[...] = h_ref[...] + _rms(f, gpost_ref[...])


def _mlp(h, gpre, w_in, w_out, gpost, tm, tf):
    rows = h.shape[0]
    return pl.pallas_call(
        _mlp_kernel,
        grid=(rows // tm, D_FF // tf),
        in_specs=[pl.BlockSpec((tm, D_MODEL), lambda i, j: (i, 0)),
                  pl.BlockSpec(gpre.shape, lambda i, j: (0, 0)),
                  pl.BlockSpec((D_MODEL, tf), lambda i, j: (0, j)),
                  pl.BlockSpec((tf, D_MODEL), lambda i, j: (j, 0)),
                  pl.BlockSpec(gpost.shape, lambda i, j: (0, 0))],
        out_specs=pl.BlockSpec((tm, D_MODEL), lambda i, j: (i, 0)),
        out_shape=jax.ShapeDtypeStruct((rows, D_MODEL), F32),
        scratch_shapes=[pltpu.VMEM((tm, D_MODEL), BF16)],
        compiler_params=_cparams(("parallel", "arbitrary")),
        name="mlp",
    )(h, gpre, w_in, w_out, gpost)


def _rope_dup(w):
    return jnp.concatenate([w, w], axis=-1)


def _prep_mla(w_dq, q_norm, w_uq, w_dkv, kv_norm, w_ukv):
    wd = jnp.concatenate([w_dq, w_dkv[:, :MLA_KV_LORA], _rope_dup(w_dkv[:, MLA_KV_LORA:])], axis=1)
    uq = w_uq.reshape(MLA_Q_LORA, MLA_HEADS, MLA_NOPE + MLA_ROPE)
    wuq = jnp.concatenate([uq[..., :MLA_NOPE], _rope_dup(uq[..., MLA_NOPE:])], axis=-1)
    wuq = wuq.reshape(MLA_Q_LORA, MLA_HEADS * MLA_QK_PAD)
    ukv = w_ukv.reshape(MLA_KV_LORA, MLA_HEADS, MLA_NOPE + MLA_V)
    wuk = ukv[..., :MLA_NOPE].reshape(MLA_KV_LORA, MLA_HEADS * MLA_NOPE)
    wuv = ukv[..., MLA_NOPE:].reshape(MLA_KV_LORA, MLA_HEADS * MLA_V)
    wd, wuq, wuk, wuv = (t.astype(BF16) for t in (wd, wuq, wuk, wuv))
    rows = (wd, q_norm[None], kv_norm[None], wuq, wuk, wuv)
    cols = (wd, q_norm[None], kv_norm[None], wuq.T, wuk, wuv.T)
    return rows, cols


def _rope_tables(pos):
    half = MLA_ROPE // 2
    inv = ROPE_THETA ** (-(jnp.arange(half, dtype=F32) / half))
    ang = pos.astype(F32)[:, None] * inv[None, :]
    cos, sin = jnp.cos(ang), jnp.sin(ang)
    zero = jnp.zeros((pos.shape[0], 128 - MLA_ROPE), F32)
    return (jnp.concatenate([cos, cos, zero], axis=1), jnp.concatenate([-sin, sin, zero], axis=1),
            cos.T, sin.T)


def _trunk(x, meta_tokens, p):
    batch, seq, _ = x.shape
    h = x.reshape(batch * seq, D_MODEL)
    hm = jnp.broadcast_to(meta_tokens[None], (batch, N_META, D_MODEL)).reshape(batch * N_META, D_MODEL)
    rows_m = batch * N_META
    tq = 512
    tabs_r = _rope_tables(N_META + jnp.arange(seq))
    cos_m, sin_m = (jnp.tile(t, (batch, 1)) for t in _rope_tables(jnp.arange(N_META))[:2])
    for i in range(DEPTH):
        j = i // 2
        g_pre = p["norm_mix_pre"][i][None]
        g_post = p["norm_mix_post"][i][None]
        if i % 2 == 0:
            w_rows, w_cols = p["mla"][j]
            qt, kc, vt = _mla_proj_cols(h, g_pre, w_cols, tabs_r, 256, seq)
            q_m, kc_m, v_m = _mla_proj_rows(hm, g_pre, w_rows, cos_m, sin_m)
            o, o_m = _mla_attn(qt, kc, vt, q_m, kc_m, v_m, batch, min(2048, seq), 512)
            w_o = p["mla_w_o"][j]
        else:
            qt, k, v = _swa_proj_cols(h, g_pre, p["swa_wqt"][j], p["swa_wkv"][j], 512)
            q_m, k_m, v_m = _swa_proj(hm, g_pre, p["swa_w_qkv"][j], rows_m)
            o = _swa_attn_cols(qt, k, v, k_m, v_m, p["swa_bias"][j], batch, min(1024, seq))
            o_m = _swa_attn(q_m, k, v, k_m, v_m, p["swa_bias_meta"][j], batch, N_META, N_META, False)
            w_o = p["swa_w_o"][j]
        h = _attn_out(o, w_o, g_post, h, 512)
        hm = _attn_out(o_m, w_o, g_post, hm, rows_m)
        mlp_args = (p["norm_mlp_pre"][i][None], p["mlp_w_in"][i], p["mlp_w_out"][i],
                    p["norm_mlp_post"][i][None])
        h = _mlp(h, *mlp_args, min(1024, h.shape[0]), 512)
        hm = _mlp(hm, *mlp_args, rows_m, 512)
    return h.reshape(batch, seq, D_MODEL)


def kernel(x_prompt, x_sample, meta_tokens, rel_bias, mla_w_dq, mla_q_norm, mla_w_uq, mla_w_dkv, mla_kv_norm, mla_w_ukv, mla_w_o, swa_w_qkv, swa_w_o, swa_sink, mlp_w_in, mlp_w_out, norm_mix_pre, norm_mix_post, norm_mlp_pre, norm_mlp_post):
    seq = x_prompt.shape[1]
    assert x_sample.shape[1] == seq and seq % 512 == 0
    n_mla = mla_w_dq.shape[0]
    n_swa = swa_w_qkv.shape[0]
    dq = SWA_HEADS * SWA_HEAD_DIM
    qpos = N_META + np.arange(BLOCK)
    codes = np.stack([
        _swa_tile_codes(qpos + BLOCK, 0),
        _swa_tile_codes(qpos, 0),
        _swa_tile_codes(qpos + 2 * BLOCK, 0),
    ])
    codes_meta = _swa_tile_codes(np.arange(N_META), 0)[None]
    p = {
        "mla": [_prep_mla(mla_w_dq[j], mla_q_norm[j], mla_w_uq[j], mla_w_dkv[j], mla_kv_norm[j],
                          mla_w_ukv[j]) for j in range(n_mla)],
        "mla_w_o": mla_w_o.astype(BF16),
        "swa_w_qkv": swa_w_qkv.astype(BF16),
        "swa_wqt": jnp.swapaxes(swa_w_qkv[:, :, :dq], 1, 2).astype(BF16),
        "swa_wkv": swa_w_qkv[:, :, dq:].astype(BF16),
        "swa_w_o": swa_w_o.astype(BF16),
        "swa_bias": [_swa_bias(jnp.asarray(codes.transpose(0, 2, 1)), rel_bias, swa_sink[j], True)
                     for j in range(n_swa)],
        "swa_bias_meta": [_swa_bias(jnp.asarray(codes_meta), rel_bias, swa_sink[j], False)
                          for j in range(n_swa)],
        "mlp_w_in": mlp_w_in.astype(BF16),
        "mlp_w_out": mlp_w_out.astype(BF16),
        "norm_mix_pre": norm_mix_pre, "norm_mix_post": norm_mix_post,
        "norm_mlp_pre": norm_mlp_pre, "norm_mlp_post": norm_mlp_post,
    }
    return (_trunk(x_prompt, meta_tokens, p), _trunk(x_sample, meta_tokens, p))
```

```python
import functools
import math

import jax
import jax.numpy as jnp
import numpy as np
from jax import lax
from jax.experimental import pallas as pl
from jax.experimental.pallas import tpu as pltpu

D_MODEL = 2048
DEPTH = 4
N_META = 16
MLA_HEADS = 16
MLA_Q_LORA = 512
MLA_KV_LORA = 512
MLA_NOPE = 128
MLA_ROPE = 64
MLA_V = 128
MLA_VX = MLA_V + 16
MLA_QK_PAD = 256
MLA_STREAMS = 2
ROPE_THETA = 10000.0
SWA_HEADS = 32
SWA_KV_HEADS = 4
SWA_GROUP = SWA_HEADS // SWA_KV_HEADS
SWA_HEAD_DIM = 64
WINDOW = 128
BLOCK = 128
N_BUCKETS = 32
MAX_DISTANCE = 128
D_FF = 4 * D_MODEL
EPS = 1e-6

SWA_BAND = 3 * BLOCK
SWA_KEYS = 512
SINK_COL = SWA_BAND + N_META
NEG = -1e30

LOG2E = math.log2(math.e)
VMEM_LIMIT_V7X = 56 * 1024 * 1024

BF16 = jnp.bfloat16
F32 = jnp.float32


def _cparams(sem):
    return pltpu.CompilerParams(dimension_semantics=sem, vmem_limit_bytes=VMEM_LIMIT_V7X)


def _rms(x, g):
    r = lax.rsqrt(jnp.mean(x * x, axis=-1, keepdims=True) + EPS)
    return x * r * g


def _dot(a, b):
    return jnp.dot(a, b, preferred_element_type=F32)


def _dot_t(a, b):
    return lax.dot_general(a, b, (((1,), (1,)), ((), ())), preferred_element_type=F32)


def _rope128(y, cos, sin):
    return y * cos + pltpu.roll(y, 32, axis=1) * sin


MLA_SCALE = (MLA_NOPE + MLA_ROPE) ** -0.5 * LOG2E


def _mla_latents(h_ref, g_ref, wd_ref, qn_ref, kvn_ref):
    xn = _rms(h_ref[...], g_ref[...]).astype(BF16)
    a = _dot(xn, wd_ref[...])
    cq = _rms(a[:, :MLA_Q_LORA], qn_ref[...]).astype(BF16)
    ckv = _rms(a[:, MLA_Q_LORA:MLA_Q_LORA + MLA_KV_LORA], kvn_ref[...]).astype(BF16)
    return cq, ckv, a[:, MLA_Q_LORA + MLA_KV_LORA:]


def _store_keys(kc_ref, ckv, wuk_ref, kr):
    for hd in range(MLA_HEADS):
        lo = hd * MLA_QK_PAD
        kc_ref[:, lo:lo + MLA_NOPE] = _dot(
            ckv, wuk_ref[:, hd * MLA_NOPE:(hd + 1) * MLA_NOPE]).astype(BF16)
        kc_ref[:, lo + MLA_NOPE:lo + MLA_QK_PAD] = kr


def _mla_proj_rows_kernel(h_ref, g_ref, wd_ref, qn_ref, kvn_ref, wuq_ref, wuk_ref, wuv_ref,
                          cos_ref, sin_ref, q_ref, kc_ref, v_ref):
    cq, ckv, kr = _mla_latents(h_ref, g_ref, wd_ref, qn_ref, kvn_ref)
    cos = cos_ref[...]
    sin = sin_ref[...]
    _store_keys(kc_ref, ckv, wuk_ref, _rope128(kr, cos, sin).astype(BF16))
    v_ref[...] = _dot(ckv, wuv_ref[...]).astype(BF16)
    for hd in range(MLA_HEADS):
        lo = hd * MLA_QK_PAD
        qh = _dot(cq, wuq_ref[:, lo:lo + MLA_QK_PAD])
        q_ref[:, lo:lo + MLA_NOPE] = (qh[:, :MLA_NOPE] * MLA_SCALE).astype(BF16)
        q_ref[:, lo + MLA_NOPE:lo + MLA_QK_PAD] = (
            _rope128(qh[:, MLA_NOPE:], cos, sin) * MLA_SCALE).astype(BF16)


def _mla_proj_cols_kernel(h_ref, g_ref, wd_ref, qn_ref, kvn_ref, wuqt_ref, wuk_ref, wuvt_ref,
                          cos_ref, sin_ref, cost_ref, sint_ref, qt_ref, kc_ref, vt_ref):
    cq, ckv, kr = _mla_latents(h_ref, g_ref, wd_ref, qn_ref, kvn_ref)
    _store_keys(kc_ref, ckv, wuk_ref, _rope128(kr, cos_ref[...], sin_ref[...]).astype(BF16))
    vt = _dot_t(wuvt_ref[...], ckv).astype(BF16)
    ones = jnp.ones((MLA_VX - MLA_V, vt.shape[1]), BF16)
    for hd in range(MLA_HEADS):
        vt_ref[hd * MLA_VX:hd * MLA_VX + MLA_V, :] = vt[hd * MLA_V:(hd + 1) * MLA_V]
        vt_ref[hd * MLA_VX + MLA_V:(hd + 1) * MLA_VX, :] = ones
    qt = _dot_t(wuqt_ref[...], cq)
    c = cost_ref[...]
    s = sint_ref[...]
    half = MLA_ROPE // 2
    for hd in range(MLA_HEADS):
        lo = hd * MLA_QK_PAD
        qt_ref[lo:lo + MLA_NOPE, :] = (qt[lo:lo + MLA_NOPE] * MLA_SCALE).astype(BF16)
        x1 = qt[lo + MLA_NOPE:lo + MLA_NOPE + half]
        x2 = qt[lo + MLA_NOPE + half:lo + MLA_NOPE + MLA_ROPE]
        qt_ref[lo + MLA_NOPE:lo + MLA_NOPE + half, :] = ((x1 * c - x2 * s) * MLA_SCALE).astype(BF16)
        qt_ref[lo + MLA_NOPE + half:lo + MLA_NOPE + MLA_ROPE, :] = (
            (x2 * c + x1 * s) * MLA_SCALE).astype(BF16)
        qt_ref[lo + MLA_NOPE + MLA_ROPE:lo + MLA_QK_PAD, :] = jnp.zeros(
            (MLA_QK_PAD - MLA_NOPE - MLA_ROPE, qt.shape[1]), BF16)


def _full_spec(a):
    return pl.BlockSpec(a.shape, lambda *_: (0,) * a.ndim)


def _layer_spec(w, layer):
    return pl.BlockSpec((None,) + w.shape[1:], lambda *_: (layer,) + (0,) * (w.ndim - 1))


def _mla_proj_rows(h, g, w, cos, sin):
    rows = h.shape[0]
    wd, qn, kvn, wuq, wuk, wuv = w
    args = (h, g, wd, qn, kvn, wuq, wuk, wuv, cos, sin)
    widths = (MLA_HEADS * MLA_QK_PAD, MLA_HEADS * MLA_QK_PAD, MLA_HEADS * MLA_V)
    return pl.pallas_call(
        _mla_proj_rows_kernel,
        grid=(1,),
        in_specs=[_full_spec(a) for a in args],
        out_specs=[pl.BlockSpec((rows, n), lambda i: (0, 0)) for n in widths],
        out_shape=[jax.ShapeDtypeStruct((rows, n), BF16) for n in widths],
        compiler_params=_cparams(("arbitrary",)),
        name="mla_proj_rows",
    )(*args)


def _mla_proj_cols(h, g, w, tabs, tm, seq):
    rows = h.shape[0]
    wd, qn, kvn, wuqt, wuk, wuvt = w
    cos, sin, cost, sint = tabs
    nblk = seq // tm
    row = lambda width: pl.BlockSpec((tm, width), lambda i: (i, 0))
    col = lambda height: pl.BlockSpec((height, tm), lambda i: (0, i))
    tab = pl.BlockSpec((tm, 128), lambda i: (i % nblk, 0))
    tabt = pl.BlockSpec((MLA_ROPE // 2, tm), lambda i: (0, i % nblk))
    return pl.pallas_call(
        _mla_proj_cols_kernel,
        grid=(rows // tm,),
        in_specs=[row(D_MODEL)] + [_full_spec(a) for a in (g, wd, qn, kvn, wuqt, wuk, wuvt)]
        + [tab, tab, tabt, tabt],
        out_specs=[col(MLA_HEADS * MLA_QK_PAD), row(MLA_HEADS * MLA_QK_PAD), col(MLA_HEADS * MLA_VX)],
        out_shape=[jax.ShapeDtypeStruct((MLA_HEADS * MLA_QK_PAD, rows), BF16),
                   jax.ShapeDtypeStruct((rows, MLA_HEADS * MLA_QK_PAD), BF16),
                   jax.ShapeDtypeStruct((MLA_HEADS * MLA_VX, rows), BF16)],
        compiler_params=_cparams(("parallel",)),
        name="mla_proj_cols",
    )(h, g, wd, qn, kvn, wuqt, wuk, wuvt, cos, sin, cost, sint)


def _mla_attn_kernel(qt_ref, k_ref, vt_ref, qm_ref, km_ref, vm_ref, ki_ref, vti_ref,
                     o_ref, om_ref, mm_ref, lm_ref, am_ref, *, tk):
    i = pl.program_id(2)
    n_chunks = k_ref.shape[0] // tk
    qm = qm_ref[...]

    @pl.when(i == 0)
    def _():
        s = _dot_t(qm, km_ref[...])
        m = jnp.max(s, axis=-1, keepdims=True)
        p = jnp.exp2(s - m)
        mm_ref[...] = m
        lm_ref[...] = jnp.sum(p, axis=-1, keepdims=True)
        am_ref[...] = _dot(p.astype(BF16), vm_ref[...])

    n = qt_ref.shape[1] // MLA_STREAMS
    qts = [qt_ref[:, h * n:(h + 1) * n] for h in range(MLA_STREAMS)]
    score = lambda h, k: _dot(k, qts[h])
    meta = {"s": _dot_t(qm, ki_ref[...])}

    def meta_softmax():
        m_old = mm_ref[...]
        m = jnp.maximum(m_old, jnp.max(meta["s"], axis=-1, keepdims=True))
        meta["m"] = m
        meta["alpha"] = jnp.exp2(m_old - m)
        meta["p"] = jnp.exp2(meta["s"] - m)

    def meta_values():
        meta["o"] = _dot_t(meta["p"].astype(BF16), vti_ref[0:MLA_V, :])

    def meta_update():
        mm_ref[...] = meta["m"]
        lm_ref[...] = meta["alpha"] * lm_ref[...] + jnp.sum(meta["p"], axis=-1, keepdims=True)
        am_ref[...] = meta["alpha"] * am_ref[...] + meta["o"]

    stages = [(min(1, n_chunks - 1), meta_softmax), (min(2, n_chunks - 1), meta_values),
              (min(4, n_chunks - 1), meta_update)]

    pad = 128 - N_META
    vmt = jnp.concatenate([vm_ref[...].astype(F32), jnp.zeros((pad, MLA_V), F32)], axis=0).T
    vmt = jnp.concatenate([vmt, jnp.ones((MLA_VX - MLA_V, 128), F32)], axis=0).astype(BF16)
    km = km_ref[...]
    st = []
    for h in range(MLA_STREAMS):
        s = score(h, km)
        m = jnp.max(s, axis=0, keepdims=True)
        p = jnp.exp2(s - m)
        pm = jnp.concatenate([p.astype(BF16), jnp.zeros((pad, n), BF16)], axis=0)
        st.append({"m": m, "o_prev": _dot(vmt, pm), "acc": jnp.zeros((MLA_VX, n), F32),
                   "s_next": score(h, k_ref[0:tk, :])})
    for c in range(n_chunks):
        for h in range(MLA_STREAMS):
            t = st[h]
            s = t["s_next"]
            if c + 1 < n_chunks:
                t["s_next"] = score(h, k_ref[(c + 1) * tk:(c + 2) * tk, :])
            m_new = jnp.maximum(t["m"], jnp.max(s, axis=0, keepdims=True))
            alpha = jnp.exp2(t["m"] - m_new)
            p = jnp.exp2(s - m_new)
            o_c = _dot(vt_ref[:, c * tk:(c + 1) * tk], p.astype(BF16))
            t["acc"] = alpha * (t["acc"] + t["o_prev"])
            t["o_prev"] = o_c
            t["m"] = m_new
        for at, stage in stages:
            if at == c:
                stage()
    for h in range(MLA_STREAMS):
        acc = st[h]["acc"] + st[h]["o_prev"]
        o_ref[h * n:(h + 1) * n, :] = (
            acc[:MLA_V] * (1.0 / acc[MLA_V:MLA_V + 1])).T.astype(BF16)

    @pl.when(i == pl.num_programs(2) - 1)
    def _():
        om_ref[...] = (am_ref[...] / lm_ref[...]).astype(BF16)


def _mla_attn(qt, kc, vt, q_meta, kc_meta, v_meta, batch, seq0, tq, tk):
    rows = kc.shape[0]
    seq = rows // batch
    nq = seq // tq
    meta = lambda width: pl.BlockSpec((N_META, width), lambda s, h, i: (seq0 + s, h))
    return pl.pallas_call(
        functools.partial(_mla_attn_kernel, tk=tk),
        grid=(batch, MLA_HEADS, nq),
        in_specs=[
            pl.BlockSpec((MLA_QK_PAD, tq), lambda s, h, i: (h, s * nq + i)),
            pl.BlockSpec((seq, MLA_QK_PAD), lambda s, h, i: (s, h)),
            pl.BlockSpec((MLA_VX, seq), lambda s, h, i: (h, s)),
            meta(MLA_QK_PAD), meta(MLA_QK_PAD), meta(MLA_V),
            pl.BlockSpec((tq, MLA_QK_PAD), lambda s, h, i: (s * nq + i, h)),
            pl.BlockSpec((MLA_VX, tq), lambda s, h, i: (h, s * nq + i)),
        ],
        out_specs=[pl.BlockSpec((tq, MLA_V), lambda s, h, i: (s * nq + i, h)),
                   pl.BlockSpec((N_META, MLA_V), lambda s, h, i: (s, h))],
        out_shape=[jax.ShapeDtypeStruct((rows, MLA_HEADS * MLA_V), BF16),
                   jax.ShapeDtypeStruct((batch * N_META, MLA_HEADS * MLA_V), BF16)],
        scratch_shapes=[pltpu.VMEM((N_META, 1), F32), pltpu.VMEM((N_META, 1), F32),
                        pltpu.VMEM((N_META, MLA_V), F32)],
        compiler_params=_cparams(("parallel", "parallel", "arbitrary")),
        name="mla_attn",
    )(qt, kc, vt, q_meta, kc_meta, v_meta, kc, vt)


def _swa_proj_kernel(h_ref, g_ref, w_ref, q_ref, k_ref, v_ref):
    xn = _rms(h_ref[...], g_ref[...]).astype(BF16)
    dq = SWA_HEADS * SWA_HEAD_DIM
    dk = SWA_KV_HEADS * SWA_HEAD_DIM
    scale = SWA_HEAD_DIM ** -0.5
    for kvh in range(SWA_KV_HEADS):
        lo = kvh * SWA_GROUP * SWA_HEAD_DIM
        qk = _dot(xn, w_ref[:, lo:lo + SWA_GROUP * SWA_HEAD_DIM]) * scale
        for g in range(SWA_GROUP):
            q_ref[kvh * SWA_GROUP + g] = qk[:, g * SWA_HEAD_DIM:(g + 1) * SWA_HEAD_DIM].astype(BF16)
    kv = _dot(xn, w_ref[:, dq:dq + 2 * dk])
    for kvh in range(SWA_KV_HEADS):
        k_ref[kvh] = kv[:, kvh * SWA_HEAD_DIM:(kvh + 1) * SWA_HEAD_DIM].astype(BF16)
        v_ref[kvh] = kv[:, dk + kvh * SWA_HEAD_DIM:dk + (kvh + 1) * SWA_HEAD_DIM].astype(BF16)


def _swa_proj(h, g, w, layer, tm):
    rows = h.shape[0]
    hm = lambda n: pl.BlockSpec((n, tm, SWA_HEAD_DIM), lambda i: (0, i, 0))
    return pl.pallas_call(
        _swa_proj_kernel,
        grid=(rows // tm,),
        in_specs=[pl.BlockSpec((tm, D_MODEL), lambda i: (i, 0)),
                  _full_spec(g), _layer_spec(w, layer)],
        out_specs=[hm(SWA_HEADS), hm(SWA_KV_HEADS), hm(SWA_KV_HEADS)],
        out_shape=[jax.ShapeDtypeStruct((SWA_HEADS, rows, SWA_HEAD_DIM), BF16),
                   jax.ShapeDtypeStruct((SWA_KV_HEADS, rows, SWA_HEAD_DIM), BF16),
                   jax.ShapeDtypeStruct((SWA_KV_HEADS, rows, SWA_HEAD_DIM), BF16)],
        compiler_params=_cparams(("parallel",)),
        name="swa_proj",
    )(h, g, w)


def _swa_proj_cols_kernel(h_ref, g_ref, wqt_ref, wkv_ref, qt_ref, k_ref, v_ref):
    xn = _rms(h_ref[...], g_ref[...]).astype(BF16)
    qt = (_dot_t(wqt_ref[...], xn) * (SWA_HEAD_DIM ** -0.5 * LOG2E)).astype(BF16)
    for kvh in range(SWA_KV_HEADS):
        for g in range(SWA_GROUP):
            r0 = (kvh * SWA_GROUP + g) * SWA_HEAD_DIM
            for blk in range(qt.shape[1] // BLOCK):
                c0 = (blk * SWA_GROUP + g) * BLOCK
                qt_ref[kvh, :, c0:c0 + BLOCK] = qt[r0:r0 + SWA_HEAD_DIM, blk * BLOCK:(blk + 1) * BLOCK]
    dk = SWA_KV_HEADS * SWA_HEAD_DIM
    kv = _dot(xn, wkv_ref[...])
    for kvh in range(SWA_KV_HEADS):
        k_ref[kvh] = kv[:, kvh * SWA_HEAD_DIM:(kvh + 1) * SWA_HEAD_DIM].astype(BF16)
        v_ref[kvh] = kv[:, dk + kvh * SWA_HEAD_DIM:dk + (kvh + 1) * SWA_HEAD_DIM].astype(BF16)


def _swa_proj_cols(h, g, wqt, wkv, layer, tm):
    rows = h.shape[0]
    hm = pl.BlockSpec((SWA_KV_HEADS, tm, SWA_HEAD_DIM), lambda i: (0, i, 0))
    return pl.pallas_call(
        _swa_proj_cols_kernel,
        grid=(rows // tm,),
        in_specs=[pl.BlockSpec((tm, D_MODEL), lambda i: (i, 0)),
                  _full_spec(g), _layer_spec(wqt, layer), _layer_spec(wkv, layer)],
        out_specs=[pl.BlockSpec((SWA_KV_HEADS, SWA_HEAD_DIM, tm * SWA_GROUP), lambda i: (0, 0, i)),
                   hm, hm],
        out_shape=[jax.ShapeDtypeStruct((SWA_KV_HEADS, SWA_HEAD_DIM, rows * SWA_GROUP), BF16),
                   jax.ShapeDtypeStruct((SWA_KV_HEADS, rows, SWA_HEAD_DIM), BF16),
                   jax.ShapeDtypeStruct((SWA_KV_HEADS, rows, SWA_HEAD_DIM), BF16)],
        compiler_params=_cparams(("parallel",)),
        name="swa_proj_cols",
    )(h, g, wqt, wkv)


def _t5_bucket_exact(rel):
    nb = N_BUCKETS // 2
    max_exact = nb // 2
    table = []
    for n in range(int(np.abs(rel).max()) + 1):
        if n < max_exact:
            table.append(n)
        else:
            k = 0
            while 64 * 2 ** (k + 1) <= n * n:
                k += 1
            table.append(min(max_exact + k, nb - 1))
    return (np.asarray(table, np.int32)[np.abs(rel)] + np.where(rel > 0, nb, 0)).astype(np.int32)


def _swa_tile_codes(qpos, band_start):
    nq = qpos.shape[0]
    codes = np.full((nq, SWA_KEYS), -1, np.int32)
    kpos_band = N_META + band_start + np.arange(SWA_BAND)
    rel = kpos_band[None, :] - qpos[:, None]
    ok = np.abs(rel) <= WINDOW
    codes[:, :SWA_BAND] = np.where(ok, _t5_bucket_exact(rel), -1)
    rel_m = np.arange(N_META)[None, :] - qpos[:, None]
    codes[:, SWA_BAND:SWA_BAND + N_META] = _t5_bucket_exact(rel_m)
    codes[:, SINK_COL] = N_BUCKETS
    return codes


def _swa_bias_kernel(code_ref, rb_ref, sink_ref, o_ref, *, scale):
    hd = pl.program_id(1)
    code = code_ref[0]
    t = jnp.full(code.shape, NEG, F32)
    for b in range(N_BUCKETS):
        t = jnp.where(code == b, rb_ref[b, hd] * scale, t)
    t = jnp.where(code == N_BUCKETS, sink_ref[hd] * scale, t)
    o_ref[0, 0] = t


def _swa_bias(codes, rel_bias, sink, transposed):
    nv, a, b = codes.shape
    if transposed:
        out_spec = pl.BlockSpec((1, 1, a, b), lambda v, h: (v, h // SWA_GROUP, 0, h % SWA_GROUP))
        out_shape = (nv, SWA_KV_HEADS, a, SWA_GROUP * b)
    else:
        out_spec = pl.BlockSpec((1, 1, a, b), lambda v, h: (v, h, 0, 0))
        out_shape = (nv, SWA_HEADS, a, b)
    return pl.pallas_call(
        functools.partial(_swa_bias_kernel, scale=LOG2E if transposed else 1.0),
        grid=(nv, SWA_HEADS),
        in_specs=[pl.BlockSpec((1, a, b), lambda v, h: (v, 0, 0)),
                  pl.BlockSpec(memory_space=pltpu.SMEM),
                  pl.BlockSpec(memory_space=pltpu.SMEM)],
        out_specs=out_spec,
        out_shape=jax.ShapeDtypeStruct(out_shape, F32),
        compiler_params=_cparams(("parallel", "parallel")),
        name="swa_bias",
    )(codes, rel_bias, sink)


def _swa_attn_meta_kernel(q_ref, k_ref, v_ref, km_ref, vm_ref, b_ref, o_ref):
    pad = jnp.zeros((SWA_KEYS - SWA_BAND - N_META, SWA_HEAD_DIM), BF16)
    kt = jnp.concatenate([k_ref[0, 0:SWA_BAND, :], km_ref[0], pad], axis=0)
    vt = jnp.concatenate([v_ref[0, 0:SWA_BAND, :], vm_ref[0], pad], axis=0)
    q = q_ref[...].reshape(SWA_GROUP * N_META, SWA_HEAD_DIM)
    s = _dot_t(q, kt) + b_ref[0].reshape(SWA_GROUP * N_META, SWA_KEYS)
    m = jnp.max(s, axis=-1, keepdims=True)
    e = jnp.exp(s - m)
    l = jnp.sum(e, axis=-1, keepdims=True)
    o = (_dot(e.astype(BF16), vt) / l).reshape(SWA_GROUP, N_META, SWA_HEAD_DIM)
    o_ref[...] = jnp.concatenate([o[g] for g in range(SWA_GROUP)], axis=1).astype(BF16)


def _swa_attn(q_meta, k, v, k_meta, v_meta, bias, batch, seq0):
    seq = k.shape[1] // batch
    meta = lambda n: pl.BlockSpec((n, N_META, SWA_HEAD_DIM), lambda kv, s: (kv, seq0 + s, 0))
    real = pl.BlockSpec((1, seq, SWA_HEAD_DIM), lambda kv, s: (kv, s, 0))
    return pl.pallas_call(
        _swa_attn_meta_kernel,
        grid=(SWA_KV_HEADS, batch),
        in_specs=[meta(SWA_GROUP), real, real, meta(1), meta(1),
                  pl.BlockSpec((1, SWA_GROUP, N_META, SWA_KEYS), lambda kv, s: (0, kv, 0, 0))],
        out_specs=pl.BlockSpec((N_META, SWA_GROUP * SWA_HEAD_DIM), lambda kv, s: (s, kv)),
        out_shape=jax.ShapeDtypeStruct((batch * N_META, SWA_HEADS * SWA_HEAD_DIM), BF16),
        compiler_params=_cparams(("parallel", "parallel")),
        name="swa_meta_attn",
    )(q_meta, k, v, k_meta, v_meta, bias)


def _swa_attn_cols_kernel(qt_ref, k_ref, v_ref, km_ref, vm_ref, b_ref, o_ref):
    lanes = SWA_GROUP * BLOCK
    nsub = qt_ref.shape[2] // lanes
    seq = k_ref.shape[1]
    n_blocks = seq // BLOCK
    used = SWA_BAND + 2 * N_META
    kpad = jnp.zeros((used - SWA_BAND - N_META, SWA_HEAD_DIM), BF16)
    vpad = jnp.zeros((SWA_KEYS - SWA_BAND - N_META, SWA_HEAD_DIM), BF16)
    ppad = jnp.zeros((SWA_KEYS - used, lanes), BF16)
    ext = jnp.concatenate([jnp.ones((SWA_KEYS, 16), F32),
                           jnp.zeros((SWA_KEYS, 128 - SWA_HEAD_DIM - 16), F32)], axis=1)

    def window(j):
        b = pl.program_id(2) * nsub + j
        start = pl.multiple_of(jnp.clip((b - 1) * BLOCK, 0, seq - SWA_BAND), BLOCK)
        variant = jnp.where(b == 0, 1, jnp.where(b == n_blocks - 1, 2, 0))
        return start, variant

    def scores(j):
        start, variant = window(j)
        kt = jnp.concatenate([k_ref[0, pl.ds(start, SWA_BAND), :], km_ref[0], kpad], axis=0)
        return (_dot(kt, qt_ref[0, :, j * lanes:(j + 1) * lanes])
                + b_ref[variant, 0, 0:used, :])

    s_next = scores(0)
    for j in range(nsub):
        s = s_next
        if j + 1 < nsub:
            s_next = scores(j + 1)
        start, _ = window(j)
        vrows = jnp.concatenate([v_ref[0, pl.ds(start, SWA_BAND), :], vm_ref[0], vpad], axis=0)
        vt = jnp.concatenate([vrows.astype(F32), ext], axis=1).T.astype(BF16)
        m = jnp.max(s, axis=0, keepdims=True)
        p = jnp.concatenate([jnp.exp2(s - m).astype(BF16), ppad], axis=0)
        acc = _dot(vt, p)
        acc = acc * (1.0 / acc[SWA_HEAD_DIM:SWA_HEAD_DIM + 1])
        o = acc.T[:, :SWA_HEAD_DIM].reshape(SWA_GROUP, BLOCK, SWA_HEAD_DIM)
        o_ref[j * BLOCK:(j + 1) * BLOCK, :] = jnp.concatenate(
            [o[g] for g in range(SWA_GROUP)], axis=1).astype(BF16)


def _swa_attn_cols(qt, k, v, k_meta, v_meta, bias, batch, seq0, tq):
    rows = k.shape[1]
    seq = rows // batch
    nq = seq // tq
    kv_spec = pl.BlockSpec((1, seq, SWA_HEAD_DIM), lambda kv, s, i: (kv, s, 0))
    meta_spec = pl.BlockSpec((1, N_META, SWA_HEAD_DIM), lambda kv, s, i: (kv, seq0 + s, 0))
    return pl.pallas_call(
        _swa_attn_cols_kernel,
        grid=(SWA_KV_HEADS, batch, nq),
        in_specs=[
            pl.BlockSpec((1, SWA_HEAD_DIM, tq * SWA_GROUP), lambda kv, s, i: (kv, 0, s * nq + i)),
            kv_spec, kv_spec, meta_spec, meta_spec,
            pl.BlockSpec((bias.shape[0], 1) + bias.shape[2:], lambda kv, s, i: (0, kv, 0, 0)),
        ],
        out_specs=pl.BlockSpec((tq, SWA_GROUP * SWA_HEAD_DIM), lambda kv, s, i: (s * nq + i, kv)),
        out_shape=jax.ShapeDtypeStruct((rows, SWA_HEADS * SWA_HEAD_DIM), BF16),
        compiler_params=_cparams(("parallel", "parallel", "arbitrary")),
        name="swa_attn_cols",
    )(qt, k, v, k_meta, v_meta, bias)


def _attn_out_kernel(o_ref, w_ref, g_ref, h_ref, out_ref):
    m = _dot(o_ref[...], w_ref[...])
    out_ref[...] = h_ref[...] + _rms(m, g_ref[...])


def _attn_out(o, w, layer, g, h, tm):
    rows = h.shape[0]
    return pl.pallas_call(
        _attn_out_kernel,
        grid=(rows // tm,),
        in_specs=[pl.BlockSpec((tm, o.shape[1]), lambda i: (i, 0)),
                  _layer_spec(w, layer), _full_spec(g),
                  pl.BlockSpec((tm, D_MODEL), lambda i: (i, 0))],
        out_specs=pl.BlockSpec((tm, D_MODEL), lambda i: (i, 0)),
        out_shape=jax.ShapeDtypeStruct((rows, D_MODEL), F32),
        compiler_params=_cparams(("parallel",)),
        name="attn_out",
    )(o, w, g, h)


def _mlp_kernel(h_ref, gpre_ref, win_ref, wout_ref, gpost_ref, out_ref, xn_ref):
    j = pl.program_id(1)
    last = pl.num_programs(1) - 1

    def ffn(xn):
        u = jnp.maximum(_dot(xn, win_ref[...]), 0.0)
        return _dot((u * u).astype(BF16), wout_ref[...])

    @pl.when(j == 0)
    def _():
        xn = _rms(h_ref[...], gpre_ref[...]).astype(BF16)
        xn_ref[...] = xn
        out_ref[...] = ffn(xn)

    @pl.when(jnp.logical_and(j > 0, j < last))
    def _():
        out_ref[...] += ffn(xn_ref[...])

    @pl.when(j == last)
    def _():
        f = out_ref[...] + ffn(xn_ref[...])
        out_ref[...] = h_ref[...] + _rms(f, gpost_ref[...])


def _mlp(h, gpre, w_in, w_out, layer, gpost, tm, tf):
    rows = h.shape[0]
    return pl.pallas_call(
        _mlp_kernel,
        grid=(rows // tm, D_FF // tf),
        in_specs=[pl.BlockSpec((tm, D_MODEL), lambda i, j: (i, 0)),
                  pl.BlockSpec(gpre.shape, lambda i, j: (0, 0)),
                  pl.BlockSpec((None, D_MODEL, tf), lambda i, j: (layer, 0, j)),
                  pl.BlockSpec((None, tf, D_MODEL), lambda i, j: (layer, j, 0)),
                  pl.BlockSpec(gpost.shape, lambda i, j: (0, 0))],
        out_specs=pl.BlockSpec((tm, D_MODEL), lambda i, j: (i, 0)),
        out_shape=jax.ShapeDtypeStruct((rows, D_MODEL), F32),
        scratch_shapes=[pltpu.VMEM((tm, D_MODEL), BF16)],
        compiler_params=_cparams(("parallel", "arbitrary")),
        name="mlp",
    )(h, gpre, w_in, w_out, gpost)


def _rope_dup(w):
    return jnp.concatenate([w, w], axis=-1)


def _prep_mla(w_dq, q_norm, w_uq, w_dkv, kv_norm, w_ukv):
    wd = jnp.concatenate([w_dq, w_dkv[:, :MLA_KV_LORA], _rope_dup(w_dkv[:, MLA_KV_LORA:])], axis=1)
    uq = w_uq.reshape(MLA_Q_LORA, MLA_HEADS, MLA_NOPE + MLA_ROPE)
    wuq = jnp.concatenate([uq[..., :MLA_NOPE], _rope_dup(uq[..., MLA_NOPE:])], axis=-1)
    wuq = wuq.reshape(MLA_Q_LORA, MLA_HEADS * MLA_QK_PAD)
    ukv = w_ukv.reshape(MLA_KV_LORA, MLA_HEADS, MLA_NOPE + MLA_V)
    wuk = ukv[..., :MLA_NOPE].reshape(MLA_KV_LORA, MLA_HEADS * MLA_NOPE)
    wuv = ukv[..., MLA_NOPE:].reshape(MLA_KV_LORA, MLA_HEADS * MLA_V)
    wd, wuq, wuk, wuv = (t.astype(BF16) for t in (wd, wuq, wuk, wuv))
    rows = (wd, q_norm[None], kv_norm[None], wuq, wuk, wuv)
    cols = (wd, q_norm[None], kv_norm[None], wuq.T, wuk, wuv.T)
    return rows, cols


def _rope_tables(pos):
    half = MLA_ROPE // 2
    inv = ROPE_THETA ** (-(jnp.arange(half, dtype=F32) / half))
    ang = pos.astype(F32)[:, None] * inv[None, :]
    cos, sin = jnp.cos(ang), jnp.sin(ang)
    zero = jnp.zeros((pos.shape[0], 128 - MLA_ROPE), F32)
    return (jnp.concatenate([cos, cos, zero], axis=1), jnp.concatenate([-sin, sin, zero], axis=1),
            cos.T, sin.T)


def _trunks(xs, meta_tokens, p):
    seq = xs[0].shape[1]
    batches = [x.shape[0] for x in xs]
    seq0 = [sum(batches[:g]) for g in range(len(xs))]
    total = sum(batches)
    hs = [x.reshape(b * seq, D_MODEL) for x, b in zip(xs, batches)]
    rows_m = total * N_META
    hm = jnp.broadcast_to(meta_tokens[None], (total, N_META, D_MODEL)).reshape(rows_m, D_MODEL)
    tabs_r = _rope_tables(N_META + jnp.arange(seq))
    cos_m, sin_m = (jnp.tile(t, (total, 1)) for t in _rope_tables(jnp.arange(N_META))[:2])
    for i in range(DEPTH):
        j = i // 2
        g_pre = p["norm_mix_pre"][i][None]
        g_post = p["norm_mix_post"][i][None]
        os, oms = [], []
        if i % 2 == 0:
            w_rows, w_cols = p["mla"][j]
            q_m, kc_m, v_m = _mla_proj_rows(hm, g_pre, w_rows, cos_m, sin_m)
            for h, b, s0 in zip(hs, batches, seq0):
                qt, kc, vt = _mla_proj_cols(h, g_pre, w_cols, tabs_r, 256, seq)
                o, o_m = _mla_attn(qt, kc, vt, q_m, kc_m, v_m, b, s0, min(2048, seq), 512)
                os.append(o)
                oms.append(o_m)
            w_o = p["mla_w_o"]
        else:
            q_m, k_m, v_m = _swa_proj(hm, g_pre, p["swa_w_qkv"], j, rows_m)
            for h, b, s0 in zip(hs, batches, seq0):
                qt, k, v = _swa_proj_cols(h, g_pre, p["swa_wqt"], p["swa_wkv"], j, 512)
                os.append(_swa_attn_cols(qt, k, v, k_m, v_m, p["swa_bias"][j], b, s0,
                                         min(1024, seq)))
                oms.append(_swa_attn(q_m, k, v, k_m, v_m, p["swa_bias_meta"][j], b, s0))
            w_o = p["swa_w_o"]
        mlp_args = (p["norm_mlp_pre"][i][None], p["mlp_w_in"], p["mlp_w_out"], i,
                    p["norm_mlp_post"][i][None])
        hs = [_attn_out(o, w_o, j, g_post, h, 512) for o, h in zip(os, hs)]
        hs = [_mlp(h, *mlp_args, min(1024, h.shape[0]), 512) for h in hs]
        hm = _attn_out(jnp.concatenate(oms, axis=0), w_o, j, g_post, hm, rows_m)
        hm = _mlp(hm, *mlp_args, rows_m, 512)
    return tuple(h.reshape(b, seq, D_MODEL) for h, b in zip(hs, batches))


def kernel(x_prompt, x_sample, meta_tokens, rel_bias, mla_w_dq, mla_q_norm, mla_w_uq, mla_w_dkv, mla_kv_norm, mla_w_ukv, mla_w_o, swa_w_qkv, swa_w_o, swa_sink, mlp_w_in, mlp_w_out, norm_mix_pre, norm_mix_post, norm_mlp_pre, norm_mlp_post):
    seq = x_prompt.shape[1]
    assert x_sample.shape[1] == seq and seq % 512 == 0
    n_mla = mla_w_dq.shape[0]
    n_swa = swa_w_qkv.shape[0]
    dq = SWA_HEADS * SWA_HEAD_DIM
    qpos = N_META + np.arange(BLOCK)
    codes = np.stack([
        _swa_tile_codes(qpos + BLOCK, 0),
        _swa_tile_codes(qpos, 0),
        _swa_tile_codes(qpos + 2 * BLOCK, 0),
    ])
    codes_meta = _swa_tile_codes(np.arange(N_META), 0)[None]
    p = {
        "mla": [_prep_mla(mla_w_dq[j], mla_q_norm[j], mla_w_uq[j], mla_w_dkv[j], mla_kv_norm[j],
                          mla_w_ukv[j]) for j in range(n_mla)],
        "mla_w_o": mla_w_o.astype(BF16),
        "swa_w_qkv": swa_w_qkv.astype(BF16),
        "swa_wqt": jnp.swapaxes(swa_w_qkv[:, :, :dq], 1, 2).astype(BF16),
        "swa_wkv": swa_w_qkv[:, :, dq:].astype(BF16),
        "swa_w_o": swa_w_o.astype(BF16),
        "swa_bias": [_swa_bias(jnp.asarray(codes.transpose(0, 2, 1)), rel_bias, swa_sink[j], True)
                     for j in range(n_swa)],
        "swa_bias_meta": [_swa_bias(jnp.asarray(codes_meta), rel_bias, swa_sink[j], False)
                          for j in range(n_swa)],
        "mlp_w_in": mlp_w_in.astype(BF16),
        "mlp_w_out": mlp_w_out.astype(BF16),
        "norm_mix_pre": norm_mix_pre, "norm_mix_post": norm_mix_post,
        "norm_mlp_pre": norm_mlp_pre, "norm_mlp_post": norm_mlp_post,
    }
    return _trunks((x_prompt, x_sample), meta_tokens, p)
```

```python
import functools
import math

import jax
import jax.numpy as jnp
import numpy as np
from jax import lax
from jax.experimental import pallas as pl
from jax.experimental.pallas import tpu as pltpu

D_MODEL = 2048
DEPTH = 4
N_META = 16
MLA_HEADS = 16
MLA_Q_LORA = 512
MLA_KV_LORA = 512
MLA_NOPE = 128
MLA_ROPE = 64
MLA_V = 128
MLA_VX = MLA_V + 16
MLA_QK_PAD = 256
MLA_STREAMS = 8
ROPE_THETA = 10000.0
SWA_HEADS = 32
SWA_KV_HEADS = 4
SWA_GROUP = SWA_HEADS // SWA_KV_HEADS
SWA_HEAD_DIM = 64
WINDOW = 128
BLOCK = 128
N_BUCKETS = 32
MAX_DISTANCE = 128
D_FF = 4 * D_MODEL
EPS = 1e-6

SWA_BAND = 3 * BLOCK
SWA_KEYS = 512
SINK_COL = SWA_BAND + N_META
NEG = -1e30

LOG2E = math.log2(math.e)
VMEM_LIMIT_V7X = 56 * 1024 * 1024

BF16 = jnp.bfloat16
F32 = jnp.float32


def _cparams(sem):
    return pltpu.CompilerParams(dimension_semantics=sem, vmem_limit_bytes=VMEM_LIMIT_V7X)


def _rms(x, g):
    r = lax.rsqrt(jnp.mean(x * x, axis=-1, keepdims=True) + EPS)
    return x * r * g


def _dot(a, b):
    return jnp.dot(a, b, preferred_element_type=F32)


def _dot_t(a, b):
    return lax.dot_general(a, b, (((1,), (1,)), ((), ())), preferred_element_type=F32)


def _rope128(y, cos, sin):
    return y * cos + pltpu.roll(y, 32, axis=1) * sin


MLA_SCALE = (MLA_NOPE + MLA_ROPE) ** -0.5 * LOG2E


def _mla_latents(h_ref, g_ref, wd_ref, qn_ref, kvn_ref):
    xn = _rms(h_ref[...], g_ref[...]).astype(BF16)
    a = _dot(xn, wd_ref[...])
    cq = _rms(a[:, :MLA_Q_LORA], qn_ref[...]).astype(BF16)
    ckv = _rms(a[:, MLA_Q_LORA:MLA_Q_LORA + MLA_KV_LORA], kvn_ref[...]).astype(BF16)
    return cq, ckv, a[:, MLA_Q_LORA + MLA_KV_LORA:]


def _store_keys(kc_ref, ckv, wuk_ref, kr):
    for hd in range(MLA_HEADS):
        lo = hd * MLA_QK_PAD
        kc_ref[:, lo:lo + MLA_NOPE] = _dot(
            ckv, wuk_ref[:, hd * MLA_NOPE:(hd + 1) * MLA_NOPE]).astype(BF16)
        kc_ref[:, lo + MLA_NOPE:lo + MLA_QK_PAD] = kr


def _mla_proj_rows_kernel(h_ref, g_ref, wd_ref, qn_ref, kvn_ref, wuq_ref, wuk_ref, wuv_ref,
                          cos_ref, sin_ref, q_ref, kc_ref, v_ref):
    cq, ckv, kr = _mla_latents(h_ref, g_ref, wd_ref, qn_ref, kvn_ref)
    cos = cos_ref[...]
    sin = sin_ref[...]
    _store_keys(kc_ref, ckv, wuk_ref, _rope128(kr, cos, sin).astype(BF16))
    v_ref[...] = _dot(ckv, wuv_ref[...]).astype(BF16)
    for hd in range(MLA_HEADS):
        lo = hd * MLA_QK_PAD
        qh = _dot(cq, wuq_ref[:, lo:lo + MLA_QK_PAD])
        q_ref[:, lo:lo + MLA_NOPE] = (qh[:, :MLA_NOPE] * MLA_SCALE).astype(BF16)
        q_ref[:, lo + MLA_NOPE:lo + MLA_QK_PAD] = (
            _rope128(qh[:, MLA_NOPE:], cos, sin) * MLA_SCALE).astype(BF16)


def _mla_proj_cols_kernel(h_ref, g_ref, wd_ref, qn_ref, kvn_ref, wuqt_ref, wuk_ref, wuvt_ref,
                          cos_ref, sin_ref, cost_ref, sint_ref, qt_ref, kc_ref, vt_ref):
    cq, ckv, kr = _mla_latents(h_ref, g_ref, wd_ref, qn_ref, kvn_ref)
    _store_keys(kc_ref, ckv, wuk_ref, _rope128(kr, cos_ref[...], sin_ref[...]).astype(BF16))
    tm = cq.shape[0]
    c = cost_ref[...]
    s = sint_ref[...]
    half = MLA_ROPE // 2
    ones = jnp.ones((MLA_VX - MLA_V, tm), BF16)
    zeros = jnp.zeros((MLA_QK_PAD - MLA_NOPE - MLA_ROPE, tm), BF16)
    hg = 4
    for h0 in range(0, MLA_HEADS, hg):
        vt = _dot_t(wuvt_ref[h0 * MLA_V:(h0 + hg) * MLA_V, :], ckv).astype(BF16)
        qt = _dot_t(wuqt_ref[h0 * MLA_QK_PAD:(h0 + hg) * MLA_QK_PAD, :], cq)
        for k in range(hg):
            vo = (h0 + k) * MLA_VX
            vt_ref[vo:vo + MLA_V, :] = vt[k * MLA_V:(k + 1) * MLA_V]
            vt_ref[vo + MLA_V:vo + MLA_VX, :] = ones
            lo = k * MLA_QK_PAD
            qo = (h0 + k) * MLA_QK_PAD
            qt_ref[qo:qo + MLA_NOPE, :] = (qt[lo:lo + MLA_NOPE] * MLA_SCALE).astype(BF16)
            x1 = qt[lo + MLA_NOPE:lo + MLA_NOPE + half]
            x2 = qt[lo + MLA_NOPE + half:lo + MLA_NOPE + MLA_ROPE]
            qt_ref[qo + MLA_NOPE:qo + MLA_NOPE + half, :] = (
                (x1 * c - x2 * s) * MLA_SCALE).astype(BF16)
            qt_ref[qo + MLA_NOPE + half:qo + MLA_NOPE + MLA_ROPE, :] = (
                (x2 * c + x1 * s) * MLA_SCALE).astype(BF16)
            qt_ref[qo + MLA_NOPE + MLA_ROPE:qo + MLA_QK_PAD, :] = zeros


def _full_spec(a, buffers=None):
    mode = None if buffers is None else pl.Buffered(buffers)
    return pl.BlockSpec(a.shape, lambda *_: (0,) * a.ndim, pipeline_mode=mode)


def _layer_spec(w, layer):
    return pl.BlockSpec((None,) + w.shape[1:], lambda *_: (layer,) + (0,) * (w.ndim - 1))


def _mla_proj_rows(h, g, w, cos, sin):
    rows = h.shape[0]
    wd, qn, kvn, wuq, wuk, wuv = w
    args = (h, g, wd, qn, kvn, wuq, wuk, wuv, cos, sin)
    widths = (MLA_HEADS * MLA_QK_PAD, MLA_HEADS * MLA_QK_PAD, MLA_HEADS * MLA_V)
    return pl.pallas_call(
        _mla_proj_rows_kernel,
        grid=(1,),
        in_specs=[_full_spec(a) for a in args],
        out_specs=[pl.BlockSpec((rows, n), lambda i: (0, 0)) for n in widths],
        out_shape=[jax.ShapeDtypeStruct((rows, n), BF16) for n in widths],
        compiler_params=_cparams(("arbitrary",)),
        name="mla_proj_rows",
    )(*args)


def _mla_proj_cols(h, g, w, tabs, tm, seq):
    rows = h.shape[0]
    wd, qn, kvn, wuqt, wuk, wuvt = w
    cos, sin, cost, sint = tabs
    nblk = seq // tm
    row = lambda width: pl.BlockSpec((tm, width), lambda i: (i, 0))
    col = lambda height: pl.BlockSpec((height, tm), lambda i: (0, i))
    tab = pl.BlockSpec((tm, 128), lambda i: (i % nblk, 0))
    tabt = pl.BlockSpec((MLA_ROPE // 2, tm), lambda i: (0, i % nblk))
    return pl.pallas_call(
        _mla_proj_cols_kernel,
        grid=(rows // tm,),
        in_specs=[row(D_MODEL)] + [_full_spec(a, 1) for a in (g, wd, qn, kvn, wuqt, wuk, wuvt)]
        + [tab, tab, tabt, tabt],
        out_specs=[col(MLA_HEADS * MLA_QK_PAD), row(MLA_HEADS * MLA_QK_PAD), col(MLA_HEADS * MLA_VX)],
        out_shape=[jax.ShapeDtypeStruct((MLA_HEADS * MLA_QK_PAD, rows), BF16),
                   jax.ShapeDtypeStruct((rows, MLA_HEADS * MLA_QK_PAD), BF16),
                   jax.ShapeDtypeStruct((MLA_HEADS * MLA_VX, rows), BF16)],
        compiler_params=_cparams(("parallel",)),
        name="mla_proj_cols",
    )(h, g, wd, qn, kvn, wuqt, wuk, wuvt, cos, sin, cost, sint)


def _mla_attn_kernel(qt_ref, k_ref, vt_ref, qm_ref, km_ref, vm_ref, ki_ref, vti_ref,
                     o_ref, om_ref, mm_ref, lm_ref, am_ref, *, tk):
    i = pl.program_id(2)
    n_chunks = k_ref.shape[0] // tk
    qm = qm_ref[...]

    @pl.when(i == 0)
    def _():
        s = _dot_t(qm, km_ref[...])
        m = jnp.max(s, axis=-1, keepdims=True)
        p = jnp.exp2(s - m)
        mm_ref[...] = m
        lm_ref[...] = jnp.sum(p, axis=-1, keepdims=True)
        am_ref[...] = _dot(p.astype(BF16), vm_ref[...])

    n = qt_ref.shape[1] // MLA_STREAMS
    qts = [qt_ref[:, h * n:(h + 1) * n] for h in range(MLA_STREAMS)]
    score = lambda h, k: _dot(k, qts[h])
    meta = {"s": _dot_t(qm, ki_ref[...])}

    def meta_softmax():
        m_old = mm_ref[...]
        m = jnp.maximum(m_old, jnp.max(meta["s"], axis=-1, keepdims=True))
        meta["m"] = m
        meta["alpha"] = jnp.exp2(m_old - m)
        meta["p"] = jnp.exp2(meta["s"] - m)

    def meta_values():
        meta["o"] = _dot_t(meta["p"].astype(BF16), vti_ref[0:MLA_V, :])

    def meta_update():
        mm_ref[...] = meta["m"]
        lm_ref[...] = meta["alpha"] * lm_ref[...] + jnp.sum(meta["p"], axis=-1, keepdims=True)
        am_ref[...] = meta["alpha"] * am_ref[...] + meta["o"]

    stages = [(min(1, n_chunks - 1), meta_softmax), (min(2, n_chunks - 1), meta_values),
              (min(4, n_chunks - 1), meta_update)]

    pad = 128 - N_META
    vmt = jnp.concatenate([vm_ref[...].astype(F32), jnp.zeros((pad, MLA_V), F32)], axis=0).T
    vmt = jnp.concatenate([vmt, jnp.ones((MLA_VX - MLA_V, 128), F32)], axis=0).astype(BF16)
    km = km_ref[...]
    st = []
    for h in range(MLA_STREAMS):
        s = score(h, km)
        m = jnp.max(s, axis=0, keepdims=True)
        p = jnp.exp2(s - m)
        pm = jnp.concatenate([p.astype(BF16), jnp.zeros((pad, n), BF16)], axis=0)
        st.append({"m": m, "o_prev": _dot(vmt, pm), "acc": jnp.zeros((MLA_VX, n), F32),
                   "s_next": score(h, k_ref[0:tk, :])})
    for c in range(n_chunks):
        for h in range(MLA_STREAMS):
            t = st[h]
            s = t["s_next"]
            if c + 1 < n_chunks:
                t["s_next"] = score(h, k_ref[(c + 1) * tk:(c + 2) * tk, :])
            m_new = jnp.maximum(t["m"], jnp.max(s, axis=0, keepdims=True))
            alpha = jnp.exp2(t["m"] - m_new)
            p = jnp.exp2(s - m_new)
            o_c = _dot(vt_ref[:, c * tk:(c + 1) * tk], p.astype(BF16))
            t["acc"] = alpha * (t["acc"] + t["o_prev"])
            t["o_prev"] = o_c
            t["m"] = m_new
        for at, stage in stages:
            if at == c:
                stage()
    for h in range(MLA_STREAMS):
        acc = st[h]["acc"] + st[h]["o_prev"]
        o_ref[h * n:(h + 1) * n, :] = (
            acc[:MLA_V] * (1.0 / acc[MLA_V:MLA_V + 1])).T.astype(BF16)

    @pl.when(i == pl.num_programs(2) - 1)
    def _():
        om_ref[...] = (am_ref[...] / lm_ref[...]).astype(BF16)


def _mla_attn(qt, kc, vt, q_meta, kc_meta, v_meta, batch, seq0, tq, tk):
    rows = kc.shape[0]
    seq = rows // batch
    nq = seq // tq
    meta = lambda width: pl.BlockSpec((N_META, width), lambda s, h, i: (seq0 + s, h))
    return pl.pallas_call(
        functools.partial(_mla_attn_kernel, tk=tk),
        grid=(batch, MLA_HEADS, nq),
        in_specs=[
            pl.BlockSpec((MLA_QK_PAD, tq), lambda s, h, i: (h, s * nq + i)),
            pl.BlockSpec((seq, MLA_QK_PAD), lambda s, h, i: (s, h)),
            pl.BlockSpec((MLA_VX, seq), lambda s, h, i: (h, s)),
            meta(MLA_QK_PAD), meta(MLA_QK_PAD), meta(MLA_V),
            pl.BlockSpec((tq, MLA_QK_PAD), lambda s, h, i: (s * nq + i, h)),
            pl.BlockSpec((MLA_VX, tq), lambda s, h, i: (h, s * nq + i)),
        ],
        out_specs=[pl.BlockSpec((tq, MLA_V), lambda s, h, i: (s * nq + i, h)),
                   pl.BlockSpec((N_META, MLA_V), lambda s, h, i: (s, h))],
        out_shape=[jax.ShapeDtypeStruct((rows, MLA_HEADS * MLA_V), BF16),
                   jax.ShapeDtypeStruct((batch * N_META, MLA_HEADS * MLA_V), BF16)],
        scratch_shapes=[pltpu.VMEM((N_META, 1), F32), pltpu.VMEM((N_META, 1), F32),
                        pltpu.VMEM((N_META, MLA_V), F32)],
        compiler_params=_cparams(("parallel", "parallel", "arbitrary")),
        name="mla_attn",
    )(qt, kc, vt, q_meta, kc_meta, v_meta, kc, vt)


def _swa_proj_kernel(h_ref, g_ref, w_ref, q_ref, k_ref, v_ref):
    xn = _rms(h_ref[...], g_ref[...]).astype(BF16)
    dq = SWA_HEADS * SWA_HEAD_DIM
    dk = SWA_KV_HEADS * SWA_HEAD_DIM
    scale = SWA_HEAD_DIM ** -0.5
    for kvh in range(SWA_KV_HEADS):
        lo = kvh * SWA_GROUP * SWA_HEAD_DIM
        qk = _dot(xn, w_ref[:, lo:lo + SWA_GROUP * SWA_HEAD_DIM]) * scale
        for g in range(SWA_GROUP):
            q_ref[kvh * SWA_GROUP + g] = qk[:, g * SWA_HEAD_DIM:(g + 1) * SWA_HEAD_DIM].astype(BF16)
    kv = _dot(xn, w_ref[:, dq:dq + 2 * dk])
    for kvh in range(SWA_KV_HEADS):
        k_ref[kvh] = kv[:, kvh * SWA_HEAD_DIM:(kvh + 1) * SWA_HEAD_DIM].astype(BF16)
        v_ref[kvh] = kv[:, dk + kvh * SWA_HEAD_DIM:dk + (kvh + 1) * SWA_HEAD_DIM].astype(BF16)


def _swa_proj(h, g, w, layer, tm):
    rows = h.shape[0]
    hm = lambda n: pl.BlockSpec((n, tm, SWA_HEAD_DIM), lambda i: (0, i, 0))
    return pl.pallas_call(
        _swa_proj_kernel,
        grid=(rows // tm,),
        in_specs=[pl.BlockSpec((tm, D_MODEL), lambda i: (i, 0)),
                  _full_spec(g), _layer_spec(w, layer)],
        out_specs=[hm(SWA_HEADS), hm(SWA_KV_HEADS), hm(SWA_KV_HEADS)],
        out_shape=[jax.ShapeDtypeStruct((SWA_HEADS, rows, SWA_HEAD_DIM), BF16),
                   jax.ShapeDtypeStruct((SWA_KV_HEADS, rows, SWA_HEAD_DIM), BF16),
                   jax.ShapeDtypeStruct((SWA_KV_HEADS, rows, SWA_HEAD_DIM), BF16)],
        compiler_params=_cparams(("parallel",)),
        name="swa_proj",
    )(h, g, w)


def _swa_proj_cols_kernel(h_ref, g_ref, wqt_ref, wkv_ref, qt_ref, k_ref, v_ref):
    xn = _rms(h_ref[...], g_ref[...]).astype(BF16)
    qt = (_dot_t(wqt_ref[...], xn) * (SWA_HEAD_DIM ** -0.5 * LOG2E)).astype(BF16)
    for kvh in range(SWA_KV_HEADS):
        for g in range(SWA_GROUP):
            r0 = (kvh * SWA_GROUP + g) * SWA_HEAD_DIM
            for blk in range(qt.shape[1] // BLOCK):
                c0 = (blk * SWA_GROUP + g) * BLOCK
                qt_ref[kvh, :, c0:c0 + BLOCK] = qt[r0:r0 + SWA_HEAD_DIM, blk * BLOCK:(blk + 1) * BLOCK]
    dk = SWA_KV_HEADS * SWA_HEAD_DIM
    kv = _dot(xn, wkv_ref[...])
    for kvh in range(SWA_KV_HEADS):
        k_ref[kvh] = kv[:, kvh * SWA_HEAD_DIM:(kvh + 1) * SWA_HEAD_DIM].astype(BF16)
        v_ref[kvh] = kv[:, dk + kvh * SWA_HEAD_DIM:dk + (kvh + 1) * SWA_HEAD_DIM].astype(BF16)


def _swa_proj_cols(h, g, wqt, wkv, layer, tm):
    rows = h.shape[0]
    hm = pl.BlockSpec((SWA_KV_HEADS, tm, SWA_HEAD_DIM), lambda i: (0, i, 0))
    return pl.pallas_call(
        _swa_proj_cols_kernel,
        grid=(rows // tm,),
        in_specs=[pl.BlockSpec((tm, D_MODEL), lambda i: (i, 0)),
                  _full_spec(g), _layer_spec(wqt, layer), _layer_spec(wkv, layer)],
        out_specs=[pl.BlockSpec((SWA_KV_HEADS, SWA_HEAD_DIM, tm * SWA_GROUP), lambda i: (0, 0, i)),
                   hm, hm],
        out_shape=[jax.ShapeDtypeStruct((SWA_KV_HEADS, SWA_HEAD_DIM, rows * SWA_GROUP), BF16),
                   jax.ShapeDtypeStruct((SWA_KV_HEADS, rows, SWA_HEAD_DIM), BF16),
                   jax.ShapeDtypeStruct((SWA_KV_HEADS, rows, SWA_HEAD_DIM), BF16)],
        compiler_params=_cparams(("parallel",)),
        name="swa_proj_cols",
    )(h, g, wqt, wkv)


def _t5_bucket_exact(rel):
    nb = N_BUCKETS // 2
    max_exact = nb // 2
    table = []
    for n in range(int(np.abs(rel).max()) + 1):
        if n < max_exact:
            table.append(n)
        else:
            k = 0
            while 64 * 2 ** (k + 1) <= n * n:
                k += 1
            table.append(min(max_exact + k, nb - 1))
    return (np.asarray(table, np.int32)[np.abs(rel)] + np.where(rel > 0, nb, 0)).astype(np.int32)


def _swa_tile_codes(qpos, band_start):
    nq = qpos.shape[0]
    codes = np.full((nq, SWA_KEYS), -1, np.int32)
    kpos_band = N_META + band_start + np.arange(SWA_BAND)
    rel = kpos_band[None, :] - qpos[:, None]
    ok = np.abs(rel) <= WINDOW
    codes[:, :SWA_BAND] = np.where(ok, _t5_bucket_exact(rel), -1)
    rel_m = np.arange(N_META)[None, :] - qpos[:, None]
    codes[:, SWA_BAND:SWA_BAND + N_META] = _t5_bucket_exact(rel_m)
    codes[:, SINK_COL] = N_BUCKETS
    return codes


def _swa_bias_kernel(code_ref, rb_ref, sink_ref, o_ref, *, scale):
    hd = pl.program_id(1)
    code = code_ref[0]
    t = jnp.full(code.shape, NEG, F32)
    for b in range(N_BUCKETS):
        t = jnp.where(code == b, rb_ref[b, hd] * scale, t)
    t = jnp.where(code == N_BUCKETS, sink_ref[hd] * scale, t)
    o_ref[0, 0] = t


def _swa_bias(codes, rel_bias, sink, transposed):
    nv, a, b = codes.shape
    if transposed:
        out_spec = pl.BlockSpec((1, 1, a, b), lambda v, h: (v, h // SWA_GROUP, 0, h % SWA_GROUP))
        out_shape = (nv, SWA_KV_HEADS, a, SWA_GROUP * b)
    else:
        out_spec = pl.BlockSpec((1, 1, a, b), lambda v, h: (v, h, 0, 0))
        out_shape = (nv, SWA_HEADS, a, b)
    return pl.pallas_call(
        functools.partial(_swa_bias_kernel, scale=LOG2E if transposed else 1.0),
        grid=(nv, SWA_HEADS),
        in_specs=[pl.BlockSpec((1, a, b), lambda v, h: (v, 0, 0)),
                  pl.BlockSpec(memory_space=pltpu.SMEM),
                  pl.BlockSpec(memory_space=pltpu.SMEM)],
        out_specs=out_spec,
        out_shape=jax.ShapeDtypeStruct(out_shape, F32),
        compiler_params=_cparams(("parallel", "parallel")),
        name="swa_bias",
    )(codes, rel_bias, sink)


def _swa_attn_meta_kernel(q_ref, k_ref, v_ref, km_ref, vm_ref, b_ref, o_ref):
    pad = jnp.zeros((SWA_KEYS - SWA_BAND - N_META, SWA_HEAD_DIM), BF16)
    kt = jnp.concatenate([k_ref[0, 0:SWA_BAND, :], km_ref[0], pad], axis=0)
    vt = jnp.concatenate([v_ref[0, 0:SWA_BAND, :], vm_ref[0], pad], axis=0)
    q = q_ref[...].reshape(SWA_GROUP * N_META, SWA_HEAD_DIM)
    s = _dot_t(q, kt) + b_ref[0].reshape(SWA_GROUP * N_META, SWA_KEYS)
    m = jnp.max(s, axis=-1, keepdims=True)
    e = jnp.exp(s - m)
    l = jnp.sum(e, axis=-1, keepdims=True)
    o = (_dot(e.astype(BF16), vt) / l).reshape(SWA_GROUP, N_META, SWA_HEAD_DIM)
    o_ref[...] = jnp.concatenate([o[g] for g in range(SWA_GROUP)], axis=1).astype(BF16)


def _swa_attn(q_meta, k, v, k_meta, v_meta, bias, batch, seq0):
    seq = k.shape[1] // batch
    meta = lambda n: pl.BlockSpec((n, N_META, SWA_HEAD_DIM), lambda kv, s: (kv, seq0 + s, 0))
    real = pl.BlockSpec((1, seq, SWA_HEAD_DIM), lambda kv, s: (kv, s, 0))
    return pl.pallas_call(
        _swa_attn_meta_kernel,
        grid=(SWA_KV_HEADS, batch),
        in_specs=[meta(SWA_GROUP), real, real, meta(1), meta(1),
                  pl.BlockSpec((1, SWA_GROUP, N_META, SWA_KEYS), lambda kv, s: (0, kv, 0, 0))],
        out_specs=pl.BlockSpec((N_META, SWA_GROUP * SWA_HEAD_DIM), lambda kv, s: (s, kv)),
        out_shape=jax.ShapeDtypeStruct((batch * N_META, SWA_HEADS * SWA_HEAD_DIM), BF16),
        compiler_params=_cparams(("parallel", "parallel")),
        name="swa_meta_attn",
    )(q_meta, k, v, k_meta, v_meta, bias)


def _swa_attn_cols_kernel(qt_ref, k_ref, v_ref, km_ref, vm_ref, b_ref, o_ref):
    lanes = SWA_GROUP * BLOCK
    nsub = qt_ref.shape[2] // lanes
    seq = k_ref.shape[1]
    n_blocks = seq // BLOCK
    used = SWA_BAND + 2 * N_META
    kpad = jnp.zeros((used - SWA_BAND - N_META, SWA_HEAD_DIM), BF16)
    vpad = jnp.zeros((SWA_KEYS - SWA_BAND - N_META, SWA_HEAD_DIM), BF16)
    ppad = jnp.zeros((SWA_KEYS - used, lanes), BF16)
    ext = jnp.concatenate([jnp.ones((SWA_KEYS, 16), F32),
                           jnp.zeros((SWA_KEYS, 128 - SWA_HEAD_DIM - 16), F32)], axis=1)

    def window(j):
        b = pl.program_id(2) * nsub + j
        start = pl.multiple_of(jnp.clip((b - 1) * BLOCK, 0, seq - SWA_BAND), BLOCK)
        variant = jnp.where(b == 0, 1, jnp.where(b == n_blocks - 1, 2, 0))
        return start, variant

    def scores(j):
        start, variant = window(j)
        kt = jnp.concatenate([k_ref[0, pl.ds(start, SWA_BAND), :], km_ref[0], kpad], axis=0)
        return (_dot(kt, qt_ref[0, :, j * lanes:(j + 1) * lanes])
                + b_ref[variant, 0, 0:used, :])

    s_next = scores(0)
    for j in range(nsub):
        s = s_next
        if j + 1 < nsub:
            s_next = scores(j + 1)
        start, _ = window(j)
        vrows = jnp.concatenate([v_ref[0, pl.ds(start, SWA_BAND), :], vm_ref[0], vpad], axis=0)
        vt = jnp.concatenate([vrows.astype(F32), ext], axis=1).T.astype(BF16)
        m = jnp.max(s, axis=0, keepdims=True)
        p = jnp.concatenate([jnp.exp2(s - m).astype(BF16), ppad], axis=0)
        acc = _dot(vt, p)
        acc = acc * (1.0 / acc[SWA_HEAD_DIM:SWA_HEAD_DIM + 1])
        o = acc.T[:, :SWA_HEAD_DIM].reshape(SWA_GROUP, BLOCK, SWA_HEAD_DIM)
        o_ref[j * BLOCK:(j + 1) * BLOCK, :] = jnp.concatenate(
            [o[g] for g in range(SWA_GROUP)], axis=1).astype(BF16)


def _swa_attn_cols(qt, k, v, k_meta, v_meta, bias, batch, seq0, tq):
    rows = k.shape[1]
    seq = rows // batch
    nq = seq // tq
    kv_spec = pl.BlockSpec((1, seq, SWA_HEAD_DIM), lambda kv, s, i: (kv, s, 0))
    meta_spec = pl.BlockSpec((1, N_META, SWA_HEAD_DIM), lambda kv, s, i: (kv, seq0 + s, 0))
    return pl.pallas_call(
        _swa_attn_cols_kernel,
        grid=(SWA_KV_HEADS, batch, nq),
        in_specs=[
            pl.BlockSpec((1, SWA_HEAD_DIM, tq * SWA_GROUP), lambda kv, s, i: (kv, 0, s * nq + i)),
            kv_spec, kv_spec, meta_spec, meta_spec,
            pl.BlockSpec((bias.shape[0], 1) + bias.shape[2:], lambda kv, s, i: (0, kv, 0, 0)),
        ],
        out_specs=pl.BlockSpec((tq, SWA_GROUP * SWA_HEAD_DIM), lambda kv, s, i: (s * nq + i, kv)),
        out_shape=jax.ShapeDtypeStruct((rows, SWA_HEADS * SWA_HEAD_DIM), BF16),
        compiler_params=_cparams(("parallel", "parallel", "arbitrary")),
        name="swa_attn_cols",
    )(qt, k, v, k_meta, v_meta, bias)


def _attn_out_kernel(o_ref, w_ref, g_ref, h_ref, out_ref):
    m = _dot(o_ref[...], w_ref[...])
    out_ref[...] = h_ref[...] + _rms(m, g_ref[...])


def _attn_out(o, w, layer, g, h, tm):
    rows = h.shape[0]
    return pl.pallas_call(
        _attn_out_kernel,
        grid=(rows // tm,),
        in_specs=[pl.BlockSpec((tm, o.shape[1]), lambda i: (i, 0)),
                  _layer_spec(w, layer), _full_spec(g),
                  pl.BlockSpec((tm, D_MODEL), lambda i: (i, 0))],
        out_specs=pl.BlockSpec((tm, D_MODEL), lambda i: (i, 0)),
        out_shape=jax.ShapeDtypeStruct((rows, D_MODEL), F32),
        compiler_params=_cparams(("parallel",)),
        name="attn_out",
    )(o, w, g, h)


def _mlp_kernel(h_ref, gpre_ref, win_ref, wout_ref, gpost_ref, out_ref, xn_ref):
    j = pl.program_id(1)
    last = pl.num_programs(1) - 1

    def ffn(xn):
        u = jnp.maximum(_dot(xn, win_ref[...]), 0.0)
        return _dot((u * u).astype(BF16), wout_ref[...])

    @pl.when(j == 0)
    def _():
        xn = _rms(h_ref[...], gpre_ref[...]).astype(BF16)
        xn_ref[...] = xn
        out_ref[...] = ffn(xn)

    @pl.when(jnp.logical_and(j > 0, j < last))
    def _():
        out_ref[...] += ffn(xn_ref[...])

    @pl.when(j == last)
    def _():
        f = out_ref[...] + ffn(xn_ref[...])
        out_ref[...] = h_ref[...] + _rms(f, gpost_ref[...])


def _mlp(h, gpre, w_in, w_out, layer, gpost, tm, tf):
    rows = h.shape[0]
    return pl.pallas_call(
        _mlp_kernel,
        grid=(rows // tm, D_FF // tf),
        in_specs=[pl.BlockSpec((tm, D_MODEL), lambda i, j: (i, 0)),
                  pl.BlockSpec(gpre.shape, lambda i, j: (0, 0)),
                  pl.BlockSpec((None, D_MODEL, tf), lambda i, j: (layer, 0, j)),
                  pl.BlockSpec((None, tf, D_MODEL), lambda i, j: (layer, j, 0)),
                  pl.BlockSpec(gpost.shape, lambda i, j: (0, 0))],
        out_specs=pl.BlockSpec((tm, D_MODEL), lambda i, j: (i, 0)),
        out_shape=jax.ShapeDtypeStruct((rows, D_MODEL), F32),
        scratch_shapes=[pltpu.VMEM((tm, D_MODEL), BF16)],
        compiler_params=_cparams(("parallel", "arbitrary")),
        name="mlp",
    )(h, gpre, w_in, w_out, gpost)


def _rope_dup(w):
    return jnp.concatenate([w, w], axis=-1)


def _prep_mla(w_dq, q_norm, w_uq, w_dkv, kv_norm, w_ukv):
    wd = jnp.concatenate([w_dq, w_dkv[:, :MLA_KV_LORA], _rope_dup(w_dkv[:, MLA_KV_LORA:])], axis=1)
    uq = w_uq.reshape(MLA_Q_LORA, MLA_HEADS, MLA_NOPE + MLA_ROPE)
    wuq = jnp.concatenate([uq[..., :MLA_NOPE], _rope_dup(uq[..., MLA_NOPE:])], axis=-1)
    wuq = wuq.reshape(MLA_Q_LORA, MLA_HEADS * MLA_QK_PAD)
    ukv = w_ukv.reshape(MLA_KV_LORA, MLA_HEADS, MLA_NOPE + MLA_V)
    wuk = ukv[..., :MLA_NOPE].reshape(MLA_KV_LORA, MLA_HEADS * MLA_NOPE)
    wuv = ukv[..., MLA_NOPE:].reshape(MLA_KV_LORA, MLA_HEADS * MLA_V)
    wd, wuq, wuk, wuv = (t.astype(BF16) for t in (wd, wuq, wuk, wuv))
    rows = (wd, q_norm[None], kv_norm[None], wuq, wuk, wuv)
    cols = (wd, q_norm[None], kv_norm[None], wuq.T, wuk, wuv.T)
    return rows, cols


def _rope_tables(pos):
    half = MLA_ROPE // 2
    inv = ROPE_THETA ** (-(jnp.arange(half, dtype=F32) / half))
    ang = pos.astype(F32)[:, None] * inv[None, :]
    cos, sin = jnp.cos(ang), jnp.sin(ang)
    zero = jnp.zeros((pos.shape[0], 128 - MLA_ROPE), F32)
    return (jnp.concatenate([cos, cos, zero], axis=1), jnp.concatenate([-sin, sin, zero], axis=1),
            cos.T, sin.T)


def _trunks(xs, meta_tokens, p):
    seq = xs[0].shape[1]
    batches = [x.shape[0] for x in xs]
    seq0 = [sum(batches[:g]) for g in range(len(xs))]
    total = sum(batches)
    hs = [x.reshape(b * seq, D_MODEL) for x, b in zip(xs, batches)]
    rows_m = total * N_META
    hm = jnp.broadcast_to(meta_tokens[None], (total, N_META, D_MODEL)).reshape(rows_m, D_MODEL)
    tabs_r = _rope_tables(N_META + jnp.arange(seq))
    cos_m, sin_m = (jnp.tile(t, (total, 1)) for t in _rope_tables(jnp.arange(N_META))[:2])
    for i in range(DEPTH):
        j = i // 2
        g_pre = p["norm_mix_pre"][i][None]
        g_post = p["norm_mix_post"][i][None]
        os, oms = [], []
        if i % 2 == 0:
            w_rows, w_cols = p["mla"][j]
            q_m, kc_m, v_m = _mla_proj_rows(hm, g_pre, w_rows, cos_m, sin_m)
            for h, b, s0 in zip(hs, batches, seq0):
                qt, kc, vt = _mla_proj_cols(h, g_pre, w_cols, tabs_r, 512, seq)
                o, o_m = _mla_attn(qt, kc, vt, q_m, kc_m, v_m, b, s0, min(4096, seq), 512)
                os.append(o)
                oms.append(o_m)
            w_o = p["mla_w_o"]
        else:
            q_m, k_m, v_m = _swa_proj(hm, g_pre, p["swa_w_qkv"], j, rows_m)
            for h, b, s0 in zip(hs, batches, seq0):
                qt, k, v = _swa_proj_cols(h, g_pre, p["swa_wqt"], p["swa_wkv"], j, 512)
                os.append(_swa_attn_cols(qt, k, v, k_m, v_m, p["swa_bias"][j], b, s0,
                                         min(2048, seq)))
                oms.append(_swa_attn(q_m, k, v, k_m, v_m, p["swa_bias_meta"][j], b, s0))
            w_o = p["swa_w_o"]
        mlp_args = (p["norm_mlp_pre"][i][None], p["mlp_w_in"], p["mlp_w_out"], i,
                    p["norm_mlp_post"][i][None])
        hs = [_attn_out(o, w_o, j, g_post, h, 512) for o, h in zip(os, hs)]
        hs = [_mlp(h, *mlp_args, min(1024, h.shape[0]), 512) for h in hs]
        hm = _attn_out(jnp.concatenate(oms, axis=0), w_o, j, g_post, hm, rows_m)
        hm = _mlp(hm, *mlp_args, rows_m, 512)
    return tuple(h.reshape(b, seq, D_MODEL) for h, b in zip(hs, batches))


def kernel(x_prompt, x_sample, meta_tokens, rel_bias, mla_w_dq, mla_q_norm, mla_w_uq, mla_w_dkv, mla_kv_norm, mla_w_ukv, mla_w_o, swa_w_qkv, swa_w_o, swa_sink, mlp_w_in, mlp_w_out, norm_mix_pre, norm_mix_post, norm_mlp_pre, norm_mlp_post):
    seq = x_prompt.shape[1]
    assert x_sample.shape[1] == seq and seq % 512 == 0
    n_mla = mla_w_dq.shape[0]
    n_swa = swa_w_qkv.shape[0]
    dq = SWA_HEADS * SWA_HEAD_DIM
    qpos = N_META + np.arange(BLOCK)
    codes = np.stack([
        _swa_tile_codes(qpos + BLOCK, 0),
        _swa_tile_codes(qpos, 0),
        _swa_tile_codes(qpos + 2 * BLOCK, 0),
    ])
    codes_meta = _swa_tile_codes(np.arange(N_META), 0)[None]
    p = {
        "mla": [_prep_mla(mla_w_dq[j], mla_q_norm[j], mla_w_uq[j], mla_w_dkv[j], mla_kv_norm[j],
                          mla_w_ukv[j]) for j in range(n_mla)],
        "mla_w_o": mla_w_o.astype(BF16),
        "swa_w_qkv": swa_w_qkv.astype(BF16),
        "swa_wqt": jnp.swapaxes(swa_w_qkv[:, :, :dq], 1, 2).astype(BF16),
        "swa_wkv": swa_w_qkv[:, :, dq:].astype(BF16),
        "swa_w_o": swa_w_o.astype(BF16),
        "swa_bias": [_swa_bias(jnp.asarray(codes.transpose(0, 2, 1)), rel_bias, swa_sink[j], True)
                     for j in range(n_swa)],
        "swa_bias_meta": [_swa_bias(jnp.asarray(codes_meta), rel_bias, swa_sink[j], False)
                          for j in range(n_swa)],
        "mlp_w_in": mlp_w_in.astype(BF16),
        "mlp_w_out": mlp_w_out.astype(BF16),
        "norm_mix_pre": norm_mix_pre, "norm_mix_post": norm_mix_post,
        "norm_mlp_pre": norm_mlp_pre, "norm_mlp_post": norm_mlp_post,
    }
    return _trunks((x_prompt, x_sample), meta_tokens, p)
```

```python
import functools
import math

import jax
import jax.numpy as jnp
import numpy as np
from jax import lax
from jax.experimental import pallas as pl
from jax.experimental.pallas import tpu as pltpu

D_MODEL = 2048
DEPTH = 4
N_META = 16
MLA_HEADS = 16
MLA_Q_LORA = 512
MLA_KV_LORA = 512
MLA_NOPE = 128
MLA_ROPE = 64
MLA_V = 128
MLA_VX = MLA_V + 16
MLA_QK_PAD = 256
MLA_STREAMS = 8
ROPE_THETA = 10000.0
SWA_HEADS = 32
SWA_KV_HEADS = 4
SWA_GROUP = SWA_HEADS // SWA_KV_HEADS
SWA_HEAD_DIM = 64
WINDOW = 128
BLOCK = 128
N_BUCKETS = 32
MAX_DISTANCE = 128
D_FF = 4 * D_MODEL
MLP_SUB = 512
EPS = 1e-6

SWA_BAND = 3 * BLOCK
SWA_KEYS = 512
SINK_COL = SWA_BAND + N_META
NEG = -1e30

LOG2E = math.log2(math.e)
VMEM_LIMIT_V7X = 61 * 1024 * 1024

BF16 = jnp.bfloat16
F32 = jnp.float32


def _cparams(sem):
    return pltpu.CompilerParams(dimension_semantics=sem, vmem_limit_bytes=VMEM_LIMIT_V7X)


def _rms(x, g):
    r = lax.rsqrt(jnp.mean(x * x, axis=-1, keepdims=True) + EPS)
    return x * r * g


def _dot(a, b):
    return jnp.dot(a, b, preferred_element_type=F32)


def _dot_t(a, b):
    return lax.dot_general(a, b, (((1,), (1,)), ((), ())), preferred_element_type=F32)


def _rope128(y, cos, sin):
    return y * cos + pltpu.roll(y, 32, axis=1) * sin


MLA_SCALE = (MLA_NOPE + MLA_ROPE) ** -0.5 * LOG2E


def _mla_latents(h_ref, g_ref, wd_ref, qn_ref, kvn_ref):
    xn = _rms(h_ref[...], g_ref[...]).astype(BF16)
    a = _dot(xn, wd_ref[...])
    cq = _rms(a[:, :MLA_Q_LORA], qn_ref[...]).astype(BF16)
    ckv = _rms(a[:, MLA_Q_LORA:MLA_Q_LORA + MLA_KV_LORA], kvn_ref[...]).astype(BF16)
    return cq, ckv, a[:, MLA_Q_LORA + MLA_KV_LORA:]


def _store_keys(kc_ref, ckv, wuk_ref, kr):
    for hd in range(MLA_HEADS):
        lo = hd * MLA_QK_PAD
        kc_ref[:, lo:lo + MLA_NOPE] = _dot(
            ckv, wuk_ref[:, hd * MLA_NOPE:(hd + 1) * MLA_NOPE]).astype(BF16)
        kc_ref[:, lo + MLA_NOPE:lo + MLA_QK_PAD] = kr


def _mla_proj_rows_kernel(h_ref, g_ref, wd_ref, qn_ref, kvn_ref, wuq_ref, wuk_ref, wuv_ref,
                          cos_ref, sin_ref, q_ref, kc_ref, v_ref):
    cq, ckv, kr = _mla_latents(h_ref, g_ref, wd_ref, qn_ref, kvn_ref)
    cos = cos_ref[...]
    sin = sin_ref[...]
    _store_keys(kc_ref, ckv, wuk_ref, _rope128(kr, cos, sin).astype(BF16))
    v_ref[...] = _dot(ckv, wuv_ref[...]).astype(BF16)
    for hd in range(MLA_HEADS):
        lo = hd * MLA_QK_PAD
        qh = _dot(cq, wuq_ref[:, lo:lo + MLA_QK_PAD])
        q_ref[:, lo:lo + MLA_NOPE] = (qh[:, :MLA_NOPE] * MLA_SCALE).astype(BF16)
        q_ref[:, lo + MLA_NOPE:lo + MLA_QK_PAD] = (
            _rope128(qh[:, MLA_NOPE:], cos, sin) * MLA_SCALE).astype(BF16)


def _mla_proj_cols_kernel(h_ref, g_ref, wd_ref, qn_ref, kvn_ref, wuqt_ref, wuk_ref, wuvt_ref,
                          cos_ref, sin_ref, cost_ref, sint_ref, qt_ref, kc_ref, vt_ref):
    cq, ckv, kr = _mla_latents(h_ref, g_ref, wd_ref, qn_ref, kvn_ref)
    _store_keys(kc_ref, ckv, wuk_ref, _rope128(kr, cos_ref[...], sin_ref[...]).astype(BF16))
    tm = cq.shape[0]
    c = cost_ref[...]
    s = sint_ref[...]
    half = MLA_ROPE // 2
    ones = jnp.ones((MLA_VX - MLA_V, tm), BF16)
    zeros = jnp.zeros((MLA_QK_PAD - MLA_NOPE - MLA_ROPE, tm), BF16)
    hg = 4
    for h0 in range(0, MLA_HEADS, hg):
        vt = _dot_t(wuvt_ref[h0 * MLA_V:(h0 + hg) * MLA_V, :], ckv).astype(BF16)
        qt = _dot_t(wuqt_ref[h0 * MLA_QK_PAD:(h0 + hg) * MLA_QK_PAD, :], cq)
        for k in range(hg):
            vo = (h0 + k) * MLA_VX
            vt_ref[vo:vo + MLA_V, :] = vt[k * MLA_V:(k + 1) * MLA_V]
            vt_ref[vo + MLA_V:vo + MLA_VX, :] = ones
            lo = k * MLA_QK_PAD
            qo = (h0 + k) * MLA_QK_PAD
            qt_ref[qo:qo + MLA_NOPE, :] = (qt[lo:lo + MLA_NOPE] * MLA_SCALE).astype(BF16)
            x1 = qt[lo + MLA_NOPE:lo + MLA_NOPE + half]
            x2 = qt[lo + MLA_NOPE + half:lo + MLA_NOPE + MLA_ROPE]
            qt_ref[qo + MLA_NOPE:qo + MLA_NOPE + half, :] = (
                (x1 * c - x2 * s) * MLA_SCALE).astype(BF16)
            qt_ref[qo + MLA_NOPE + half:qo + MLA_NOPE + MLA_ROPE, :] = (
                (x2 * c + x1 * s) * MLA_SCALE).astype(BF16)
            qt_ref[qo + MLA_NOPE + MLA_ROPE:qo + MLA_QK_PAD, :] = zeros


def _full_spec(a, buffers=None):
    mode = None if buffers is None else pl.Buffered(buffers)
    return pl.BlockSpec(a.shape, lambda *_: (0,) * a.ndim, pipeline_mode=mode)


def _layer_spec(w, layer):
    return pl.BlockSpec((None,) + w.shape[1:], lambda *_: (layer,) + (0,) * (w.ndim - 1))


def _mla_proj_rows(h, g, w, cos, sin):
    rows = h.shape[0]
    wd, qn, kvn, wuq, wuk, wuv = w
    args = (h, g, wd, qn, kvn, wuq, wuk, wuv, cos, sin)
    widths = (MLA_HEADS * MLA_QK_PAD, MLA_HEADS * MLA_QK_PAD, MLA_HEADS * MLA_V)
    return pl.pallas_call(
        _mla_proj_rows_kernel,
        grid=(1,),
        in_specs=[_full_spec(a) for a in args],
        out_specs=[pl.BlockSpec((rows, n), lambda i: (0, 0)) for n in widths],
        out_shape=[jax.ShapeDtypeStruct((rows, n), BF16) for n in widths],
        compiler_params=_cparams(("arbitrary",)),
        name="mla_proj_rows",
    )(*args)


def _mla_proj_cols(h, g, w, tabs, tm, seq):
    rows = h.shape[0]
    wd, qn, kvn, wuqt, wuk, wuvt = w
    cos, sin, cost, sint = tabs
    nblk = seq // tm
    row = lambda width: pl.BlockSpec((tm, width), lambda i: (i, 0))
    col = lambda height: pl.BlockSpec((height, tm), lambda i: (0, i))
    tab = pl.BlockSpec((tm, 128), lambda i: (i % nblk, 0))
    tabt = pl.BlockSpec((MLA_ROPE // 2, tm), lambda i: (0, i % nblk))
    return pl.pallas_call(
        _mla_proj_cols_kernel,
        grid=(rows // tm,),
        in_specs=[row(D_MODEL)] + [_full_spec(a, 1) for a in (g, wd, qn, kvn, wuqt, wuk, wuvt)]
        + [tab, tab, tabt, tabt],
        out_specs=[col(MLA_HEADS * MLA_QK_PAD), row(MLA_HEADS * MLA_QK_PAD), col(MLA_HEADS * MLA_VX)],
        out_shape=[jax.ShapeDtypeStruct((MLA_HEADS * MLA_QK_PAD, rows), BF16),
                   jax.ShapeDtypeStruct((rows, MLA_HEADS * MLA_QK_PAD), BF16),
                   jax.ShapeDtypeStruct((MLA_HEADS * MLA_VX, rows), BF16)],
        compiler_params=_cparams(("parallel",)),
        name="mla_proj_cols",
    )(h, g, wd, qn, kvn, wuqt, wuk, wuvt, cos, sin, cost, sint)


def _mla_attn_kernel(qt_ref, k_ref, vt_ref, qm_ref, km_ref, vm_ref, ki_ref, vti_ref,
                     o_ref, om_ref, mm_ref, lm_ref, am_ref, *, tk):
    i = pl.program_id(2)
    n_chunks = k_ref.shape[0] // tk
    qm = qm_ref[...]

    @pl.when(i == 0)
    def _():
        s = _dot_t(qm, km_ref[...])
        m = jnp.max(s, axis=-1, keepdims=True)
        p = jnp.exp2(s - m)
        mm_ref[...] = m
        lm_ref[...] = jnp.sum(p, axis=-1, keepdims=True)
        am_ref[...] = _dot(p.astype(BF16), vm_ref[...])

    n = qt_ref.shape[1] // MLA_STREAMS
    qts = [qt_ref[:, h * n:(h + 1) * n] for h in range(MLA_STREAMS)]
    score = lambda h, k: _dot(k, qts[h])
    meta = {"s": _dot_t(qm, ki_ref[...])}

    def meta_softmax():
        m_old = mm_ref[...]
        m = jnp.maximum(m_old, jnp.max(meta["s"], axis=-1, keepdims=True))
        meta["m"] = m
        meta["alpha"] = jnp.exp2(m_old - m)
        meta["p"] = jnp.exp2(meta["s"] - m)

    def meta_values():
        meta["o"] = _dot_t(meta["p"].astype(BF16), vti_ref[0:MLA_V, :])

    def meta_update():
        mm_ref[...] = meta["m"]
        lm_ref[...] = meta["alpha"] * lm_ref[...] + jnp.sum(meta["p"], axis=-1, keepdims=True)
        am_ref[...] = meta["alpha"] * am_ref[...] + meta["o"]

    stages = [(min(1, n_chunks - 1), meta_softmax), (min(2, n_chunks - 1), meta_values),
              (min(4, n_chunks - 1), meta_update)]

    pad = 128 - N_META
    vmt = jnp.concatenate([vm_ref[...].astype(F32), jnp.zeros((pad, MLA_V), F32)], axis=0).T
    vmt = jnp.concatenate([vmt, jnp.ones((MLA_VX - MLA_V, 128), F32)], axis=0).astype(BF16)
    km = km_ref[...]
    st = []
    for h in range(MLA_STREAMS):
        s = score(h, km)
        m = jnp.max(s, axis=0, keepdims=True)
        p = jnp.exp2(s - m)
        pm = jnp.concatenate([p.astype(BF16), jnp.zeros((pad, n), BF16)], axis=0)
        st.append({"m": m, "o_prev": _dot(vmt, pm), "acc": jnp.zeros((MLA_VX, n), F32),
                   "s_next": score(h, k_ref[0:tk, :])})
    for c in range(n_chunks):
        for h in range(MLA_STREAMS):
            t = st[h]
            s = t["s_next"]
            if c + 1 < n_chunks:
                t["s_next"] = score(h, k_ref[(c + 1) * tk:(c + 2) * tk, :])
            m_new = jnp.maximum(t["m"], jnp.max(s, axis=0, keepdims=True))
            alpha = jnp.exp2(t["m"] - m_new)
            p = jnp.exp2(s - m_new)
            o_c = _dot(vt_ref[:, c * tk:(c + 1) * tk], p.astype(BF16))
            t["acc"] = alpha * (t["acc"] + t["o_prev"])
            t["o_prev"] = o_c
            t["m"] = m_new
        for at, stage in stages:
            if at == c:
                stage()
    for h in range(MLA_STREAMS):
        acc = st[h]["acc"] + st[h]["o_prev"]
        o_ref[h * n:(h + 1) * n, :] = (
            acc[:MLA_V] * (1.0 / acc[MLA_V:MLA_V + 1])).T.astype(BF16)

    @pl.when(i == pl.num_programs(2) - 1)
    def _():
        om_ref[...] = (am_ref[...] / lm_ref[...]).astype(BF16)


def _mla_attn(qt, kc, vt, q_meta, kc_meta, v_meta, batch, seq0, tq, tk):
    rows = kc.shape[0]
    seq = rows // batch
    nq = seq // tq
    meta = lambda width: pl.BlockSpec((N_META, width), lambda s, h, i: (seq0 + s, h))
    return pl.pallas_call(
        functools.partial(_mla_attn_kernel, tk=tk),
        grid=(batch, MLA_HEADS, nq),
        in_specs=[
            pl.BlockSpec((MLA_QK_PAD, tq), lambda s, h, i: (h, s * nq + i)),
            pl.BlockSpec((seq, MLA_QK_PAD), lambda s, h, i: (s, h)),
            pl.BlockSpec((MLA_VX, seq), lambda s, h, i: (h, s)),
            meta(MLA_QK_PAD), meta(MLA_QK_PAD), meta(MLA_V),
            pl.BlockSpec((tq, MLA_QK_PAD), lambda s, h, i: (s * nq + i, h)),
            pl.BlockSpec((MLA_VX, tq), lambda s, h, i: (h, s * nq + i)),
        ],
        out_specs=[pl.BlockSpec((tq, MLA_V), lambda s, h, i: (s * nq + i, h)),
                   pl.BlockSpec((N_META, MLA_V), lambda s, h, i: (s, h))],
        out_shape=[jax.ShapeDtypeStruct((rows, MLA_HEADS * MLA_V), BF16),
                   jax.ShapeDtypeStruct((batch * N_META, MLA_HEADS * MLA_V), BF16)],
        scratch_shapes=[pltpu.VMEM((N_META, 1), F32), pltpu.VMEM((N_META, 1), F32),
                        pltpu.VMEM((N_META, MLA_V), F32)],
        compiler_params=_cparams(("parallel", "parallel", "arbitrary")),
        name="mla_attn",
    )(qt, kc, vt, q_meta, kc_meta, v_meta, kc, vt)


def _swa_proj_kernel(h_ref, g_ref, w_ref, q_ref, k_ref, v_ref):
    xn = _rms(h_ref[...], g_ref[...]).astype(BF16)
    dq = SWA_HEADS * SWA_HEAD_DIM
    dk = SWA_KV_HEADS * SWA_HEAD_DIM
    scale = SWA_HEAD_DIM ** -0.5
    for kvh in range(SWA_KV_HEADS):
        lo = kvh * SWA_GROUP * SWA_HEAD_DIM
        qk = _dot(xn, w_ref[:, lo:lo + SWA_GROUP * SWA_HEAD_DIM]) * scale
        for g in range(SWA_GROUP):
            q_ref[kvh * SWA_GROUP + g] = qk[:, g * SWA_HEAD_DIM:(g + 1) * SWA_HEAD_DIM].astype(BF16)
    kv = _dot(xn, w_ref[:, dq:dq + 2 * dk])
    for kvh in range(SWA_KV_HEADS):
        k_ref[kvh] = kv[:, kvh * SWA_HEAD_DIM:(kvh + 1) * SWA_HEAD_DIM].astype(BF16)
        v_ref[kvh] = kv[:, dk + kvh * SWA_HEAD_DIM:dk + (kvh + 1) * SWA_HEAD_DIM].astype(BF16)


def _swa_proj(h, g, w, layer, tm):
    rows = h.shape[0]
    hm = lambda n: pl.BlockSpec((n, tm, SWA_HEAD_DIM), lambda i: (0, i, 0))
    return pl.pallas_call(
        _swa_proj_kernel,
        grid=(rows // tm,),
        in_specs=[pl.BlockSpec((tm, D_MODEL), lambda i: (i, 0)),
                  _full_spec(g), _layer_spec(w, layer)],
        out_specs=[hm(SWA_HEADS), hm(SWA_KV_HEADS), hm(SWA_KV_HEADS)],
        out_shape=[jax.ShapeDtypeStruct((SWA_HEADS, rows, SWA_HEAD_DIM), BF16),
                   jax.ShapeDtypeStruct((SWA_KV_HEADS, rows, SWA_HEAD_DIM), BF16),
                   jax.ShapeDtypeStruct((SWA_KV_HEADS, rows, SWA_HEAD_DIM), BF16)],
        compiler_params=_cparams(("parallel",)),
        name="swa_proj",
    )(h, g, w)


def _swa_proj_cols_kernel(h_ref, g_ref, wqt_ref, wkv_ref, qt_ref, k_ref, v_ref):
    xn = _rms(h_ref[...], g_ref[...]).astype(BF16)
    qt = (_dot_t(wqt_ref[...], xn) * (SWA_HEAD_DIM ** -0.5 * LOG2E)).astype(BF16)
    for kvh in range(SWA_KV_HEADS):
        for g in range(SWA_GROUP):
            r0 = (kvh * SWA_GROUP + g) * SWA_HEAD_DIM
            for blk in range(qt.shape[1] // BLOCK):
                c0 = (blk * SWA_GROUP + g) * BLOCK
                qt_ref[kvh, :, c0:c0 + BLOCK] = qt[r0:r0 + SWA_HEAD_DIM, blk * BLOCK:(blk + 1) * BLOCK]
    dk = SWA_KV_HEADS * SWA_HEAD_DIM
    kv = _dot(xn, wkv_ref[...])
    for kvh in range(SWA_KV_HEADS):
        k_ref[kvh] = kv[:, kvh * SWA_HEAD_DIM:(kvh + 1) * SWA_HEAD_DIM].astype(BF16)
        v_ref[kvh] = kv[:, dk + kvh * SWA_HEAD_DIM:dk + (kvh + 1) * SWA_HEAD_DIM].astype(BF16)


def _swa_proj_cols(h, g, wqt, wkv, layer, tm):
    rows = h.shape[0]
    hm = pl.BlockSpec((SWA_KV_HEADS, tm, SWA_HEAD_DIM), lambda i: (0, i, 0))
    return pl.pallas_call(
        _swa_proj_cols_kernel,
        grid=(rows // tm,),
        in_specs=[pl.BlockSpec((tm, D_MODEL), lambda i: (i, 0)),
                  _full_spec(g), _layer_spec(wqt, layer), _layer_spec(wkv, layer)],
        out_specs=[pl.BlockSpec((SWA_KV_HEADS, SWA_HEAD_DIM, tm * SWA_GROUP), lambda i: (0, 0, i)),
                   hm, hm],
        out_shape=[jax.ShapeDtypeStruct((SWA_KV_HEADS, SWA_HEAD_DIM, rows * SWA_GROUP), BF16),
                   jax.ShapeDtypeStruct((SWA_KV_HEADS, rows, SWA_HEAD_DIM), BF16),
                   jax.ShapeDtypeStruct((SWA_KV_HEADS, rows, SWA_HEAD_DIM), BF16)],
        compiler_params=_cparams(("parallel",)),
        name="swa_proj_cols",
    )(h, g, wqt, wkv)


def _t5_bucket_exact(rel):
    nb = N_BUCKETS // 2
    max_exact = nb // 2
    table = []
    for n in range(int(np.abs(rel).max()) + 1):
        if n < max_exact:
            table.append(n)
        else:
            k = 0
            while 64 * 2 ** (k + 1) <= n * n:
                k += 1
            table.append(min(max_exact + k, nb - 1))
    return (np.asarray(table, np.int32)[np.abs(rel)] + np.where(rel > 0, nb, 0)).astype(np.int32)


def _swa_tile_codes(qpos, band_start):
    nq = qpos.shape[0]
    codes = np.full((nq, SWA_KEYS), -1, np.int32)
    kpos_band = N_META + band_start + np.arange(SWA_BAND)
    rel = kpos_band[None, :] - qpos[:, None]
    ok = np.abs(rel) <= WINDOW
    codes[:, :SWA_BAND] = np.where(ok, _t5_bucket_exact(rel), -1)
    rel_m = np.arange(N_META)[None, :] - qpos[:, None]
    codes[:, SWA_BAND:SWA_BAND + N_META] = _t5_bucket_exact(rel_m)
    codes[:, SINK_COL] = N_BUCKETS
    return codes


def _swa_bias_kernel(code_ref, rb_ref, sink_ref, o_ref, *, scale):
    hd = pl.program_id(1)
    code = code_ref[0]
    t = jnp.full(code.shape, NEG, F32)
    for b in range(N_BUCKETS):
        t = jnp.where(code == b, rb_ref[b, hd] * scale, t)
    t = jnp.where(code == N_BUCKETS, sink_ref[hd] * scale, t)
    o_ref[0, 0] = t


def _swa_bias(codes, rel_bias, sink, transposed):
    nv, a, b = codes.shape
    if transposed:
        out_spec = pl.BlockSpec((1, 1, a, b), lambda v, h: (v, h // SWA_GROUP, 0, h % SWA_GROUP))
        out_shape = (nv, SWA_KV_HEADS, a, SWA_GROUP * b)
    else:
        out_spec = pl.BlockSpec((1, 1, a, b), lambda v, h: (v, h, 0, 0))
        out_shape = (nv, SWA_HEADS, a, b)
    return pl.pallas_call(
        functools.partial(_swa_bias_kernel, scale=LOG2E if transposed else 1.0),
        grid=(nv, SWA_HEADS),
        in_specs=[pl.BlockSpec((1, a, b), lambda v, h: (v, 0, 0)),
                  pl.BlockSpec(memory_space=pltpu.SMEM),
                  pl.BlockSpec(memory_space=pltpu.SMEM)],
        out_specs=out_spec,
        out_shape=jax.ShapeDtypeStruct(out_shape, F32),
        compiler_params=_cparams(("parallel", "parallel")),
        name="swa_bias",
    )(codes, rel_bias, sink)


def _swa_attn_meta_kernel(q_ref, k_ref, v_ref, km_ref, vm_ref, b_ref, o_ref):
    pad = jnp.zeros((SWA_KEYS - SWA_BAND - N_META, SWA_HEAD_DIM), BF16)
    kt = jnp.concatenate([k_ref[0, 0:SWA_BAND, :], km_ref[0], pad], axis=0)
    vt = jnp.concatenate([v_ref[0, 0:SWA_BAND, :], vm_ref[0], pad], axis=0)
    q = q_ref[...].reshape(SWA_GROUP * N_META, SWA_HEAD_DIM)
    s = _dot_t(q, kt) + b_ref[0].reshape(SWA_GROUP * N_META, SWA_KEYS)
    m = jnp.max(s, axis=-1, keepdims=True)
    e = jnp.exp(s - m)
    l = jnp.sum(e, axis=-1, keepdims=True)
    o = (_dot(e.astype(BF16), vt) / l).reshape(SWA_GROUP, N_META, SWA_HEAD_DIM)
    o_ref[...] = jnp.concatenate([o[g] for g in range(SWA_GROUP)], axis=1).astype(BF16)


def _swa_attn(q_meta, k, v, k_meta, v_meta, bias, batch, seq0):
    seq = k.shape[1] // batch
    meta = lambda n: pl.BlockSpec((n, N_META, SWA_HEAD_DIM), lambda kv, s: (kv, seq0 + s, 0))
    real = pl.BlockSpec((1, seq, SWA_HEAD_DIM), lambda kv, s: (kv, s, 0))
    return pl.pallas_call(
        _swa_attn_meta_kernel,
        grid=(SWA_KV_HEADS, batch),
        in_specs=[meta(SWA_GROUP), real, real, meta(1), meta(1),
                  pl.BlockSpec((1, SWA_GROUP, N_META, SWA_KEYS), lambda kv, s: (0, kv, 0, 0))],
        out_specs=pl.BlockSpec((N_META, SWA_GROUP * SWA_HEAD_DIM), lambda kv, s: (s, kv)),
        out_shape=jax.ShapeDtypeStruct((batch * N_META, SWA_HEADS * SWA_HEAD_DIM), BF16),
        compiler_params=_cparams(("parallel", "parallel")),
        name="swa_meta_attn",
    )(q_meta, k, v, k_meta, v_meta, bias)


def _swa_attn_cols_kernel(qt_ref, k_ref, v_ref, km_ref, vm_ref, b_ref, o_ref):
    lanes = SWA_GROUP * BLOCK
    nsub = qt_ref.shape[2] // lanes
    seq = k_ref.shape[1]
    n_blocks = seq // BLOCK
    used = SWA_BAND + 2 * N_META
    kpad = jnp.zeros((used - SWA_BAND - N_META, SWA_HEAD_DIM), BF16)
    vpad = jnp.zeros((SWA_KEYS - SWA_BAND - N_META, SWA_HEAD_DIM), BF16)
    ppad = jnp.zeros((SWA_KEYS - used, lanes), BF16)
    ext = jnp.concatenate([jnp.ones((SWA_KEYS, 16), F32),
                           jnp.zeros((SWA_KEYS, 128 - SWA_HEAD_DIM - 16), F32)], axis=1)

    def window(j):
        b = pl.program_id(2) * nsub + j
        start = pl.multiple_of(jnp.clip((b - 1) * BLOCK, 0, seq - SWA_BAND), BLOCK)
        variant = jnp.where(b == 0, 1, jnp.where(b == n_blocks - 1, 2, 0))
        return start, variant

    def scores(j):
        start, variant = window(j)
        kt = jnp.concatenate([k_ref[0, pl.ds(start, SWA_BAND), :], km_ref[0], kpad], axis=0)
        return (_dot(kt, qt_ref[0, :, j * lanes:(j + 1) * lanes])
                + b_ref[variant, 0, 0:used, :])

    s_next = scores(0)
    for j in range(nsub):
        s = s_next
        if j + 1 < nsub:
            s_next = scores(j + 1)
        start, _ = window(j)
        vrows = jnp.concatenate([v_ref[0, pl.ds(start, SWA_BAND), :], vm_ref[0], vpad], axis=0)
        vt = jnp.concatenate([vrows.astype(F32), ext], axis=1).T.astype(BF16)
        m = jnp.max(s, axis=0, keepdims=True)
        p = jnp.concatenate([jnp.exp2(s - m).astype(BF16), ppad], axis=0)
        acc = _dot(vt, p)
        acc = acc * (1.0 / acc[SWA_HEAD_DIM:SWA_HEAD_DIM + 1])
        o = acc.T[:, :SWA_HEAD_DIM].reshape(SWA_GROUP, BLOCK, SWA_HEAD_DIM)
        o_ref[j * BLOCK:(j + 1) * BLOCK, :] = jnp.concatenate(
            [o[g] for g in range(SWA_GROUP)], axis=1).astype(BF16)


def _swa_attn_cols(qt, k, v, k_meta, v_meta, bias, batch, seq0, tq):
    rows = k.shape[1]
    seq = rows // batch
    nq = seq // tq
    kv_spec = pl.BlockSpec((1, seq, SWA_HEAD_DIM), lambda kv, s, i: (kv, s, 0))
    meta_spec = pl.BlockSpec((1, N_META, SWA_HEAD_DIM), lambda kv, s, i: (kv, seq0 + s, 0))
    return pl.pallas_call(
        _swa_attn_cols_kernel,
        grid=(SWA_KV_HEADS, batch, nq),
        in_specs=[
            pl.BlockSpec((1, SWA_HEAD_DIM, tq * SWA_GROUP), lambda kv, s, i: (kv, 0, s * nq + i)),
            kv_spec, kv_spec, meta_spec, meta_spec,
            pl.BlockSpec((bias.shape[0], 1) + bias.shape[2:], lambda kv, s, i: (0, kv, 0, 0)),
        ],
        out_specs=pl.BlockSpec((tq, SWA_GROUP * SWA_HEAD_DIM), lambda kv, s, i: (s * nq + i, kv)),
        out_shape=jax.ShapeDtypeStruct((rows, SWA_HEADS * SWA_HEAD_DIM), BF16),
        compiler_params=_cparams(("parallel", "parallel", "arbitrary")),
        name="swa_attn_cols",
    )(qt, k, v, k_meta, v_meta, bias)


def _attn_out_kernel(o_ref, w_ref, g_ref, h_ref, out_ref):
    m = _dot(o_ref[...], w_ref[...])
    out_ref[...] = h_ref[...] + _rms(m, g_ref[...])


def _attn_out(o, w, layer, g, h, tm):
    rows = h.shape[0]
    return pl.pallas_call(
        _attn_out_kernel,
        grid=(rows // tm,),
        in_specs=[pl.BlockSpec((tm, o.shape[1]), lambda i: (i, 0)),
                  _layer_spec(w, layer), _full_spec(g),
                  pl.BlockSpec((tm, D_MODEL), lambda i: (i, 0))],
        out_specs=pl.BlockSpec((tm, D_MODEL), lambda i: (i, 0)),
        out_shape=jax.ShapeDtypeStruct((rows, D_MODEL), F32),
        compiler_params=_cparams(("parallel",)),
        name="attn_out",
    )(o, w, g, h)


def _mlp_kernel(h_ref, gpre_ref, win_ref, wout_ref, gpost_ref, out_ref, xn_ref):
    j = pl.program_id(1)
    last = pl.num_programs(1) - 1

    n_sub = win_ref.shape[1] // MLP_SUB

    def ffn(xn, c):
        u = jnp.maximum(_dot(xn, win_ref[:, c * MLP_SUB:(c + 1) * MLP_SUB]), 0.0)
        return _dot((u * u).astype(BF16), wout_ref[c * MLP_SUB:(c + 1) * MLP_SUB, :])

    @pl.when(j == 0)
    def _():
        xn = _rms(h_ref[...], gpre_ref[...]).astype(BF16)
        xn_ref[...] = xn
        out_ref[...] = ffn(xn, 0)
        for c in range(1, n_sub):
            out_ref[...] += ffn(xn, c)

    @pl.when(jnp.logical_and(j > 0, j < last))
    def _():
        for c in range(n_sub):
            out_ref[...] += ffn(xn_ref[...], c)

    @pl.when(j == last)
    def _():
        for c in range(n_sub - 1):
            out_ref[...] += ffn(xn_ref[...], c)
        f = out_ref[...] + ffn(xn_ref[...], n_sub - 1)
        out_ref[...] = h_ref[...] + _rms(f, gpost_ref[...])


def _mlp(h, gpre, w_in, w_out, layer, gpost, tm, tf):
    rows = h.shape[0]
    return pl.pallas_call(
        _mlp_kernel,
        grid=(rows // tm, D_FF // tf),
        in_specs=[pl.BlockSpec((tm, D_MODEL), lambda i, j: (i, 0)),
                  pl.BlockSpec(gpre.shape, lambda i, j: (0, 0)),
                  pl.BlockSpec((None, D_MODEL, tf), lambda i, j: (layer, 0, j)),
                  pl.BlockSpec((None, tf, D_MODEL), lambda i, j: (layer, j, 0)),
                  pl.BlockSpec(gpost.shape, lambda i, j: (0, 0))],
        out_specs=pl.BlockSpec((tm, D_MODEL), lambda i, j: (i, 0)),
        out_shape=jax.ShapeDtypeStruct((rows, D_MODEL), F32),
        scratch_shapes=[pltpu.VMEM((tm, D_MODEL), BF16)],
        compiler_params=_cparams(("parallel", "arbitrary")),
        name="mlp",
    )(h, gpre, w_in, w_out, gpost)


def _rope_dup(w):
    return jnp.concatenate([w, w], axis=-1)


def _prep_mla(w_dq, q_norm, w_uq, w_dkv, kv_norm, w_ukv):
    wd = jnp.concatenate([w_dq, w_dkv[:, :MLA_KV_LORA], _rope_dup(w_dkv[:, MLA_KV_LORA:])], axis=1)
    uq = w_uq.reshape(MLA_Q_LORA, MLA_HEADS, MLA_NOPE + MLA_ROPE)
    wuq = jnp.concatenate([uq[..., :MLA_NOPE], _rope_dup(uq[..., MLA_NOPE:])], axis=-1)
    wuq = wuq.reshape(MLA_Q_LORA, MLA_HEADS * MLA_QK_PAD)
    ukv = w_ukv.reshape(MLA_KV_LORA, MLA_HEADS, MLA_NOPE + MLA_V)
    wuk = ukv[..., :MLA_NOPE].reshape(MLA_KV_LORA, MLA_HEADS * MLA_NOPE)
    wuv = ukv[..., MLA_NOPE:].reshape(MLA_KV_LORA, MLA_HEADS * MLA_V)
    wd, wuq, wuk, wuv = (t.astype(BF16) for t in (wd, wuq, wuk, wuv))
    rows = (wd, q_norm[None], kv_norm[None], wuq, wuk, wuv)
    cols = (wd, q_norm[None], kv_norm[None], wuq.T, wuk, wuv.T)
    return rows, cols


def _rope_tables(pos):
    half = MLA_ROPE // 2
    inv = ROPE_THETA ** (-(jnp.arange(half, dtype=F32) / half))
    ang = pos.astype(F32)[:, None] * inv[None, :]
    cos, sin = jnp.cos(ang), jnp.sin(ang)
    zero = jnp.zeros((pos.shape[0], 128 - MLA_ROPE), F32)
    return (jnp.concatenate([cos, cos, zero], axis=1), jnp.concatenate([-sin, sin, zero], axis=1),
            cos.T, sin.T)


def _trunks(xs, meta_tokens, p):
    seq = xs[0].shape[1]
    batches = [x.shape[0] for x in xs]
    seq0 = [sum(batches[:g]) for g in range(len(xs))]
    total = sum(batches)
    hs = [x.reshape(b * seq, D_MODEL) for x, b in zip(xs, batches)]
    rows_m = total * N_META
    hm = jnp.broadcast_to(meta_tokens[None], (total, N_META, D_MODEL)).reshape(rows_m, D_MODEL)
    tabs_r = _rope_tables(N_META + jnp.arange(seq))
    cos_m, sin_m = (jnp.tile(t, (total, 1)) for t in _rope_tables(jnp.arange(N_META))[:2])
    for i in range(DEPTH):
        j = i // 2
        g_pre = p["norm_mix_pre"][i][None]
        g_post = p["norm_mix_post"][i][None]
        os, oms = [], []
        if i % 2 == 0:
            w_rows, w_cols = p["mla"][j]
            q_m, kc_m, v_m = _mla_proj_rows(hm, g_pre, w_rows, cos_m, sin_m)
            for h, b, s0 in zip(hs, batches, seq0):
                qt, kc, vt = _mla_proj_cols(h, g_pre, w_cols, tabs_r, 512, seq)
                o, o_m = _mla_attn(qt, kc, vt, q_m, kc_m, v_m, b, s0, min(4096, seq), 512)
                os.append(o)
                oms.append(o_m)
            w_o = p["mla_w_o"]
        else:
            q_m, k_m, v_m = _swa_proj(hm, g_pre, p["swa_w_qkv"], j, rows_m)
            for h, b, s0 in zip(hs, batches, seq0):
                qt, k, v = _swa_proj_cols(h, g_pre, p["swa_wqt"], p["swa_wkv"], j, 512)
                os.append(_swa_attn_cols(qt, k, v, k_m, v_m, p["swa_bias"][j], b, s0,
                                         min(2048, seq)))
                oms.append(_swa_attn(q_m, k, v, k_m, v_m, p["swa_bias_meta"][j], b, s0))
            w_o = p["swa_w_o"]
        mlp_args = (p["norm_mlp_pre"][i][None], p["mlp_w_in"], p["mlp_w_out"], i,
                    p["norm_mlp_post"][i][None])
        hs = [_attn_out(o, w_o, j, g_post, h, 512) for o, h in zip(os, hs)]
        hs = [_mlp(h, *mlp_args, min(1024, h.shape[0]), 1024) for h in hs]
        hm = _attn_out(jnp.concatenate(oms, axis=0), w_o, j, g_post, hm, rows_m)
        hm = _mlp(hm, *mlp_args, rows_m, 512)
    return tuple(h.reshape(b, seq, D_MODEL) for h, b in zip(hs, batches))


def kernel(x_prompt, x_sample, meta_tokens, rel_bias, mla_w_dq, mla_q_norm, mla_w_uq, mla_w_dkv, mla_kv_norm, mla_w_ukv, mla_w_o, swa_w_qkv, swa_w_o, swa_sink, mlp_w_in, mlp_w_out, norm_mix_pre, norm_mix_post, norm_mlp_pre, norm_mlp_post):
    seq = x_prompt.shape[1]
    assert x_sample.shape[1] == seq and seq % 512 == 0
    n_mla = mla_w_dq.shape[0]
    n_swa = swa_w_qkv.shape[0]
    dq = SWA_HEADS * SWA_HEAD_DIM
    qpos = N_META + np.arange(BLOCK)
    codes = np.stack([
        _swa_tile_codes(qpos + BLOCK, 0),
        _swa_tile_codes(qpos, 0),
        _swa_tile_codes(qpos + 2 * BLOCK, 0),
    ])
    codes_meta = _swa_tile_codes(np.arange(N_META), 0)[None]
    p = {
        "mla": [_prep_mla(mla_w_dq[j], mla_q_norm[j], mla_w_uq[j], mla_w_dkv[j], mla_kv_norm[j],
                          mla_w_ukv[j]) for j in range(n_mla)],
        "mla_w_o": mla_w_o.astype(BF16),
        "swa_w_qkv": swa_w_qkv.astype(BF16),
        "swa_wqt": jnp.swapaxes(swa_w_qkv[:, :, :dq], 1, 2).astype(BF16),
        "swa_wkv": swa_w_qkv[:, :, dq:].astype(BF16),
        "swa_w_o": swa_w_o.astype(BF16),
        "swa_bias": [_swa_bias(jnp.asarray(codes.transpose(0, 2, 1)), rel_bias, swa_sink[j], True)
                     for j in range(n_swa)],
        "swa_bias_meta": [_swa_bias(jnp.asarray(codes_meta), rel_bias, swa_sink[j], False)
                          for j in range(n_swa)],
        "mlp_w_in": mlp_w_in.astype(BF16),
        "mlp_w_out": mlp_w_out.astype(BF16),
        "norm_mix_pre": norm_mix_pre, "norm_mix_post": norm_mix_post,
        "norm_mlp_pre": norm_mlp_pre, "norm_mlp_post": norm_mlp_post,
    }
    return _trunks((x_prompt, x_sample), meta_tokens, p)
```

```python
import functools
import math

import jax
import jax.numpy as jnp
import numpy as np
from jax import lax
from jax.experimental import pallas as pl
from jax.experimental.pallas import tpu as pltpu

D_MODEL = 2048
DEPTH = 4
N_META = 16
MLA_HEADS = 16
MLA_Q_LORA = 512
MLA_KV_LORA = 512
MLA_NOPE = 128
MLA_ROPE = 64
MLA_V = 128
MLA_VX = MLA_V + 16
MLA_QK_PAD = 256
MLA_STREAMS = 8
ROPE_THETA = 10000.0
SWA_HEADS = 32
SWA_KV_HEADS = 4
SWA_GROUP = SWA_HEADS // SWA_KV_HEADS
SWA_HEAD_DIM = 64
WINDOW = 128
BLOCK = 128
N_BUCKETS = 32
MAX_DISTANCE = 128
D_FF = 4 * D_MODEL
MLP_SUB = 512
EPS = 1e-6

SWA_BAND = 3 * BLOCK
SWA_KEYS = 512
SINK_COL = SWA_BAND + N_META
NEG = -1e30

LOG2E = math.log2(math.e)
VMEM_LIMIT_V7X = 61 * 1024 * 1024

BF16 = jnp.bfloat16
F32 = jnp.float32


def _cparams(sem):
    return pltpu.CompilerParams(dimension_semantics=sem, vmem_limit_bytes=VMEM_LIMIT_V7X)


def _rms(x, g):
    r = lax.rsqrt(jnp.mean(x * x, axis=-1, keepdims=True) + EPS)
    return x * r * g


def _dot(a, b):
    return jnp.dot(a, b, preferred_element_type=F32)


def _dot_t(a, b):
    return lax.dot_general(a, b, (((1,), (1,)), ((), ())), preferred_element_type=F32)


def _rope128(y, cos, sin):
    return y * cos + pltpu.roll(y, 32, axis=1) * sin


MLA_SCALE = (MLA_NOPE + MLA_ROPE) ** -0.5 * LOG2E


def _mla_latents(h_ref, g_ref, wd_ref, qn_ref, kvn_ref):
    xn = _rms(h_ref[...], g_ref[...]).astype(BF16)
    a = _dot(xn, wd_ref[...])
    cq = _rms(a[:, :MLA_Q_LORA], qn_ref[...]).astype(BF16)
    ckv = _rms(a[:, MLA_Q_LORA:MLA_Q_LORA + MLA_KV_LORA], kvn_ref[...]).astype(BF16)
    return cq, ckv, a[:, MLA_Q_LORA + MLA_KV_LORA:]


def _store_keys(kc_ref, ckv, wuk_ref, kr):
    for hd in range(MLA_HEADS):
        lo = hd * MLA_QK_PAD
        kc_ref[:, lo:lo + MLA_NOPE] = _dot(
            ckv, wuk_ref[:, hd * MLA_NOPE:(hd + 1) * MLA_NOPE]).astype(BF16)
        kc_ref[:, lo + MLA_NOPE:lo + MLA_QK_PAD] = kr


def _mla_proj_rows_kernel(h_ref, g_ref, wd_ref, qn_ref, kvn_ref, wuq_ref, wuk_ref, wuv_ref,
                          cos_ref, sin_ref, q_ref, kc_ref, v_ref):
    cq, ckv, kr = _mla_latents(h_ref, g_ref, wd_ref, qn_ref, kvn_ref)
    cos = cos_ref[...]
    sin = sin_ref[...]
    _store_keys(kc_ref, ckv, wuk_ref, _rope128(kr, cos, sin).astype(BF16))
    v_ref[...] = _dot(ckv, wuv_ref[...]).astype(BF16)
    for hd in range(MLA_HEADS):
        lo = hd * MLA_QK_PAD
        qh = _dot(cq, wuq_ref[:, lo:lo + MLA_QK_PAD])
        q_ref[:, lo:lo + MLA_NOPE] = (qh[:, :MLA_NOPE] * MLA_SCALE).astype(BF16)
        q_ref[:, lo + MLA_NOPE:lo + MLA_QK_PAD] = (
            _rope128(qh[:, MLA_NOPE:], cos, sin) * MLA_SCALE).astype(BF16)


def _mla_proj_cols_kernel(h_ref, g_ref, wd_ref, qn_ref, kvn_ref, wuqt_ref, wuk_ref, wuvt_ref,
                          cos_ref, sin_ref, cost_ref, sint_ref, qt_ref, kc_ref, vt_ref):
    cq, ckv, kr = _mla_latents(h_ref, g_ref, wd_ref, qn_ref, kvn_ref)
    _store_keys(kc_ref, ckv, wuk_ref, _rope128(kr, cos_ref[...], sin_ref[...]).astype(BF16))
    tm = cq.shape[0]
    c = cost_ref[...]
    s = sint_ref[...]
    half = MLA_ROPE // 2
    ones = jnp.ones((MLA_VX - MLA_V, tm), BF16)
    zeros = jnp.zeros((MLA_QK_PAD - MLA_NOPE - MLA_ROPE, tm), BF16)
    hg = 4
    for h0 in range(0, MLA_HEADS, hg):
        vt = _dot_t(wuvt_ref[h0 * MLA_V:(h0 + hg) * MLA_V, :], ckv).astype(BF16)
        qt = _dot_t(wuqt_ref[h0 * MLA_QK_PAD:(h0 + hg) * MLA_QK_PAD, :], cq)
        for k in range(hg):
            vo = (h0 + k) * MLA_VX
            vt_ref[vo:vo + MLA_V, :] = vt[k * MLA_V:(k + 1) * MLA_V]
            vt_ref[vo + MLA_V:vo + MLA_VX, :] = ones
            lo = k * MLA_QK_PAD
            qo = (h0 + k) * MLA_QK_PAD
            qt_ref[qo:qo + MLA_NOPE, :] = (qt[lo:lo + MLA_NOPE] * MLA_SCALE).astype(BF16)
            x1 = qt[lo + MLA_NOPE:lo + MLA_NOPE + half]
            x2 = qt[lo + MLA_NOPE + half:lo + MLA_NOPE + MLA_ROPE]
            qt_ref[qo + MLA_NOPE:qo + MLA_NOPE + half, :] = (
                (x1 * c - x2 * s) * MLA_SCALE).astype(BF16)
            qt_ref[qo + MLA_NOPE + half:qo + MLA_NOPE + MLA_ROPE, :] = (
                (x2 * c + x1 * s) * MLA_SCALE).astype(BF16)
            qt_ref[qo + MLA_NOPE + MLA_ROPE:qo + MLA_QK_PAD, :] = zeros


def _full_spec(a, buffers=None):
    mode = None if buffers is None else pl.Buffered(buffers)
    return pl.BlockSpec(a.shape, lambda *_: (0,) * a.ndim, pipeline_mode=mode)


def _layer_spec(w, layer):
    return pl.BlockSpec((None,) + w.shape[1:], lambda *_: (layer,) + (0,) * (w.ndim - 1))


def _mla_proj_rows(h, g, w, cos, sin):
    rows = h.shape[0]
    wd, qn, kvn, wuq, wuk, wuv = w
    args = (h, g, wd, qn, kvn, wuq, wuk, wuv, cos, sin)
    widths = (MLA_HEADS * MLA_QK_PAD, MLA_HEADS * MLA_QK_PAD, MLA_HEADS * MLA_V)
    return pl.pallas_call(
        _mla_proj_rows_kernel,
        grid=(1,),
        in_specs=[_full_spec(a) for a in args],
        out_specs=[pl.BlockSpec((rows, n), lambda i: (0, 0)) for n in widths],
        out_shape=[jax.ShapeDtypeStruct((rows, n), BF16) for n in widths],
        compiler_params=_cparams(("arbitrary",)),
        name="mla_proj_rows",
    )(*args)


def _mla_proj_cols(h, g, w, tabs, tm, seq):
    rows = h.shape[0]
    wd, qn, kvn, wuqt, wuk, wuvt = w
    cos, sin, cost, sint = tabs
    nblk = seq // tm
    row = lambda width: pl.BlockSpec((tm, width), lambda i: (i, 0))
    col = lambda height: pl.BlockSpec((height, tm), lambda i: (0, i))
    tab = pl.BlockSpec((tm, 128), lambda i: (i % nblk, 0))
    tabt = pl.BlockSpec((MLA_ROPE // 2, tm), lambda i: (0, i % nblk))
    return pl.pallas_call(
        _mla_proj_cols_kernel,
        grid=(rows // tm,),
        in_specs=[row(D_MODEL)] + [_full_spec(a, 1) for a in (g, wd, qn, kvn, wuqt, wuk, wuvt)]
        + [tab, tab, tabt, tabt],
        out_specs=[col(MLA_HEADS * MLA_QK_PAD), row(MLA_HEADS * MLA_QK_PAD), col(MLA_HEADS * MLA_VX)],
        out_shape=[jax.ShapeDtypeStruct((MLA_HEADS * MLA_QK_PAD, rows), BF16),
                   jax.ShapeDtypeStruct((rows, MLA_HEADS * MLA_QK_PAD), BF16),
                   jax.ShapeDtypeStruct((MLA_HEADS * MLA_VX, rows), BF16)],
        compiler_params=_cparams(("parallel",)),
        name="mla_proj_cols",
    )(h, g, wd, qn, kvn, wuqt, wuk, wuvt, cos, sin, cost, sint)


def _mla_attn_kernel(qt_ref, k_ref, vt_ref, qm_ref, km_ref, vm_ref, o_ref, om_ref, *, tk):
    n_chunks = k_ref.shape[0] // tk
    qm = qm_ref[...]
    km = km_ref[...]
    n = qt_ref.shape[1] // MLA_STREAMS
    qts = [qt_ref[:, h * n:(h + 1) * n] for h in range(MLA_STREAMS)]
    score = lambda h, k: _dot(k, qts[h])

    meta = {}

    def meta_scores():
        meta["s"] = _dot_t(qm, k_ref[...])
        meta["sm"] = _dot_t(qm, km)

    def meta_softmax():
        m = jnp.maximum(jnp.max(meta["s"], axis=-1, keepdims=True),
                        jnp.max(meta["sm"], axis=-1, keepdims=True))
        meta["p"] = jnp.exp2(meta["s"] - m)
        meta["pm"] = jnp.exp2(meta["sm"] - m)

    def meta_values():
        meta["o"] = (_dot_t(meta["p"].astype(BF16), vt_ref[0:MLA_V, :])
                     + _dot(meta["pm"].astype(BF16), vm_ref[...]))

    def meta_store():
        l = (jnp.sum(meta["p"], axis=-1, keepdims=True)
             + jnp.sum(meta["pm"], axis=-1, keepdims=True))
        om_ref[...] = (meta["o"] / l).astype(BF16)

    stages = [(0, meta_scores), (min(2, n_chunks - 1), meta_softmax),
              (min(3, n_chunks - 1), meta_values), (min(5, n_chunks - 1), meta_store)]

    pad = 128 - N_META
    vmt = jnp.concatenate([vm_ref[...].astype(F32), jnp.zeros((pad, MLA_V), F32)], axis=0).T
    vmt = jnp.concatenate([vmt, jnp.ones((MLA_VX - MLA_V, 128), F32)], axis=0).astype(BF16)
    st = []
    for h in range(MLA_STREAMS):
        s = score(h, km)
        m = jnp.max(s, axis=0, keepdims=True)
        p = jnp.exp2(s - m)
        pm = jnp.concatenate([p.astype(BF16), jnp.zeros((pad, n), BF16)], axis=0)
        st.append({"m": m, "o_prev": _dot(vmt, pm), "acc": jnp.zeros((MLA_VX, n), F32),
                   "s_next": score(h, k_ref[0:tk, :])})
    for c in range(n_chunks):
        for h in range(MLA_STREAMS):
            t = st[h]
            s = t["s_next"]
            if c + 1 < n_chunks:
                t["s_next"] = score(h, k_ref[(c + 1) * tk:(c + 2) * tk, :])
            m_new = jnp.maximum(t["m"], jnp.max(s, axis=0, keepdims=True))
            alpha = jnp.exp2(t["m"] - m_new)
            p = jnp.exp2(s - m_new)
            o_c = _dot(vt_ref[:, c * tk:(c + 1) * tk], p.astype(BF16))
            t["acc"] = alpha * (t["acc"] + t["o_prev"])
            t["o_prev"] = o_c
            t["m"] = m_new
        for at, stage in stages:
            if at == c:
                stage()
    for h in range(MLA_STREAMS):
        acc = st[h]["acc"] + st[h]["o_prev"]
        o_ref[h * n:(h + 1) * n, :] = (
            acc[:MLA_V] * (1.0 / acc[MLA_V:MLA_V + 1])).T.astype(BF16)


def _mla_attn(qt, kc, vt, q_meta, kc_meta, v_meta, batch, seq0, tk):
    rows = kc.shape[0]
    seq = rows // batch
    meta = lambda width: pl.BlockSpec((N_META, width), lambda s, h: (seq0 + s, h))
    return pl.pallas_call(
        functools.partial(_mla_attn_kernel, tk=tk),
        grid=(batch, MLA_HEADS),
        in_specs=[
            pl.BlockSpec((MLA_QK_PAD, seq), lambda s, h: (h, s)),
            pl.BlockSpec((seq, MLA_QK_PAD), lambda s, h: (s, h)),
            pl.BlockSpec((MLA_VX, seq), lambda s, h: (h, s)),
            meta(MLA_QK_PAD), meta(MLA_QK_PAD), meta(MLA_V),
        ],
        out_specs=[pl.BlockSpec((seq, MLA_V), lambda s, h: (s, h)),
                   pl.BlockSpec((N_META, MLA_V), lambda s, h: (s, h))],
        out_shape=[jax.ShapeDtypeStruct((rows, MLA_HEADS * MLA_V), BF16),
                   jax.ShapeDtypeStruct((batch * N_META, MLA_HEADS * MLA_V), BF16)],
        compiler_params=_cparams(("parallel", "parallel")),
        name="mla_attn",
    )(qt, kc, vt, q_meta, kc_meta, v_meta)


def _swa_proj_kernel(h_ref, g_ref, w_ref, q_ref, k_ref, v_ref):
    xn = _rms(h_ref[...], g_ref[...]).astype(BF16)
    dq = SWA_HEADS * SWA_HEAD_DIM
    dk = SWA_KV_HEADS * SWA_HEAD_DIM
    scale = SWA_HEAD_DIM ** -0.5
    for kvh in range(SWA_KV_HEADS):
        lo = kvh * SWA_GROUP * SWA_HEAD_DIM
        qk = _dot(xn, w_ref[:, lo:lo + SWA_GROUP * SWA_HEAD_DIM]) * scale
        for g in range(SWA_GROUP):
            q_ref[kvh * SWA_GROUP + g] = qk[:, g * SWA_HEAD_DIM:(g + 1) * SWA_HEAD_DIM].astype(BF16)
    kv = _dot(xn, w_ref[:, dq:dq + 2 * dk])
    for kvh in range(SWA_KV_HEADS):
        k_ref[kvh] = kv[:, kvh * SWA_HEAD_DIM:(kvh + 1) * SWA_HEAD_DIM].astype(BF16)
        v_ref[kvh] = kv[:, dk + kvh * SWA_HEAD_DIM:dk + (kvh + 1) * SWA_HEAD_DIM].astype(BF16)


def _swa_proj(h, g, w, layer, tm):
    rows = h.shape[0]
    hm = lambda n: pl.BlockSpec((n, tm, SWA_HEAD_DIM), lambda i: (0, i, 0))
    return pl.pallas_call(
        _swa_proj_kernel,
        grid=(rows // tm,),
        in_specs=[pl.BlockSpec((tm, D_MODEL), lambda i: (i, 0)),
                  _full_spec(g), _layer_spec(w, layer)],
        out_specs=[hm(SWA_HEADS), hm(SWA_KV_HEADS), hm(SWA_KV_HEADS)],
        out_shape=[jax.ShapeDtypeStruct((SWA_HEADS, rows, SWA_HEAD_DIM), BF16),
                   jax.ShapeDtypeStruct((SWA_KV_HEADS, rows, SWA_HEAD_DIM), BF16),
                   jax.ShapeDtypeStruct((SWA_KV_HEADS, rows, SWA_HEAD_DIM), BF16)],
        compiler_params=_cparams(("parallel",)),
        name="swa_proj",
    )(h, g, w)


def _swa_proj_cols_kernel(h_ref, g_ref, wqt_ref, wkv_ref, qt_ref, k_ref, v_ref):
    xn = _rms(h_ref[...], g_ref[...]).astype(BF16)
    qt = (_dot_t(wqt_ref[...], xn) * (SWA_HEAD_DIM ** -0.5 * LOG2E)).astype(BF16)
    for kvh in range(SWA_KV_HEADS):
        for g in range(SWA_GROUP):
            r0 = (kvh * SWA_GROUP + g) * SWA_HEAD_DIM
            for blk in range(qt.shape[1] // BLOCK):
                c0 = (blk * SWA_GROUP + g) * BLOCK
                qt_ref[kvh, :, c0:c0 + BLOCK] = qt[r0:r0 + SWA_HEAD_DIM, blk * BLOCK:(blk + 1) * BLOCK]
    dk = SWA_KV_HEADS * SWA_HEAD_DIM
    kv = _dot(xn, wkv_ref[...])
    for kvh in range(SWA_KV_HEADS):
        k_ref[kvh] = kv[:, kvh * SWA_HEAD_DIM:(kvh + 1) * SWA_HEAD_DIM].astype(BF16)
        v_ref[kvh] = kv[:, dk + kvh * SWA_HEAD_DIM:dk + (kvh + 1) * SWA_HEAD_DIM].astype(BF16)


def _swa_proj_cols(h, g, wqt, wkv, layer, tm):
    rows = h.shape[0]
    hm = pl.BlockSpec((SWA_KV_HEADS, tm, SWA_HEAD_DIM), lambda i: (0, i, 0))
    return pl.pallas_call(
        _swa_proj_cols_kernel,
        grid=(rows // tm,),
        in_specs=[pl.BlockSpec((tm, D_MODEL), lambda i: (i, 0)),
                  _full_spec(g), _layer_spec(wqt, layer), _layer_spec(wkv, layer)],
        out_specs=[pl.BlockSpec((SWA_KV_HEADS, SWA_HEAD_DIM, tm * SWA_GROUP), lambda i: (0, 0, i)),
                   hm, hm],
        out_shape=[jax.ShapeDtypeStruct((SWA_KV_HEADS, SWA_HEAD_DIM, rows * SWA_GROUP), BF16),
                   jax.ShapeDtypeStruct((SWA_KV_HEADS, rows, SWA_HEAD_DIM), BF16),
                   jax.ShapeDtypeStruct((SWA_KV_HEADS, rows, SWA_HEAD_DIM), BF16)],
        compiler_params=_cparams(("parallel",)),
        name="swa_proj_cols",
    )(h, g, wqt, wkv)


def _t5_bucket_exact(rel):
    nb = N_BUCKETS // 2
    max_exact = nb // 2
    table = []
    for n in range(int(np.abs(rel).max()) + 1):
        if n < max_exact:
            table.append(n)
        else:
            k = 0
            while 64 * 2 ** (k + 1) <= n * n:
                k += 1
            table.append(min(max_exact + k, nb - 1))
    return (np.asarray(table, np.int32)[np.abs(rel)] + np.where(rel > 0, nb, 0)).astype(np.int32)


def _swa_tile_codes(qpos, band_start):
    nq = qpos.shape[0]
    codes = np.full((nq, SWA_KEYS), -1, np.int32)
    kpos_band = N_META + band_start + np.arange(SWA_BAND)
    rel = kpos_band[None, :] - qpos[:, None]
    ok = np.abs(rel) <= WINDOW
    codes[:, :SWA_BAND] = np.where(ok, _t5_bucket_exact(rel), -1)
    rel_m = np.arange(N_META)[None, :] - qpos[:, None]
    codes[:, SWA_BAND:SWA_BAND + N_META] = _t5_bucket_exact(rel_m)
    codes[:, SINK_COL] = N_BUCKETS
    return codes


def _swa_bias_kernel(code_ref, rb_ref, sink_ref, o_ref, *, scale):
    hd = pl.program_id(1)
    code = code_ref[0]
    t = jnp.full(code.shape, NEG, F32)
    for b in range(N_BUCKETS):
        t = jnp.where(code == b, rb_ref[b, hd] * scale, t)
    t = jnp.where(code == N_BUCKETS, sink_ref[hd] * scale, t)
    o_ref[0, 0] = t


def _swa_bias(codes, rel_bias, sink, transposed):
    nv, a, b = codes.shape
    if transposed:
        out_spec = pl.BlockSpec((1, 1, a, b), lambda v, h: (v, h // SWA_GROUP, 0, h % SWA_GROUP))
        out_shape = (nv, SWA_KV_HEADS, a, SWA_GROUP * b)
    else:
        out_spec = pl.BlockSpec((1, 1, a, b), lambda v, h: (v, h, 0, 0))
        out_shape = (nv, SWA_HEADS, a, b)
    return pl.pallas_call(
        functools.partial(_swa_bias_kernel, scale=LOG2E if transposed else 1.0),
        grid=(nv, SWA_HEADS),
        in_specs=[pl.BlockSpec((1, a, b), lambda v, h: (v, 0, 0)),
                  pl.BlockSpec(memory_space=pltpu.SMEM),
                  pl.BlockSpec(memory_space=pltpu.SMEM)],
        out_specs=out_spec,
        out_shape=jax.ShapeDtypeStruct(out_shape, F32),
        compiler_params=_cparams(("parallel", "parallel")),
        name="swa_bias",
    )(codes, rel_bias, sink)


def _swa_attn_meta_kernel(q_ref, k_ref, v_ref, km_ref, vm_ref, b_ref, o_ref):
    pad = jnp.zeros((SWA_KEYS - SWA_BAND - N_META, SWA_HEAD_DIM), BF16)
    kt = jnp.concatenate([k_ref[0, 0:SWA_BAND, :], km_ref[0], pad], axis=0)
    vt = jnp.concatenate([v_ref[0, 0:SWA_BAND, :], vm_ref[0], pad], axis=0)
    q = q_ref[...].reshape(SWA_GROUP * N_META, SWA_HEAD_DIM)
    s = _dot_t(q, kt) + b_ref[0].reshape(SWA_GROUP * N_META, SWA_KEYS)
    m = jnp.max(s, axis=-1, keepdims=True)
    e = jnp.exp(s - m)
    l = jnp.sum(e, axis=-1, keepdims=True)
    o = (_dot(e.astype(BF16), vt) / l).reshape(SWA_GROUP, N_META, SWA_HEAD_DIM)
    o_ref[...] = jnp.concatenate([o[g] for g in range(SWA_GROUP)], axis=1).astype(BF16)


def _swa_attn(q_meta, k, v, k_meta, v_meta, bias, batch, seq0):
    seq = k.shape[1] // batch
    meta = lambda n: pl.BlockSpec((n, N_META, SWA_HEAD_DIM), lambda kv, s: (kv, seq0 + s, 0))
    real = pl.BlockSpec((1, seq, SWA_HEAD_DIM), lambda kv, s: (kv, s, 0))
    return pl.pallas_call(
        _swa_attn_meta_kernel,
        grid=(SWA_KV_HEADS, batch),
        in_specs=[meta(SWA_GROUP), real, real, meta(1), meta(1),
                  pl.BlockSpec((1, SWA_GROUP, N_META, SWA_KEYS), lambda kv, s: (0, kv, 0, 0))],
        out_specs=pl.BlockSpec((N_META, SWA_GROUP * SWA_HEAD_DIM), lambda kv, s: (s, kv)),
        out_shape=jax.ShapeDtypeStruct((batch * N_META, SWA_HEADS * SWA_HEAD_DIM), BF16),
        compiler_params=_cparams(("parallel", "parallel")),
        name="swa_meta_attn",
    )(q_meta, k, v, k_meta, v_meta, bias)


def _swa_attn_cols_kernel(qt_ref, k_ref, v_ref, km_ref, vm_ref, b_ref, o_ref):
    lanes = SWA_GROUP * BLOCK
    nsub = qt_ref.shape[2] // lanes
    seq = k_ref.shape[1]
    n_blocks = seq // BLOCK
    used = SWA_BAND + 2 * N_META
    kpad = jnp.zeros((used - SWA_BAND - N_META, SWA_HEAD_DIM), BF16)
    vpad = jnp.zeros((SWA_KEYS - SWA_BAND - N_META, SWA_HEAD_DIM), BF16)
    ppad = jnp.zeros((SWA_KEYS - used, lanes), BF16)
    ext = jnp.concatenate([jnp.ones((SWA_KEYS, 16), F32),
                           jnp.zeros((SWA_KEYS, 128 - SWA_HEAD_DIM - 16), F32)], axis=1)

    def window(j):
        b = pl.program_id(2) * nsub + j
        start = pl.multiple_of(jnp.clip((b - 1) * BLOCK, 0, seq - SWA_BAND), BLOCK)
        variant = jnp.where(b == 0, 1, jnp.where(b == n_blocks - 1, 2, 0))
        return start, variant

    def scores(j):
        start, variant = window(j)
        kt = jnp.concatenate([k_ref[0, pl.ds(start, SWA_BAND), :], km_ref[0], kpad], axis=0)
        return (_dot(kt, qt_ref[0, :, j * lanes:(j + 1) * lanes])
                + b_ref[variant, 0, 0:used, :])

    s_next = scores(0)
    for j in range(nsub):
        s = s_next
        if j + 1 < nsub:
            s_next = scores(j + 1)
        start, _ = window(j)
        vrows = jnp.concatenate([v_ref[0, pl.ds(start, SWA_BAND), :], vm_ref[0], vpad], axis=0)
        vt = jnp.concatenate([vrows.astype(F32), ext], axis=1).T.astype(BF16)
        m = jnp.max(s, axis=0, keepdims=True)
        p = jnp.concatenate([jnp.exp2(s - m).astype(BF16), ppad], axis=0)
        acc = _dot(vt, p)
        acc = acc * (1.0 / acc[SWA_HEAD_DIM:SWA_HEAD_DIM + 1])
        o = acc.T[:, :SWA_HEAD_DIM].reshape(SWA_GROUP, BLOCK, SWA_HEAD_DIM)
        o_ref[j * BLOCK:(j + 1) * BLOCK, :] = jnp.concatenate(
            [o[g] for g in range(SWA_GROUP)], axis=1).astype(BF16)


def _swa_attn_cols(qt, k, v, k_meta, v_meta, bias, batch, seq0, tq):
    rows = k.shape[1]
    seq = rows // batch
    nq = seq // tq
    kv_spec = pl.BlockSpec((1, seq, SWA_HEAD_DIM), lambda kv, s, i: (kv, s, 0))
    meta_spec = pl.BlockSpec((1, N_META, SWA_HEAD_DIM), lambda kv, s, i: (kv, seq0 + s, 0))
    return pl.pallas_call(
        _swa_attn_cols_kernel,
        grid=(SWA_KV_HEADS, batch, nq),
        in_specs=[
            pl.BlockSpec((1, SWA_HEAD_DIM, tq * SWA_GROUP), lambda kv, s, i: (kv, 0, s * nq + i)),
            kv_spec, kv_spec, meta_spec, meta_spec,
            pl.BlockSpec((bias.shape[0], 1) + bias.shape[2:], lambda kv, s, i: (0, kv, 0, 0)),
        ],
        out_specs=pl.BlockSpec((tq, SWA_GROUP * SWA_HEAD_DIM), lambda kv, s, i: (s * nq + i, kv)),
        out_shape=jax.ShapeDtypeStruct((rows, SWA_HEADS * SWA_HEAD_DIM), BF16),
        compiler_params=_cparams(("parallel", "parallel", "arbitrary")),
        name="swa_attn_cols",
    )(qt, k, v, k_meta, v_meta, bias)


def _attn_out_kernel(o_ref, w_ref, g_ref, h_ref, out_ref):
    m = _dot(o_ref[...], w_ref[...])
    out_ref[...] = h_ref[...] + _rms(m, g_ref[...])


def _attn_out(o, w, layer, g, h, tm):
    rows = h.shape[0]
    return pl.pallas_call(
        _attn_out_kernel,
        grid=(rows // tm,),
        in_specs=[pl.BlockSpec((tm, o.shape[1]), lambda i: (i, 0)),
                  _layer_spec(w, layer), _full_spec(g),
                  pl.BlockSpec((tm, D_MODEL), lambda i: (i, 0))],
        out_specs=pl.BlockSpec((tm, D_MODEL), lambda i: (i, 0)),
        out_shape=jax.ShapeDtypeStruct((rows, D_MODEL), F32),
        compiler_params=_cparams(("parallel",)),
        name="attn_out",
    )(o, w, g, h)


def _mlp_kernel(h_ref, gpre_ref, win_ref, wout_ref, gpost_ref, out_ref, xn_ref):
    j = pl.program_id(1)
    last = pl.num_programs(1) - 1

    n_sub = win_ref.shape[1] // MLP_SUB

    def ffn(xn, c):
        u = jnp.maximum(_dot(xn, win_ref[:, c * MLP_SUB:(c + 1) * MLP_SUB]), 0.0)
        return _dot((u * u).astype(BF16), wout_ref[c * MLP_SUB:(c + 1) * MLP_SUB, :])

    @pl.when(j == 0)
    def _():
        xn = _rms(h_ref[...], gpre_ref[...]).astype(BF16)
        xn_ref[...] = xn
        out_ref[...] = ffn(xn, 0)
        for c in range(1, n_sub):
            out_ref[...] += ffn(xn, c)

    @pl.when(jnp.logical_and(j > 0, j < last))
    def _():
        for c in range(n_sub):
            out_ref[...] += ffn(xn_ref[...], c)

    @pl.when(j == last)
    def _():
        for c in range(n_sub - 1):
            out_ref[...] += ffn(xn_ref[...], c)
        f = out_ref[...] + ffn(xn_ref[...], n_sub - 1)
        out_ref[...] = h_ref[...] + _rms(f, gpost_ref[...])


def _mlp(h, gpre, w_in, w_out, layer, gpost, tm, tf):
    rows = h.shape[0]
    return pl.pallas_call(
        _mlp_kernel,
        grid=(rows // tm, D_FF // tf),
        in_specs=[pl.BlockSpec((tm, D_MODEL), lambda i, j: (i, 0)),
                  pl.BlockSpec(gpre.shape, lambda i, j: (0, 0)),
                  pl.BlockSpec((None, D_MODEL, tf), lambda i, j: (layer, 0, j)),
                  pl.BlockSpec((None, tf, D_MODEL), lambda i, j: (layer, j, 0)),
                  pl.BlockSpec(gpost.shape, lambda i, j: (0, 0))],
        out_specs=pl.BlockSpec((tm, D_MODEL), lambda i, j: (i, 0)),
        out_shape=jax.ShapeDtypeStruct((rows, D_MODEL), F32),
        scratch_shapes=[pltpu.VMEM((tm, D_MODEL), BF16)],
        compiler_params=_cparams(("parallel", "arbitrary")),
        name="mlp",
    )(h, gpre, w_in, w_out, gpost)


def _rope_dup(w):
    return jnp.concatenate([w, w], axis=-1)


def _prep_mla(w_dq, q_norm, w_uq, w_dkv, kv_norm, w_ukv):
    wd = jnp.concatenate([w_dq, w_dkv[:, :MLA_KV_LORA], _rope_dup(w_dkv[:, MLA_KV_LORA:])], axis=1)
    uq = w_uq.reshape(MLA_Q_LORA, MLA_HEADS, MLA_NOPE + MLA_ROPE)
    wuq = jnp.concatenate([uq[..., :MLA_NOPE], _rope_dup(uq[..., MLA_NOPE:])], axis=-1)
    wuq = wuq.reshape(MLA_Q_LORA, MLA_HEADS * MLA_QK_PAD)
    ukv = w_ukv.reshape(MLA_KV_LORA, MLA_HEADS, MLA_NOPE + MLA_V)
    wuk = ukv[..., :MLA_NOPE].reshape(MLA_KV_LORA, MLA_HEADS * MLA_NOPE)
    wuv = ukv[..., MLA_NOPE:].reshape(MLA_KV_LORA, MLA_HEADS * MLA_V)
    wd, wuq, wuk, wuv = (t.astype(BF16) for t in (wd, wuq, wuk, wuv))
    rows = (wd, q_norm[None], kv_norm[None], wuq, wuk, wuv)
    cols = (wd, q_norm[None], kv_norm[None], wuq.T, wuk, wuv.T)
    return rows, cols


def _rope_tables(pos):
    half = MLA_ROPE // 2
    inv = ROPE_THETA ** (-(jnp.arange(half, dtype=F32) / half))
    ang = pos.astype(F32)[:, None] * inv[None, :]
    cos, sin = jnp.cos(ang), jnp.sin(ang)
    zero = jnp.zeros((pos.shape[0], 128 - MLA_ROPE), F32)
    return (jnp.concatenate([cos, cos, zero], axis=1), jnp.concatenate([-sin, sin, zero], axis=1),
            cos.T, sin.T)


def _tiles(rows, seq):
    return {"proj": 512, "attn_out": 512, "mlp": min(1024, rows), "mlp_ff": 1024,
            "mla_keys": 512, "swa_queries": min(2048, seq)}


def _trunks(xs, meta_tokens, p):
    seq = xs[0].shape[1]
    batches = [x.shape[0] for x in xs]
    seq0 = [sum(batches[:g]) for g in range(len(xs))]
    total = sum(batches)
    hs = [x.reshape(b * seq, D_MODEL) for x, b in zip(xs, batches)]
    rows_m = total * N_META
    hm = jnp.broadcast_to(meta_tokens[None], (total, N_META, D_MODEL)).reshape(rows_m, D_MODEL)
    tabs_r = _rope_tables(N_META + jnp.arange(seq))
    cos_m, sin_m = (jnp.tile(t, (total, 1)) for t in _rope_tables(jnp.arange(N_META))[:2])
    for i in range(DEPTH):
        j = i // 2
        g_pre = p["norm_mix_pre"][i][None]
        g_post = p["norm_mix_post"][i][None]
        os, oms = [], []
        if i % 2 == 0:
            w_rows, w_cols = p["mla"][j]
            q_m, kc_m, v_m = _mla_proj_rows(hm, g_pre, w_rows, cos_m, sin_m)
            for h, b, s0 in zip(hs, batches, seq0):
                t = _tiles(h.shape[0], seq)
                qt, kc, vt = _mla_proj_cols(h, g_pre, w_cols, tabs_r, t["proj"], seq)
                o, o_m = _mla_attn(qt, kc, vt, q_m, kc_m, v_m, b, s0, t["mla_keys"])
                os.append(o)
                oms.append(o_m)
            w_o = p["mla_w_o"]
        else:
            q_m, k_m, v_m = _swa_proj(hm, g_pre, p["swa_w_qkv"], j, rows_m)
            for h, b, s0 in zip(hs, batches, seq0):
                t = _tiles(h.shape[0], seq)
                qt, k, v = _swa_proj_cols(h, g_pre, p["swa_wqt"], p["swa_wkv"], j, t["proj"])
                os.append(_swa_attn_cols(qt, k, v, k_m, v_m, p["swa_bias"][j], b, s0,
                                         t["swa_queries"]))
                oms.append(_swa_attn(q_m, k, v, k_m, v_m, p["swa_bias_meta"][j], b, s0))
            w_o = p["swa_w_o"]
        mlp_args = (p["norm_mlp_pre"][i][None], p["mlp_w_in"], p["mlp_w_out"], i,
                    p["norm_mlp_post"][i][None])
        tiles = [_tiles(h.shape[0], seq) for h in hs]
        hs = [_attn_out(o, w_o, j, g_post, h, t["attn_out"]) for o, h, t in zip(os, hs, tiles)]
        hs = [_mlp(h, *mlp_args, t["mlp"], t["mlp_ff"]) for h, t in zip(hs, tiles)]
        hm = _attn_out(jnp.concatenate(oms, axis=0), w_o, j, g_post, hm, rows_m)
        hm = _mlp(hm, *mlp_args, rows_m, MLP_SUB)
    return tuple(h.reshape(b, seq, D_MODEL) for h, b in zip(hs, batches))


def kernel(x_prompt, x_sample, meta_tokens, rel_bias, mla_w_dq, mla_q_norm, mla_w_uq, mla_w_dkv, mla_kv_norm, mla_w_ukv, mla_w_o, swa_w_qkv, swa_w_o, swa_sink, mlp_w_in, mlp_w_out, norm_mix_pre, norm_mix_post, norm_mlp_pre, norm_mlp_post):
    seq = x_prompt.shape[1]
    assert x_sample.shape[1] == seq and seq % 512 == 0
    n_mla = mla_w_dq.shape[0]
    n_swa = swa_w_qkv.shape[0]
    dq = SWA_HEADS * SWA_HEAD_DIM
    qpos = N_META + np.arange(BLOCK)
    codes = np.stack([
        _swa_tile_codes(qpos + BLOCK, 0),
        _swa_tile_codes(qpos, 0),
        _swa_tile_codes(qpos + 2 * BLOCK, 0),
    ])
    codes_meta = _swa_tile_codes(np.arange(N_META), 0)[None]
    p = {
        "mla": [_prep_mla(mla_w_dq[j], mla_q_norm[j], mla_w_uq[j], mla_w_dkv[j], mla_kv_norm[j],
                          mla_w_ukv[j]) for j in range(n_mla)],
        "mla_w_o": mla_w_o.astype(BF16),
        "swa_w_qkv": swa_w_qkv.astype(BF16),
        "swa_wqt": jnp.swapaxes(swa_w_qkv[:, :, :dq], 1, 2).astype(BF16),
        "swa_wkv": swa_w_qkv[:, :, dq:].astype(BF16),
        "swa_w_o": swa_w_o.astype(BF16),
        "swa_bias": [_swa_bias(jnp.asarray(codes.transpose(0, 2, 1)), rel_bias, swa_sink[j], True)
                     for j in range(n_swa)],
        "swa_bias_meta": [_swa_bias(jnp.asarray(codes_meta), rel_bias, swa_sink[j], False)
                          for j in range(n_swa)],
        "mlp_w_in": mlp_w_in.astype(BF16),
        "mlp_w_out": mlp_w_out.astype(BF16),
        "norm_mix_pre": norm_mix_pre, "norm_mix_post": norm_mix_post,
        "norm_mlp_pre": norm_mlp_pre, "norm_mlp_post": norm_mlp_post,
    }
    return _trunks((x_prompt, x_sample), meta_tokens, p)
```

```python
import functools
import math

import jax
import jax.numpy as jnp
import numpy as np
from jax import lax
from jax.experimental import pallas as pl
from jax.experimental.pallas import tpu as pltpu

D_MODEL = 2048
DEPTH = 4
N_META = 16
MLA_HEADS = 16
MLA_Q_LORA = 512
MLA_KV_LORA = 512
MLA_NOPE = 128
MLA_ROPE = 64
MLA_V = 128
MLA_VX = MLA_V + 16
MLA_QK_PAD = 256
MLA_STREAMS = 8
ROPE_THETA = 10000.0
SWA_HEADS = 32
SWA_KV_HEADS = 4
SWA_GROUP = SWA_HEADS // SWA_KV_HEADS
SWA_HEAD_DIM = 64
WINDOW = 128
BLOCK = 128
N_BUCKETS = 32
MAX_DISTANCE = 128
D_FF = 4 * D_MODEL
MLP_SUB = 512
EPS = 1e-6

SWA_BAND = 3 * BLOCK
SWA_KEYS = 512
SINK_COL = SWA_BAND + N_META
NEG = -1e30

LOG2E = math.log2(math.e)
VMEM_LIMIT_V7X = 61 * 1024 * 1024

BF16 = jnp.bfloat16
F32 = jnp.float32


def _cparams(sem):
    return pltpu.CompilerParams(dimension_semantics=sem, vmem_limit_bytes=VMEM_LIMIT_V7X)


def _rms(x, g):
    r = lax.rsqrt(jnp.mean(x * x, axis=-1, keepdims=True) + EPS)
    return x * r * g


def _dot(a, b):
    return jnp.dot(a, b, preferred_element_type=F32)


def _dot_t(a, b):
    return lax.dot_general(a, b, (((1,), (1,)), ((), ())), preferred_element_type=F32)


def _rope128(y, cos, sin):
    return y * cos + pltpu.roll(y, 32, axis=1) * sin


MLA_SCALE = (MLA_NOPE + MLA_ROPE) ** -0.5 * LOG2E


def _mla_latents(h_ref, g_ref, wd_ref, qn_ref, kvn_ref):
    xn = _rms(h_ref[...], g_ref[...]).astype(BF16)
    a = _dot(xn, wd_ref[...])
    cq = _rms(a[:, :MLA_Q_LORA], qn_ref[...]).astype(BF16)
    ckv = _rms(a[:, MLA_Q_LORA:MLA_Q_LORA + MLA_KV_LORA], kvn_ref[...]).astype(BF16)
    return cq, ckv, a[:, MLA_Q_LORA + MLA_KV_LORA:]


def _store_keys(kc_ref, ckv, wuk_ref, kr):
    for hd in range(MLA_HEADS):
        lo = hd * MLA_QK_PAD
        kc_ref[:, lo:lo + MLA_NOPE] = _dot(
            ckv, wuk_ref[:, hd * MLA_NOPE:(hd + 1) * MLA_NOPE]).astype(BF16)
        kc_ref[:, lo + MLA_NOPE:lo + MLA_QK_PAD] = kr


def _mla_proj_rows_kernel(h_ref, g_ref, wd_ref, qn_ref, kvn_ref, wuq_ref, wuk_ref, wuv_ref,
                          cos_ref, sin_ref, q_ref, kc_ref, v_ref):
    cq, ckv, kr = _mla_latents(h_ref, g_ref, wd_ref, qn_ref, kvn_ref)
    cos = cos_ref[...]
    sin = sin_ref[...]
    _store_keys(kc_ref, ckv, wuk_ref, _rope128(kr, cos, sin).astype(BF16))
    v_ref[...] = _dot(ckv, wuv_ref[...]).astype(BF16)
    for hd in range(MLA_HEADS):
        lo = hd * MLA_QK_PAD
        qh = _dot(cq, wuq_ref[:, lo:lo + MLA_QK_PAD])
        q_ref[:, lo:lo + MLA_NOPE] = (qh[:, :MLA_NOPE] * MLA_SCALE).astype(BF16)
        q_ref[:, lo + MLA_NOPE:lo + MLA_QK_PAD] = (
            _rope128(qh[:, MLA_NOPE:], cos, sin) * MLA_SCALE).astype(BF16)


def _mla_proj_cols_kernel(h_ref, g_ref, wd_ref, qn_ref, kvn_ref, wuqt_ref, wuk_ref, wuvt_ref,
                          cos_ref, sin_ref, cost_ref, sint_ref, qt_ref, kc_ref, vt_ref):
    cq, ckv, kr = _mla_latents(h_ref, g_ref, wd_ref, qn_ref, kvn_ref)
    _store_keys(kc_ref, ckv, wuk_ref, _rope128(kr, cos_ref[...], sin_ref[...]).astype(BF16))
    tm = cq.shape[0]
    c = cost_ref[...]
    s = sint_ref[...]
    half = MLA_ROPE // 2
    ones = jnp.ones((MLA_VX - MLA_V, tm), BF16)
    zeros = jnp.zeros((MLA_QK_PAD - MLA_NOPE - MLA_ROPE, tm), BF16)
    hg = 4
    for h0 in range(0, MLA_HEADS, hg):
        vt = _dot_t(wuvt_ref[h0 * MLA_V:(h0 + hg) * MLA_V, :], ckv).astype(BF16)
        qt = _dot_t(wuqt_ref[h0 * MLA_QK_PAD:(h0 + hg) * MLA_QK_PAD, :], cq)
        for k in range(hg):
            vo = (h0 + k) * MLA_VX
            vt_ref[vo:vo + MLA_V, :] = vt[k * MLA_V:(k + 1) * MLA_V]
            vt_ref[vo + MLA_V:vo + MLA_VX, :] = ones
            lo = k * MLA_QK_PAD
            qo = (h0 + k) * MLA_QK_PAD
            qt_ref[qo:qo + MLA_NOPE, :] = (qt[lo:lo + MLA_NOPE] * MLA_SCALE).astype(BF16)
            x1 = qt[lo + MLA_NOPE:lo + MLA_NOPE + half]
            x2 = qt[lo + MLA_NOPE + half:lo + MLA_NOPE + MLA_ROPE]
            qt_ref[qo + MLA_NOPE:qo + MLA_NOPE + half, :] = (
                (x1 * c - x2 * s) * MLA_SCALE).astype(BF16)
            qt_ref[qo + MLA_NOPE + half:qo + MLA_NOPE + MLA_ROPE, :] = (
                (x2 * c + x1 * s) * MLA_SCALE).astype(BF16)
            qt_ref[qo + MLA_NOPE + MLA_ROPE:qo + MLA_QK_PAD, :] = zeros


def _full_spec(a, buffers=None):
    mode = None if buffers is None else pl.Buffered(buffers)
    return pl.BlockSpec(a.shape, lambda *_: (0,) * a.ndim, pipeline_mode=mode)


def _layer_spec(w, layer):
    return pl.BlockSpec((None,) + w.shape[1:], lambda *_: (layer,) + (0,) * (w.ndim - 1))


def _mla_proj_rows(h, g, w, cos, sin):
    rows = h.shape[0]
    wd, qn, kvn, wuq, wuk, wuv = w
    args = (h, g, wd, qn, kvn, wuq, wuk, wuv, cos, sin)
    widths = (MLA_HEADS * MLA_QK_PAD, MLA_HEADS * MLA_QK_PAD, MLA_HEADS * MLA_V)
    return pl.pallas_call(
        _mla_proj_rows_kernel,
        grid=(1,),
        in_specs=[_full_spec(a) for a in args],
        out_specs=[pl.BlockSpec((rows, n), lambda i: (0, 0)) for n in widths],
        out_shape=[jax.ShapeDtypeStruct((rows, n), BF16) for n in widths],
        compiler_params=_cparams(("arbitrary",)),
        name="mla_proj_rows",
    )(*args)


def _mla_proj_cols(h, g, w, tabs, tm, seq):
    rows = h.shape[0]
    wd, qn, kvn, wuqt, wuk, wuvt = w
    cos, sin, cost, sint = tabs
    nblk = seq // tm
    row = lambda width: pl.BlockSpec((tm, width), lambda i: (i, 0))
    col = lambda height: pl.BlockSpec((height, tm), lambda i: (0, i))
    tab = pl.BlockSpec((tm, 128), lambda i: (i % nblk, 0))
    tabt = pl.BlockSpec((MLA_ROPE // 2, tm), lambda i: (0, i % nblk))
    return pl.pallas_call(
        _mla_proj_cols_kernel,
        grid=(rows // tm,),
        in_specs=[row(D_MODEL)] + [_full_spec(a, 1) for a in (g, wd, qn, kvn, wuqt, wuk, wuvt)]
        + [tab, tab, tabt, tabt],
        out_specs=[col(MLA_HEADS * MLA_QK_PAD), row(MLA_HEADS * MLA_QK_PAD), col(MLA_HEADS * MLA_VX)],
        out_shape=[jax.ShapeDtypeStruct((MLA_HEADS * MLA_QK_PAD, rows), BF16),
                   jax.ShapeDtypeStruct((rows, MLA_HEADS * MLA_QK_PAD), BF16),
                   jax.ShapeDtypeStruct((MLA_HEADS * MLA_VX, rows), BF16)],
        compiler_params=_cparams(("parallel",)),
        name="mla_proj_cols",
    )(h, g, wd, qn, kvn, wuqt, wuk, wuvt, cos, sin, cost, sint)


def _mla_attn_kernel(qt_ref, k_ref, vt_ref, qm_ref, km_ref, vm_ref, ki_ref, vti_ref,
                     o_ref, om_ref, mm_ref, lm_ref, am_ref, *, tk):
    i = pl.program_id(2)
    n_chunks = k_ref.shape[0] // tk
    qm = qm_ref[...]

    @pl.when(i == 0)
    def _():
        s = _dot_t(qm, km_ref[...])
        m = jnp.max(s, axis=-1, keepdims=True)
        p = jnp.exp2(s - m)
        mm_ref[...] = m
        lm_ref[...] = jnp.sum(p, axis=-1, keepdims=True)
        am_ref[...] = _dot(p.astype(BF16), vm_ref[...])

    n = qt_ref.shape[1] // MLA_STREAMS
    qts = [qt_ref[:, h * n:(h + 1) * n] for h in range(MLA_STREAMS)]
    score = lambda h, k: _dot(k, qts[h])
    meta = {"s": _dot_t(qm, ki_ref[...])}

    def meta_softmax():
        m_old = mm_ref[...]
        m = jnp.maximum(m_old, jnp.max(meta["s"], axis=-1, keepdims=True))
        meta["m"] = m
        meta["alpha"] = jnp.exp2(m_old - m)
        meta["p"] = jnp.exp2(meta["s"] - m)

    def meta_values():
        meta["o"] = _dot_t(meta["p"].astype(BF16), vti_ref[0:MLA_V, :])

    def meta_update():
        mm_ref[...] = meta["m"]
        lm_ref[...] = meta["alpha"] * lm_ref[...] + jnp.sum(meta["p"], axis=-1, keepdims=True)
        am_ref[...] = meta["alpha"] * am_ref[...] + meta["o"]

    stages = [(min(1, n_chunks - 1), meta_softmax), (min(2, n_chunks - 1), meta_values),
              (min(4, n_chunks - 1), meta_update)]

    pad = 128 - N_META
    vmt = jnp.concatenate([vm_ref[...].astype(F32), jnp.zeros((pad, MLA_V), F32)], axis=0).T
    vmt = jnp.concatenate([vmt, jnp.ones((MLA_VX - MLA_V, 128), F32)], axis=0).astype(BF16)
    km = km_ref[...]
    st = []
    for h in range(MLA_STREAMS):
        s = score(h, km)
        m = jnp.max(s, axis=0, keepdims=True)
        p = jnp.exp2(s - m)
        pm = jnp.concatenate([p.astype(BF16), jnp.zeros((pad, n), BF16)], axis=0)
        st.append({"m": m, "o_prev": _dot(vmt, pm), "acc": jnp.zeros((MLA_VX, n), F32),
                   "s_next": score(h, k_ref[0:tk, :])})
    for c in range(n_chunks):
        for h in range(MLA_STREAMS):
            t = st[h]
            s = t["s_next"]
            if c + 1 < n_chunks:
                t["s_next"] = score(h, k_ref[(c + 1) * tk:(c + 2) * tk, :])
            m_new = jnp.maximum(t["m"], jnp.max(s, axis=0, keepdims=True))
            alpha = jnp.exp2(t["m"] - m_new)
            p = jnp.exp2(s - m_new)
            o_c = _dot(vt_ref[:, c * tk:(c + 1) * tk], p.astype(BF16))
            t["acc"] = alpha * (t["acc"] + t["o_prev"])
            t["o_prev"] = o_c
            t["m"] = m_new
        for at, stage in stages:
            if at == c:
                stage()
    for h in range(MLA_STREAMS):
        acc = st[h]["acc"] + st[h]["o_prev"]
        o_ref[h * n:(h + 1) * n, :] = (
            acc[:MLA_V] * (1.0 / acc[MLA_V:MLA_V + 1])).T.astype(BF16)

    @pl.when(i == pl.num_programs(2) - 1)
    def _():
        om_ref[...] = (am_ref[...] / lm_ref[...]).astype(BF16)


def _mla_attn(qt, kc, vt, q_meta, kc_meta, v_meta, batch, seq0, tq, tk):
    rows = kc.shape[0]
    seq = rows // batch
    nq = seq // tq
    meta = lambda width: pl.BlockSpec((N_META, width), lambda s, h, i: (seq0 + s, h))
    return pl.pallas_call(
        functools.partial(_mla_attn_kernel, tk=tk),
        grid=(batch, MLA_HEADS, nq),
        in_specs=[
            pl.BlockSpec((MLA_QK_PAD, tq), lambda s, h, i: (h, s * nq + i)),
            pl.BlockSpec((seq, MLA_QK_PAD), lambda s, h, i: (s, h)),
            pl.BlockSpec((MLA_VX, seq), lambda s, h, i: (h, s)),
            meta(MLA_QK_PAD), meta(MLA_QK_PAD), meta(MLA_V),
            pl.BlockSpec((tq, MLA_QK_PAD), lambda s, h, i: (s * nq + i, h)),
            pl.BlockSpec((MLA_VX, tq), lambda s, h, i: (h, s * nq + i)),
        ],
        out_specs=[pl.BlockSpec((tq, MLA_V), lambda s, h, i: (s * nq + i, h)),
                   pl.BlockSpec((N_META, MLA_V), lambda s, h, i: (s, h))],
        out_shape=[jax.ShapeDtypeStruct((rows, MLA_HEADS * MLA_V), BF16),
                   jax.ShapeDtypeStruct((batch * N_META, MLA_HEADS * MLA_V), BF16)],
        scratch_shapes=[pltpu.VMEM((N_META, 1), F32), pltpu.VMEM((N_META, 1), F32),
                        pltpu.VMEM((N_META, MLA_V), F32)],
        compiler_params=_cparams(("parallel", "parallel", "arbitrary")),
        name="mla_attn",
    )(qt, kc, vt, q_meta, kc_meta, v_meta, kc, vt)


def _swa_proj_kernel(h_ref, g_ref, w_ref, q_ref, k_ref, v_ref):
    xn = _rms(h_ref[...], g_ref[...]).astype(BF16)
    dq = SWA_HEADS * SWA_HEAD_DIM
    dk = SWA_KV_HEADS * SWA_HEAD_DIM
    scale = SWA_HEAD_DIM ** -0.5
    for kvh in range(SWA_KV_HEADS):
        lo = kvh * SWA_GROUP * SWA_HEAD_DIM
        qk = _dot(xn, w_ref[:, lo:lo + SWA_GROUP * SWA_HEAD_DIM]) * scale
        for g in range(SWA_GROUP):
            q_ref[kvh * SWA_GROUP + g] = qk[:, g * SWA_HEAD_DIM:(g + 1) * SWA_HEAD_DIM].astype(BF16)
    kv = _dot(xn, w_ref[:, dq:dq + 2 * dk])
    for kvh in range(SWA_KV_HEADS):
        k_ref[kvh] = kv[:, kvh * SWA_HEAD_DIM:(kvh + 1) * SWA_HEAD_DIM].astype(BF16)
        v_ref[kvh] = kv[:, dk + kvh * SWA_HEAD_DIM:dk + (kvh + 1) * SWA_HEAD_DIM].astype(BF16)


def _swa_proj(h, g, w, layer, tm):
    rows = h.shape[0]
    hm = lambda n: pl.BlockSpec((n, tm, SWA_HEAD_DIM), lambda i: (0, i, 0))
    return pl.pallas_call(
        _swa_proj_kernel,
        grid=(rows // tm,),
        in_specs=[pl.BlockSpec((tm, D_MODEL), lambda i: (i, 0)),
                  _full_spec(g), _layer_spec(w, layer)],
        out_specs=[hm(SWA_HEADS), hm(SWA_KV_HEADS), hm(SWA_KV_HEADS)],
        out_shape=[jax.ShapeDtypeStruct((SWA_HEADS, rows, SWA_HEAD_DIM), BF16),
                   jax.ShapeDtypeStruct((SWA_KV_HEADS, rows, SWA_HEAD_DIM), BF16),
                   jax.ShapeDtypeStruct((SWA_KV_HEADS, rows, SWA_HEAD_DIM), BF16)],
        compiler_params=_cparams(("parallel",)),
        name="swa_proj",
    )(h, g, w)


def _swa_proj_cols_kernel(h_ref, g_ref, wqt_ref, wkv_ref, qt_ref, k_ref, v_ref):
    xn = _rms(h_ref[...], g_ref[...]).astype(BF16)
    qt = (_dot_t(wqt_ref[...], xn) * (SWA_HEAD_DIM ** -0.5 * LOG2E)).astype(BF16)
    for kvh in range(SWA_KV_HEADS):
        for g in range(SWA_GROUP):
            r0 = (kvh * SWA_GROUP + g) * SWA_HEAD_DIM
            for blk in range(qt.shape[1] // BLOCK):
                c0 = (blk * SWA_GROUP + g) * BLOCK
                qt_ref[kvh, :, c0:c0 + BLOCK] = qt[r0:r0 + SWA_HEAD_DIM, blk * BLOCK:(blk + 1) * BLOCK]
    dk = SWA_KV_HEADS * SWA_HEAD_DIM
    kv = _dot(xn, wkv_ref[...])
    for kvh in range(SWA_KV_HEADS):
        k_ref[kvh] = kv[:, kvh * SWA_HEAD_DIM:(kvh + 1) * SWA_HEAD_DIM].astype(BF16)
        v_ref[kvh] = kv[:, dk + kvh * SWA_HEAD_DIM:dk + (kvh + 1) * SWA_HEAD_DIM].astype(BF16)


def _swa_proj_cols(h, g, wqt, wkv, layer, tm):
    rows = h.shape[0]
    hm = pl.BlockSpec((SWA_KV_HEADS, tm, SWA_HEAD_DIM), lambda i: (0, i, 0))
    return pl.pallas_call(
        _swa_proj_cols_kernel,
        grid=(rows // tm,),
        in_specs=[pl.BlockSpec((tm, D_MODEL), lambda i: (i, 0)),
                  _full_spec(g), _layer_spec(wqt, layer), _layer_spec(wkv, layer)],
        out_specs=[pl.BlockSpec((SWA_KV_HEADS, SWA_HEAD_DIM, tm * SWA_GROUP), lambda i: (0, 0, i)),
                   hm, hm],
        out_shape=[jax.ShapeDtypeStruct((SWA_KV_HEADS, SWA_HEAD_DIM, rows * SWA_GROUP), BF16),
                   jax.ShapeDtypeStruct((SWA_KV_HEADS, rows, SWA_HEAD_DIM), BF16),
                   jax.ShapeDtypeStruct((SWA_KV_HEADS, rows, SWA_HEAD_DIM), BF16)],
        compiler_params=_cparams(("parallel",)),
        name="swa_proj_cols",
    )(h, g, wqt, wkv)


def _t5_bucket_exact(rel):
    nb = N_BUCKETS // 2
    max_exact = nb // 2
    table = []
    for n in range(int(np.abs(rel).max()) + 1):
        if n < max_exact:
            table.append(n)
        else:
            k = 0
            while 64 * 2 ** (k + 1) <= n * n:
                k += 1
            table.append(min(max_exact + k, nb - 1))
    return (np.asarray(table, np.int32)[np.abs(rel)] + np.where(rel > 0, nb, 0)).astype(np.int32)


def _swa_tile_codes(qpos, band_start):
    nq = qpos.shape[0]
    codes = np.full((nq, SWA_KEYS), -1, np.int32)
    kpos_band = N_META + band_start + np.arange(SWA_BAND)
    rel = kpos_band[None, :] - qpos[:, None]
    ok = np.abs(rel) <= WINDOW
    codes[:, :SWA_BAND] = np.where(ok, _t5_bucket_exact(rel), -1)
    rel_m = np.arange(N_META)[None, :] - qpos[:, None]
    codes[:, SWA_BAND:SWA_BAND + N_META] = _t5_bucket_exact(rel_m)
    codes[:, SINK_COL] = N_BUCKETS
    return codes


def _swa_bias_kernel(code_ref, rb_ref, sink_ref, o_ref, *, scale):
    hd = pl.program_id(1)
    code = code_ref[0]
    t = jnp.full(code.shape, NEG, F32)
    for b in range(N_BUCKETS):
        t = jnp.where(code == b, rb_ref[b, hd] * scale, t)
    t = jnp.where(code == N_BUCKETS, sink_ref[hd] * scale, t)
    o_ref[0, 0] = t


def _swa_bias(codes, rel_bias, sink, transposed):
    nv, a, b = codes.shape
    if transposed:
        out_spec = pl.BlockSpec((1, 1, a, b), lambda v, h: (v, h // SWA_GROUP, 0, h % SWA_GROUP))
        out_shape = (nv, SWA_KV_HEADS, a, SWA_GROUP * b)
    else:
        out_spec = pl.BlockSpec((1, 1, a, b), lambda v, h: (v, h, 0, 0))
        out_shape = (nv, SWA_HEADS, a, b)
    return pl.pallas_call(
        functools.partial(_swa_bias_kernel, scale=LOG2E if transposed else 1.0),
        grid=(nv, SWA_HEADS),
        in_specs=[pl.BlockSpec((1, a, b), lambda v, h: (v, 0, 0)),
                  pl.BlockSpec(memory_space=pltpu.SMEM),
                  pl.BlockSpec(memory_space=pltpu.SMEM)],
        out_specs=out_spec,
        out_shape=jax.ShapeDtypeStruct(out_shape, F32),
        compiler_params=_cparams(("parallel", "parallel")),
        name="swa_bias",
    )(codes, rel_bias, sink)


def _swa_attn_meta_kernel(q_ref, k_ref, v_ref, km_ref, vm_ref, b_ref, o_ref):
    pad = jnp.zeros((SWA_KEYS - SWA_BAND - N_META, SWA_HEAD_DIM), BF16)
    kt = jnp.concatenate([k_ref[0, 0:SWA_BAND, :], km_ref[0], pad], axis=0)
    vt = jnp.concatenate([v_ref[0, 0:SWA_BAND, :], vm_ref[0], pad], axis=0)
    q = q_ref[...].reshape(SWA_GROUP * N_META, SWA_HEAD_DIM)
    s = _dot_t(q, kt) + b_ref[0].reshape(SWA_GROUP * N_META, SWA_KEYS)
    m = jnp.max(s, axis=-1, keepdims=True)
    e = jnp.exp(s - m)
    l = jnp.sum(e, axis=-1, keepdims=True)
    o = (_dot(e.astype(BF16), vt) / l).reshape(SWA_GROUP, N_META, SWA_HEAD_DIM)
    o_ref[...] = jnp.concatenate([o[g] for g in range(SWA_GROUP)], axis=1).astype(BF16)


def _swa_attn(q_meta, k, v, k_meta, v_meta, bias, batch, seq0):
    seq = k.shape[1] // batch
    meta = lambda n: pl.BlockSpec((n, N_META, SWA_HEAD_DIM), lambda kv, s: (kv, seq0 + s, 0))
    real = pl.BlockSpec((1, seq, SWA_HEAD_DIM), lambda kv, s: (kv, s, 0))
    return pl.pallas_call(
        _swa_attn_meta_kernel,
        grid=(SWA_KV_HEADS, batch),
        in_specs=[meta(SWA_GROUP), real, real, meta(1), meta(1),
                  pl.BlockSpec((1, SWA_GROUP, N_META, SWA_KEYS), lambda kv, s: (0, kv, 0, 0))],
        out_specs=pl.BlockSpec((N_META, SWA_GROUP * SWA_HEAD_DIM), lambda kv, s: (s, kv)),
        out_shape=jax.ShapeDtypeStruct((batch * N_META, SWA_HEADS * SWA_HEAD_DIM), BF16),
        compiler_params=_cparams(("parallel", "parallel")),
        name="swa_meta_attn",
    )(q_meta, k, v, k_meta, v_meta, bias)


def _swa_attn_cols_kernel(qt_ref, k_ref, v_ref, km_ref, vm_ref, b_ref, o_ref):
    lanes = SWA_GROUP * BLOCK
    nsub = qt_ref.shape[2] // lanes
    seq = k_ref.shape[1]
    n_blocks = seq // BLOCK
    used = SWA_BAND + 2 * N_META
    kpad = jnp.zeros((used - SWA_BAND - N_META, SWA_HEAD_DIM), BF16)
    vpad = jnp.zeros((SWA_KEYS - SWA_BAND - N_META, SWA_HEAD_DIM), BF16)
    ppad = jnp.zeros((SWA_KEYS - used, lanes), BF16)
    ext = jnp.concatenate([jnp.ones((SWA_KEYS, 16), F32),
                           jnp.zeros((SWA_KEYS, 128 - SWA_HEAD_DIM - 16), F32)], axis=1)

    def window(j):
        b = pl.program_id(2) * nsub + j
        start = pl.multiple_of(jnp.clip((b - 1) * BLOCK, 0, seq - SWA_BAND), BLOCK)
        variant = jnp.where(b == 0, 1, jnp.where(b == n_blocks - 1, 2, 0))
        return start, variant

    def scores(j):
        start, variant = window(j)
        kt = jnp.concatenate([k_ref[0, pl.ds(start, SWA_BAND), :], km_ref[0], kpad], axis=0)
        return (_dot(kt, qt_ref[0, :, j * lanes:(j + 1) * lanes])
                + b_ref[variant, 0, 0:used, :])

    s_next = scores(0)
    for j in range(nsub):
        s = s_next
        if j + 1 < nsub:
            s_next = scores(j + 1)
        start, _ = window(j)
        vrows = jnp.concatenate([v_ref[0, pl.ds(start, SWA_BAND), :], vm_ref[0], vpad], axis=0)
        vt = jnp.concatenate([vrows.astype(F32), ext], axis=1).T.astype(BF16)
        m = jnp.max(s, axis=0, keepdims=True)
        p = jnp.concatenate([jnp.exp2(s - m).astype(BF16), ppad], axis=0)
        acc = _dot(vt, p)
        acc = acc * (1.0 / acc[SWA_HEAD_DIM:SWA_HEAD_DIM + 1])
        o = acc.T[:, :SWA_HEAD_DIM].reshape(SWA_GROUP, BLOCK, SWA_HEAD_DIM)
        o_ref[j * BLOCK:(j + 1) * BLOCK, :] = jnp.concatenate(
            [o[g] for g in range(SWA_GROUP)], axis=1).astype(BF16)


def _swa_attn_cols(qt, k, v, k_meta, v_meta, bias, batch, seq0, tq):
    rows = k.shape[1]
    seq = rows // batch
    nq = seq // tq
    kv_spec = pl.BlockSpec((1, seq, SWA_HEAD_DIM), lambda kv, s, i: (kv, s, 0))
    meta_spec = pl.BlockSpec((1, N_META, SWA_HEAD_DIM), lambda kv, s, i: (kv, seq0 + s, 0))
    return pl.pallas_call(
        _swa_attn_cols_kernel,
        grid=(SWA_KV_HEADS, batch, nq),
        in_specs=[
            pl.BlockSpec((1, SWA_HEAD_DIM, tq * SWA_GROUP), lambda kv, s, i: (kv, 0, s * nq + i)),
            kv_spec, kv_spec, meta_spec, meta_spec,
            pl.BlockSpec((bias.shape[0], 1) + bias.shape[2:], lambda kv, s, i: (0, kv, 0, 0)),
        ],
        out_specs=pl.BlockSpec((tq, SWA_GROUP * SWA_HEAD_DIM), lambda kv, s, i: (s * nq + i, kv)),
        out_shape=jax.ShapeDtypeStruct((rows, SWA_HEADS * SWA_HEAD_DIM), BF16),
        compiler_params=_cparams(("parallel", "parallel", "arbitrary")),
        name="swa_attn_cols",
    )(qt, k, v, k_meta, v_meta, bias)


def _attn_out_kernel(o_ref, w_ref, g_ref, h_ref, out_ref):
    m = _dot(o_ref[...], w_ref[...])
    out_ref[...] = h_ref[...] + _rms(m, g_ref[...])


def _attn_out(o, w, layer, g, h, tm):
    rows = h.shape[0]
    return pl.pallas_call(
        _attn_out_kernel,
        grid=(rows // tm,),
        in_specs=[pl.BlockSpec((tm, o.shape[1]), lambda i: (i, 0)),
                  _layer_spec(w, layer), _full_spec(g),
                  pl.BlockSpec((tm, D_MODEL), lambda i: (i, 0))],
        out_specs=pl.BlockSpec((tm, D_MODEL), lambda i: (i, 0)),
        out_shape=jax.ShapeDtypeStruct((rows, D_MODEL), F32),
        compiler_params=_cparams(("parallel",)),
        name="attn_out",
    )(o, w, g, h)


def _mlp_kernel(h_ref, gpre_ref, win_ref, wout_ref, gpost_ref, out_ref, xn_ref):
    j = pl.program_id(1)
    last = pl.num_programs(1) - 1

    n_sub = win_ref.shape[1] // MLP_SUB

    def ffn(xn, c):
        u = jnp.maximum(_dot(xn, win_ref[:, c * MLP_SUB:(c + 1) * MLP_SUB]), 0.0)
        return _dot((u * u).astype(BF16), wout_ref[c * MLP_SUB:(c + 1) * MLP_SUB, :])

    @pl.when(j == 0)
    def _():
        xn = _rms(h_ref[...], gpre_ref[...]).astype(BF16)
        xn_ref[...] = xn
        out_ref[...] = ffn(xn, 0)
        for c in range(1, n_sub):
            out_ref[...] += ffn(xn, c)

    @pl.when(jnp.logical_and(j > 0, j < last))
    def _():
        for c in range(n_sub):
            out_ref[...] += ffn(xn_ref[...], c)

    @pl.when(j == last)
    def _():
        for c in range(n_sub - 1):
            out_ref[...] += ffn(xn_ref[...], c)
        f = out_ref[...] + ffn(xn_ref[...], n_sub - 1)
        out_ref[...] = h_ref[...] + _rms(f, gpost_ref[...])


def _mlp(h, gpre, w_in, w_out, layer, gpost, tm, tf):
    rows = h.shape[0]
    return pl.pallas_call(
        _mlp_kernel,
        grid=(rows // tm, D_FF // tf),
        in_specs=[pl.BlockSpec((tm, D_MODEL), lambda i, j: (i, 0)),
                  pl.BlockSpec(gpre.shape, lambda i, j: (0, 0)),
                  pl.BlockSpec((None, D_MODEL, tf), lambda i, j: (layer, 0, j)),
                  pl.BlockSpec((None, tf, D_MODEL), lambda i, j: (layer, j, 0)),
                  pl.BlockSpec(gpost.shape, lambda i, j: (0, 0))],
        out_specs=pl.BlockSpec((tm, D_MODEL), lambda i, j: (i, 0)),
        out_shape=jax.ShapeDtypeStruct((rows, D_MODEL), F32),
        scratch_shapes=[pltpu.VMEM((tm, D_MODEL), BF16)],
        compiler_params=_cparams(("parallel", "arbitrary")),
        name="mlp",
    )(h, gpre, w_in, w_out, gpost)


def _rope_dup(w):
    return jnp.concatenate([w, w], axis=-1)


def _prep_mla(w_dq, q_norm, w_uq, w_dkv, kv_norm, w_ukv):
    wd = jnp.concatenate([w_dq, w_dkv[:, :MLA_KV_LORA], _rope_dup(w_dkv[:, MLA_KV_LORA:])], axis=1)
    uq = w_uq.reshape(MLA_Q_LORA, MLA_HEADS, MLA_NOPE + MLA_ROPE)
    wuq = jnp.concatenate([uq[..., :MLA_NOPE], _rope_dup(uq[..., MLA_NOPE:])], axis=-1)
    wuq = wuq.reshape(MLA_Q_LORA, MLA_HEADS * MLA_QK_PAD)
    ukv = w_ukv.reshape(MLA_KV_LORA, MLA_HEADS, MLA_NOPE + MLA_V)
    wuk = ukv[..., :MLA_NOPE].reshape(MLA_KV_LORA, MLA_HEADS * MLA_NOPE)
    wuv = ukv[..., MLA_NOPE:].reshape(MLA_KV_LORA, MLA_HEADS * MLA_V)
    wd, wuq, wuk, wuv = (t.astype(BF16) for t in (wd, wuq, wuk, wuv))
    rows = (wd, q_norm[None], kv_norm[None], wuq, wuk, wuv)
    cols = (wd, q_norm[None], kv_norm[None], wuq.T, wuk, wuv.T)
    return rows, cols


def _rope_tables(pos):
    half = MLA_ROPE // 2
    inv = ROPE_THETA ** (-(jnp.arange(half, dtype=F32) / half))
    ang = pos.astype(F32)[:, None] * inv[None, :]
    cos, sin = jnp.cos(ang), jnp.sin(ang)
    zero = jnp.zeros((pos.shape[0], 128 - MLA_ROPE), F32)
    return (jnp.concatenate([cos, cos, zero], axis=1), jnp.concatenate([-sin, sin, zero], axis=1),
            cos.T, sin.T)


def _tiles(rows, seq):
    return {"proj": 512, "attn_out": 512, "mlp": min(1024, rows), "mlp_ff": 1024,
            "mla_queries": min(4096, seq), "mla_keys": 512, "swa_queries": min(2048, seq)}


def _trunks(xs, meta_tokens, p):
    seq = xs[0].shape[1]
    batches = [x.shape[0] for x in xs]
    seq0 = [sum(batches[:g]) for g in range(len(xs))]
    total = sum(batches)
    hs = [x.reshape(b * seq, D_MODEL) for x, b in zip(xs, batches)]
    rows_m = total * N_META
    hm = jnp.broadcast_to(meta_tokens[None], (total, N_META, D_MODEL)).reshape(rows_m, D_MODEL)
    tabs_r = _rope_tables(N_META + jnp.arange(seq))
    cos_m, sin_m = (jnp.tile(t, (total, 1)) for t in _rope_tables(jnp.arange(N_META))[:2])
    for i in range(DEPTH):
        j = i // 2
        g_pre = p["norm_mix_pre"][i][None]
        g_post = p["norm_mix_post"][i][None]
        os, oms = [], []
        if i % 2 == 0:
            w_rows, w_cols = p["mla"][j]
            q_m, kc_m, v_m = _mla_proj_rows(hm, g_pre, w_rows, cos_m, sin_m)
            for h, b, s0 in zip(hs, batches, seq0):
                t = _tiles(h.shape[0], seq)
                qt, kc, vt = _mla_proj_cols(h, g_pre, w_cols, tabs_r, t["proj"], seq)
                o, o_m = _mla_attn(qt, kc, vt, q_m, kc_m, v_m, b, s0, t["mla_queries"],
                                   t["mla_keys"])
                os.append(o)
                oms.append(o_m)
            w_o = p["mla_w_o"]
        else:
            q_m, k_m, v_m = _swa_proj(hm, g_pre, p["swa_w_qkv"], j, rows_m)
            for h, b, s0 in zip(hs, batches, seq0):
                t = _tiles(h.shape[0], seq)
                qt, k, v = _swa_proj_cols(h, g_pre, p["swa_wqt"], p["swa_wkv"], j, t["proj"])
                os.append(_swa_attn_cols(qt, k, v, k_m, v_m, p["swa_bias"][j], b, s0,
                                         t["swa_queries"]))
                oms.append(_swa_attn(q_m, k, v, k_m, v_m, p["swa_bias_meta"][j], b, s0))
            w_o = p["swa_w_o"]
        mlp_args = (p["norm_mlp_pre"][i][None], p["mlp_w_in"], p["mlp_w_out"], i,
                    p["norm_mlp_post"][i][None])
        tiles = [_tiles(h.shape[0], seq) for h in hs]
        hs = [_attn_out(o, w_o, j, g_post, h, t["attn_out"]) for o, h, t in zip(os, hs, tiles)]
        hs = [_mlp(h, *mlp_args, t["mlp"], t["mlp_ff"]) for h, t in zip(hs, tiles)]
        hm = _attn_out(jnp.concatenate(oms, axis=0), w_o, j, g_post, hm, rows_m)
        hm = _mlp(hm, *mlp_args, rows_m, MLP_SUB)
    return tuple(h.reshape(b, seq, D_MODEL) for h, b in zip(hs, batches))


def kernel(x_prompt, x_sample, meta_tokens, rel_bias, mla_w_dq, mla_q_norm, mla_w_uq, mla_w_dkv, mla_kv_norm, mla_w_ukv, mla_w_o, swa_w_qkv, swa_w_o, swa_sink, mlp_w_in, mlp_w_out, norm_mix_pre, norm_mix_post, norm_mlp_pre, norm_mlp_post):
    seq = x_prompt.shape[1]
    assert x_sample.shape[1] == seq and seq % 512 == 0
    n_mla = mla_w_dq.shape[0]
    n_swa = swa_w_qkv.shape[0]
    dq = SWA_HEADS * SWA_HEAD_DIM
    qpos = N_META + np.arange(BLOCK)
    codes = np.stack([
        _swa_tile_codes(qpos + BLOCK, 0),
        _swa_tile_codes(qpos, 0),
        _swa_tile_codes(qpos + 2 * BLOCK, 0),
    ])
    codes_meta = _swa_tile_codes(np.arange(N_META), 0)[None]
    p = {
        "mla": [_prep_mla(mla_w_dq[j], mla_q_norm[j], mla_w_uq[j], mla_w_dkv[j], mla_kv_norm[j],
                          mla_w_ukv[j]) for j in range(n_mla)],
        "mla_w_o": mla_w_o.astype(BF16),
        "swa_w_qkv": swa_w_qkv.astype(BF16),
        "swa_wqt": jnp.swapaxes(swa_w_qkv[:, :, :dq], 1, 2).astype(BF16),
        "swa_wkv": swa_w_qkv[:, :, dq:].astype(BF16),
        "swa_w_o": swa_w_o.astype(BF16),
        "swa_bias": [_swa_bias(jnp.asarray(codes.transpose(0, 2, 1)), rel_bias, swa_sink[j], True)
                     for j in range(n_swa)],
        "swa_bias_meta": [_swa_bias(jnp.asarray(codes_meta), rel_bias, swa_sink[j], False)
                          for j in range(n_swa)],
        "mlp_w_in": mlp_w_in.astype(BF16),
        "mlp_w_out": mlp_w_out.astype(BF16),
        "norm_mix_pre": norm_mix_pre, "norm_mix_post": norm_mix_post,
        "norm_mlp_pre": norm_mlp_pre, "norm_mlp_post": norm_mlp_post,
    }
    return _trunks((x_prompt, x_sample), meta_tokens, p)
```

```python
import functools
import math

import jax
import jax.numpy as jnp
import numpy as np
from jax import lax
from jax.experimental import pallas as pl
from jax.experimental.pallas import tpu as pltpu

D_MODEL = 2048
DEPTH = 4
N_META = 16
MLA_HEADS = 16
MLA_Q_LORA = 512
MLA_KV_LORA = 512
MLA_NOPE = 128
MLA_ROPE = 64
MLA_V = 128
MLA_VX = MLA_V + 16
MLA_QK_PAD = 256
MLA_STREAMS = 8
ROPE_THETA = 10000.0
SWA_HEADS = 32
SWA_KV_HEADS = 4
SWA_GROUP = SWA_HEADS // SWA_KV_HEADS
SWA_HEAD_DIM = 64
WINDOW = 128
BLOCK = 128
N_BUCKETS = 32
MAX_DISTANCE = 128
D_FF = 4 * D_MODEL
MLP_SUB = 512
EPS = 1e-6

SWA_BAND = 3 * BLOCK
SWA_KEYS = 512
SINK_COL = SWA_BAND + N_META
NEG = -1e30

LOG2E = math.log2(math.e)
VMEM_LIMIT_V7X = 61 * 1024 * 1024

BF16 = jnp.bfloat16
F32 = jnp.float32


def _cparams(sem):
    return pltpu.CompilerParams(dimension_semantics=sem, vmem_limit_bytes=VMEM_LIMIT_V7X)


def _rms(x, g):
    r = lax.rsqrt(jnp.mean(x * x, axis=-1, keepdims=True) + EPS)
    return x * r * g


def _dot(a, b):
    return jnp.dot(a, b, preferred_element_type=F32)


def _dot_t(a, b):
    return lax.dot_general(a, b, (((1,), (1,)), ((), ())), preferred_element_type=F32)


def _rope128(y, cos, sin):
    return y * cos + pltpu.roll(y, 32, axis=1) * sin


MLA_SCALE = (MLA_NOPE + MLA_ROPE) ** -0.5 * LOG2E


def _mla_latents(h_ref, g_ref, wd_ref, qn_ref, kvn_ref):
    xn = _rms(h_ref[...], g_ref[...]).astype(BF16)
    a = _dot(xn, wd_ref[...])
    cq = _rms(a[:, :MLA_Q_LORA], qn_ref[...]).astype(BF16)
    ckv = _rms(a[:, MLA_Q_LORA:MLA_Q_LORA + MLA_KV_LORA], kvn_ref[...]).astype(BF16)
    return cq, ckv, a[:, MLA_Q_LORA + MLA_KV_LORA:]


def _store_keys(kc_ref, ckv, wuk_ref, kr):
    for hd in range(MLA_HEADS):
        lo = hd * MLA_QK_PAD
        kc_ref[:, lo:lo + MLA_NOPE] = _dot(
            ckv, wuk_ref[:, hd * MLA_NOPE:(hd + 1) * MLA_NOPE]).astype(BF16)
        kc_ref[:, lo + MLA_NOPE:lo + MLA_QK_PAD] = kr


def _mla_proj_rows_kernel(h_ref, g_ref, wd_ref, qn_ref, kvn_ref, wuq_ref, wuk_ref, wuv_ref,
                          cos_ref, sin_ref, q_ref, kc_ref, v_ref):
    cq, ckv, kr = _mla_latents(h_ref, g_ref, wd_ref, qn_ref, kvn_ref)
    cos = cos_ref[...]
    sin = sin_ref[...]
    _store_keys(kc_ref, ckv, wuk_ref, _rope128(kr, cos, sin).astype(BF16))
    v_ref[...] = _dot(ckv, wuv_ref[...]).astype(BF16)
    for hd in range(MLA_HEADS):
        lo = hd * MLA_QK_PAD
        qh = _dot(cq, wuq_ref[:, lo:lo + MLA_QK_PAD])
        q_ref[:, lo:lo + MLA_NOPE] = (qh[:, :MLA_NOPE] * MLA_SCALE).astype(BF16)
        q_ref[:, lo + MLA_NOPE:lo + MLA_QK_PAD] = (
            _rope128(qh[:, MLA_NOPE:], cos, sin) * MLA_SCALE).astype(BF16)


def _mla_proj_cols_kernel(h_ref, g_ref, wd_ref, qn_ref, kvn_ref, wuqt_ref, wuk_ref, wuvt_ref,
                          cos_ref, sin_ref, cost_ref, sint_ref, qt_ref, kc_ref, vt_ref):
    cq, ckv, kr = _mla_latents(h_ref, g_ref, wd_ref, qn_ref, kvn_ref)
    _store_keys(kc_ref, ckv, wuk_ref, _rope128(kr, cos_ref[...], sin_ref[...]).astype(BF16))
    tm = cq.shape[0]
    c = cost_ref[...]
    s = sint_ref[...]
    half = MLA_ROPE // 2
    ones = jnp.ones((MLA_VX - MLA_V, tm), BF16)
    zeros = jnp.zeros((MLA_QK_PAD - MLA_NOPE - MLA_ROPE, tm), BF16)
    hg = 4
    for h0 in range(0, MLA_HEADS, hg):
        vt = _dot_t(wuvt_ref[h0 * MLA_V:(h0 + hg) * MLA_V, :], ckv).astype(BF16)
        qt = _dot_t(wuqt_ref[h0 * MLA_QK_PAD:(h0 + hg) * MLA_QK_PAD, :], cq)
        for k in range(hg):
            vo = (h0 + k) * MLA_VX
            vt_ref[vo:vo + MLA_V, :] = vt[k * MLA_V:(k + 1) * MLA_V]
            vt_ref[vo + MLA_V:vo + MLA_VX, :] = ones
            lo = k * MLA_QK_PAD
            qo = (h0 + k) * MLA_QK_PAD
            qt_ref[qo:qo + MLA_NOPE, :] = (qt[lo:lo + MLA_NOPE] * MLA_SCALE).astype(BF16)
            x1 = qt[lo + MLA_NOPE:lo + MLA_NOPE + half]
            x2 = qt[lo + MLA_NOPE + half:lo + MLA_NOPE + MLA_ROPE]
            qt_ref[qo + MLA_NOPE:qo + MLA_NOPE + half, :] = (
                (x1 * c - x2 * s) * MLA_SCALE).astype(BF16)
            qt_ref[qo + MLA_NOPE + half:qo + MLA_NOPE + MLA_ROPE, :] = (
                (x2 * c + x1 * s) * MLA_SCALE).astype(BF16)
            qt_ref[qo + MLA_NOPE + MLA_ROPE:qo + MLA_QK_PAD, :] = zeros


def _full_spec(a, buffers=None):
    mode = None if buffers is None else pl.Buffered(buffers)
    return pl.BlockSpec(a.shape, lambda *_: (0,) * a.ndim, pipeline_mode=mode)


def _layer_spec(w, layer):
    return pl.BlockSpec((None,) + w.shape[1:], lambda *_: (layer,) + (0,) * (w.ndim - 1))


def _mla_proj_rows(h, g, w, cos, sin):
    rows = h.shape[0]
    wd, qn, kvn, wuq, wuk, wuv = w
    args = (h, g, wd, qn, kvn, wuq, wuk, wuv, cos, sin)
    widths = (MLA_HEADS * MLA_QK_PAD, MLA_HEADS * MLA_QK_PAD, MLA_HEADS * MLA_V)
    return pl.pallas_call(
        _mla_proj_rows_kernel,
        grid=(1,),
        in_specs=[_full_spec(a) for a in args],
        out_specs=[pl.BlockSpec((rows, n), lambda i: (0, 0)) for n in widths],
        out_shape=[jax.ShapeDtypeStruct((rows, n), BF16) for n in widths],
        compiler_params=_cparams(("arbitrary",)),
        name="mla_proj_rows",
    )(*args)


def _mla_proj_cols(h, g, w, tabs, tm, seq):
    rows = h.shape[0]
    wd, qn, kvn, wuqt, wuk, wuvt = w
    cos, sin, cost, sint = tabs
    nblk = seq // tm
    row = lambda width: pl.BlockSpec((tm, width), lambda i: (i, 0))
    col = lambda height: pl.BlockSpec((height, tm), lambda i: (0, i))
    tab = pl.BlockSpec((tm, 128), lambda i: (i % nblk, 0))
    tabt = pl.BlockSpec((MLA_ROPE // 2, tm), lambda i: (0, i % nblk))
    return pl.pallas_call(
        _mla_proj_cols_kernel,
        grid=(rows // tm,),
        in_specs=[row(D_MODEL)] + [_full_spec(a, 1) for a in (g, wd, qn, kvn, wuqt, wuk, wuvt)]
        + [tab, tab, tabt, tabt],
        out_specs=[col(MLA_HEADS * MLA_QK_PAD), row(MLA_HEADS * MLA_QK_PAD), col(MLA_HEADS * MLA_VX)],
        out_shape=[jax.ShapeDtypeStruct((MLA_HEADS * MLA_QK_PAD, rows), BF16),
                   jax.ShapeDtypeStruct((rows, MLA_HEADS * MLA_QK_PAD), BF16),
                   jax.ShapeDtypeStruct((MLA_HEADS * MLA_VX, rows), BF16)],
        compiler_params=_cparams(("parallel",)),
        name="mla_proj_cols",
    )(h, g, wd, qn, kvn, wuqt, wuk, wuvt, cos, sin, cost, sint)


def _mla_attn_kernel(qt_ref, k_ref, vt_ref, qm_ref, km_ref, vm_ref, ki_ref, vti_ref,
                     o_ref, om_ref, mm_ref, lm_ref, am_ref, *, tk):
    i = pl.program_id(2)
    n_chunks = k_ref.shape[0] // tk
    qm = qm_ref[...]

    @pl.when(i == 0)
    def _():
        s = _dot_t(qm, km_ref[...])
        m = jnp.max(s, axis=-1, keepdims=True)
        p = jnp.exp2(s - m)
        mm_ref[...] = m
        lm_ref[...] = jnp.sum(p, axis=-1, keepdims=True)
        am_ref[...] = _dot(p.astype(BF16), vm_ref[...])

    n = qt_ref.shape[1] // MLA_STREAMS
    qts = [qt_ref[:, h * n:(h + 1) * n] for h in range(MLA_STREAMS)]
    score = lambda h, k: _dot(k, qts[h])
    meta = {"s": _dot_t(qm, ki_ref[...])}

    def meta_softmax():
        m_old = mm_ref[...]
        m = jnp.maximum(m_old, jnp.max(meta["s"], axis=-1, keepdims=True))
        meta["m"] = m
        meta["alpha"] = jnp.exp2(m_old - m)
        meta["p"] = jnp.exp2(meta["s"] - m)

    def meta_values():
        meta["o"] = _dot_t(meta["p"].astype(BF16), vti_ref[0:MLA_V, :])

    def meta_update():
        mm_ref[...] = meta["m"]
        lm_ref[...] = meta["alpha"] * lm_ref[...] + jnp.sum(meta["p"], axis=-1, keepdims=True)
        am_ref[...] = meta["alpha"] * am_ref[...] + meta["o"]

    stages = [(min(1, n_chunks - 1), meta_softmax), (min(2, n_chunks - 1), meta_values),
              (min(4, n_chunks - 1), meta_update)]

    pad = 128 - N_META
    vmt = jnp.concatenate([vm_ref[...].astype(F32), jnp.zeros((pad, MLA_V), F32)], axis=0).T
    vmt = jnp.concatenate([vmt, jnp.ones((MLA_VX - MLA_V, 128), F32)], axis=0).astype(BF16)
    km = km_ref[...]
    st = []
    for h in range(MLA_STREAMS):
        s = score(h, km)
        m = jnp.max(s, axis=0, keepdims=True)
        p = jnp.exp2(s - m)
        pm = jnp.concatenate([p.astype(BF16), jnp.zeros((pad, n), BF16)], axis=0)
        st.append({"m": m, "o_prev": _dot(vmt, pm), "acc": jnp.zeros((MLA_VX, n), F32),
                   "s_next": score(h, k_ref[0:tk, :])})
    for c in range(n_chunks):
        for h in range(MLA_STREAMS):
            t = st[h]
            s = t["s_next"]
            if c + 1 < n_chunks:
                t["s_next"] = score(h, k_ref[(c + 1) * tk:(c + 2) * tk, :])
            m_new = jnp.maximum(t["m"], jnp.max(s, axis=0, keepdims=True))
            alpha = jnp.exp2(t["m"] - m_new)
            p = jnp.exp2(s - m_new)
            o_c = _dot(vt_ref[:, c * tk:(c + 1) * tk], p.astype(BF16))
            t["acc"] = alpha * (t["acc"] + t["o_prev"])
            t["o_prev"] = o_c
            t["m"] = m_new
        for at, stage in stages:
            if at == c:
                stage()
    for h in range(MLA_STREAMS):
        acc = st[h]["acc"] + st[h]["o_prev"]
        o_ref[h * n:(h + 1) * n, :] = (
            acc[:MLA_V] * (1.0 / acc[MLA_V:MLA_V + 1])).T.astype(BF16)

    @pl.when(i == pl.num_programs(2) - 1)
    def _():
        om_ref[...] = (am_ref[...] / lm_ref[...]).astype(BF16)


def _mla_attn(qt, kc, vt, q_meta, kc_meta, v_meta, batch, seq0, tq, tk):
    rows = kc.shape[0]
    seq = rows // batch
    nq = seq // tq
    meta = lambda width: pl.BlockSpec((N_META, width), lambda s, h, i: (seq0 + s, h))
    return pl.pallas_call(
        functools.partial(_mla_attn_kernel, tk=tk),
        grid=(batch, MLA_HEADS, nq),
        in_specs=[
            pl.BlockSpec((MLA_QK_PAD, tq), lambda s, h, i: (h, s * nq + i)),
            pl.BlockSpec((seq, MLA_QK_PAD), lambda s, h, i: (s, h)),
            pl.BlockSpec((MLA_VX, seq), lambda s, h, i: (h, s)),
            meta(MLA_QK_PAD), meta(MLA_QK_PAD), meta(MLA_V),
            pl.BlockSpec((tq, MLA_QK_PAD), lambda s, h, i: (s * nq + i, h)),
            pl.BlockSpec((MLA_VX, tq), lambda s, h, i: (h, s * nq + i)),
        ],
        out_specs=[pl.BlockSpec((tq, MLA_V), lambda s, h, i: (s * nq + i, h)),
                   pl.BlockSpec((N_META, MLA_V), lambda s, h, i: (s, h))],
        out_shape=[jax.ShapeDtypeStruct((rows, MLA_HEADS * MLA_V), BF16),
                   jax.ShapeDtypeStruct((batch * N_META, MLA_HEADS * MLA_V), BF16)],
        scratch_shapes=[pltpu.VMEM((N_META, 1), F32), pltpu.VMEM((N_META, 1), F32),
                        pltpu.VMEM((N_META, MLA_V), F32)],
        compiler_params=_cparams(("parallel", "parallel", "arbitrary")),
        name="mla_attn",
    )(qt, kc, vt, q_meta, kc_meta, v_meta, kc, vt)


def _swa_proj_kernel(h_ref, g_ref, w_ref, q_ref, k_ref, v_ref):
    xn = _rms(h_ref[...], g_ref[...]).astype(BF16)
    dq = SWA_HEADS * SWA_HEAD_DIM
    dk = SWA_KV_HEADS * SWA_HEAD_DIM
    scale = SWA_HEAD_DIM ** -0.5
    for kvh in range(SWA_KV_HEADS):
        lo = kvh * SWA_GROUP * SWA_HEAD_DIM
        qk = _dot(xn, w_ref[:, lo:lo + SWA_GROUP * SWA_HEAD_DIM]) * scale
        for g in range(SWA_GROUP):
            q_ref[kvh * SWA_GROUP + g] = qk[:, g * SWA_HEAD_DIM:(g + 1) * SWA_HEAD_DIM].astype(BF16)
    kv = _dot(xn, w_ref[:, dq:dq + 2 * dk])
    for kvh in range(SWA_KV_HEADS):
        k_ref[kvh] = kv[:, kvh * SWA_HEAD_DIM:(kvh + 1) * SWA_HEAD_DIM].astype(BF16)
        v_ref[kvh] = kv[:, dk + kvh * SWA_HEAD_DIM:dk + (kvh + 1) * SWA_HEAD_DIM].astype(BF16)


def _swa_proj(h, g, w, layer, tm):
    rows = h.shape[0]
    hm = lambda n: pl.BlockSpec((n, tm, SWA_HEAD_DIM), lambda i: (0, i, 0))
    return pl.pallas_call(
        _swa_proj_kernel,
        grid=(rows // tm,),
        in_specs=[pl.BlockSpec((tm, D_MODEL), lambda i: (i, 0)),
                  _full_spec(g), _layer_spec(w, layer)],
        out_specs=[hm(SWA_HEADS), hm(SWA_KV_HEADS), hm(SWA_KV_HEADS)],
        out_shape=[jax.ShapeDtypeStruct((SWA_HEADS, rows, SWA_HEAD_DIM), BF16),
                   jax.ShapeDtypeStruct((SWA_KV_HEADS, rows, SWA_HEAD_DIM), BF16),
                   jax.ShapeDtypeStruct((SWA_KV_HEADS, rows, SWA_HEAD_DIM), BF16)],
        compiler_params=_cparams(("parallel",)),
        name="swa_proj",
    )(h, g, w)


def _swa_proj_cols_kernel(h_ref, g_ref, wqt_ref, wkv_ref, qt_ref, k_ref, v_ref):
    xn = _rms(h_ref[...], g_ref[...]).astype(BF16)
    qt = (_dot_t(wqt_ref[...], xn) * (SWA_HEAD_DIM ** -0.5 * LOG2E)).astype(BF16)
    for kvh in range(SWA_KV_HEADS):
        for g in range(SWA_GROUP):
            r0 = (kvh * SWA_GROUP + g) * SWA_HEAD_DIM
            for blk in range(qt.shape[1] // BLOCK):
                c0 = (blk * SWA_GROUP + g) * BLOCK
                qt_ref[kvh, :, c0:c0 + BLOCK] = qt[r0:r0 + SWA_HEAD_DIM, blk * BLOCK:(blk + 1) * BLOCK]
    dk = SWA_KV_HEADS * SWA_HEAD_DIM
    kv = _dot(xn, wkv_ref[...])
    for kvh in range(SWA_KV_HEADS):
        k_ref[kvh] = kv[:, kvh * SWA_HEAD_DIM:(kvh + 1) * SWA_HEAD_DIM].astype(BF16)
        v_ref[kvh] = kv[:, dk + kvh * SWA_HEAD_DIM:dk + (kvh + 1) * SWA_HEAD_DIM].astype(BF16)


def _swa_proj_cols(h, g, wqt, wkv, layer, tm):
    rows = h.shape[0]
    hm = pl.BlockSpec((SWA_KV_HEADS, tm, SWA_HEAD_DIM), lambda i: (0, i, 0))
    return pl.pallas_call(
        _swa_proj_cols_kernel,
        grid=(rows // tm,),
        in_specs=[pl.BlockSpec((tm, D_MODEL), lambda i: (i, 0)),
                  _full_spec(g), _layer_spec(wqt, layer), _layer_spec(wkv, layer)],
        out_specs=[pl.BlockSpec((SWA_KV_HEADS, SWA_HEAD_DIM, tm * SWA_GROUP), lambda i: (0, 0, i)),
                   hm, hm],
        out_shape=[jax.ShapeDtypeStruct((SWA_KV_HEADS, SWA_HEAD_DIM, rows * SWA_GROUP), BF16),
                   jax.ShapeDtypeStruct((SWA_KV_HEADS, rows, SWA_HEAD_DIM), BF16),
                   jax.ShapeDtypeStruct((SWA_KV_HEADS, rows, SWA_HEAD_DIM), BF16)],
        compiler_params=_cparams(("parallel",)),
        name="swa_proj_cols",
    )(h, g, wqt, wkv)


def _t5_bucket_exact(rel):
    nb = N_BUCKETS // 2
    max_exact = nb // 2
    table = []
    for n in range(int(np.abs(rel).max()) + 1):
        if n < max_exact:
            table.append(n)
        else:
            k = 0
            while 64 * 2 ** (k + 1) <= n * n:
                k += 1
            table.append(min(max_exact + k, nb - 1))
    return (np.asarray(table, np.int32)[np.abs(rel)] + np.where(rel > 0, nb, 0)).astype(np.int32)


def _swa_tile_codes(qpos, band_start):
    nq = qpos.shape[0]
    codes = np.full((nq, SWA_KEYS), -1, np.int32)
    kpos_band = N_META + band_start + np.arange(SWA_BAND)
    rel = kpos_band[None, :] - qpos[:, None]
    ok = np.abs(rel) <= WINDOW
    codes[:, :SWA_BAND] = np.where(ok, _t5_bucket_exact(rel), -1)
    rel_m = np.arange(N_META)[None, :] - qpos[:, None]
    codes[:, SWA_BAND:SWA_BAND + N_META] = _t5_bucket_exact(rel_m)
    codes[:, SINK_COL] = N_BUCKETS
    return codes


def _swa_bias_kernel(code_ref, rb_ref, sink_ref, o_ref, *, scale):
    hd = pl.program_id(1)
    code = code_ref[0]
    t = jnp.full(code.shape, NEG, F32)
    for b in range(N_BUCKETS):
        t = jnp.where(code == b, rb_ref[b, hd] * scale, t)
    t = jnp.where(code == N_BUCKETS, sink_ref[hd] * scale, t)
    o_ref[0, 0] = t


def _swa_bias(codes, rel_bias, sink, transposed):
    nv, a, b = codes.shape
    if transposed:
        out_spec = pl.BlockSpec((1, 1, a, b), lambda v, h: (v, h // SWA_GROUP, 0, h % SWA_GROUP))
        out_shape = (nv, SWA_KV_HEADS, a, SWA_GROUP * b)
    else:
        out_spec = pl.BlockSpec((1, 1, a, b), lambda v, h: (v, h, 0, 0))
        out_shape = (nv, SWA_HEADS, a, b)
    return pl.pallas_call(
        functools.partial(_swa_bias_kernel, scale=LOG2E if transposed else 1.0),
        grid=(nv, SWA_HEADS),
        in_specs=[pl.BlockSpec((1, a, b), lambda v, h: (v, 0, 0)),
                  pl.BlockSpec(memory_space=pltpu.SMEM),
                  pl.BlockSpec(memory_space=pltpu.SMEM)],
        out_specs=out_spec,
        out_shape=jax.ShapeDtypeStruct(out_shape, F32),
        compiler_params=_cparams(("parallel", "parallel")),
        name="swa_bias",
    )(codes, rel_bias, sink)


def _swa_attn_meta_kernel(q_ref, k_ref, v_ref, km_ref, vm_ref, b_ref, o_ref):
    pad = jnp.zeros((SWA_KEYS - SWA_BAND - N_META, SWA_HEAD_DIM), BF16)
    kt = jnp.concatenate([k_ref[0, 0:SWA_BAND, :], km_ref[0], pad], axis=0)
    vt = jnp.concatenate([v_ref[0, 0:SWA_BAND, :], vm_ref[0], pad], axis=0)
    q = q_ref[...].reshape(SWA_GROUP * N_META, SWA_HEAD_DIM)
    s = _dot_t(q, kt) + b_ref[0].reshape(SWA_GROUP * N_META, SWA_KEYS)
    m = jnp.max(s, axis=-1, keepdims=True)
    e = jnp.exp(s - m)
    l = jnp.sum(e, axis=-1, keepdims=True)
    o = (_dot(e.astype(BF16), vt) / l).reshape(SWA_GROUP, N_META, SWA_HEAD_DIM)
    o_ref[...] = jnp.concatenate([o[g] for g in range(SWA_GROUP)], axis=1).astype(BF16)


def _swa_attn(q_meta, k, v, k_meta, v_meta, bias, batch, seq0):
    seq = k.shape[1] // batch
    meta = lambda n: pl.BlockSpec((n, N_META, SWA_HEAD_DIM), lambda kv, s: (kv, seq0 + s, 0))
    real = pl.BlockSpec((1, seq, SWA_HEAD_DIM), lambda kv, s: (kv, s, 0))
    return pl.pallas_call(
        _swa_attn_meta_kernel,
        grid=(SWA_KV_HEADS, batch),
        in_specs=[meta(SWA_GROUP), real, real, meta(1), meta(1),
                  pl.BlockSpec((1, SWA_GROUP, N_META, SWA_KEYS), lambda kv, s: (0, kv, 0, 0))],
        out_specs=pl.BlockSpec((N_META, SWA_GROUP * SWA_HEAD_DIM), lambda kv, s: (s, kv)),
        out_shape=jax.ShapeDtypeStruct((batch * N_META, SWA_HEADS * SWA_HEAD_DIM), BF16),
        compiler_params=_cparams(("parallel", "parallel")),
        name="swa_meta_attn",
    )(q_meta, k, v, k_meta, v_meta, bias)


def _swa_attn_cols_kernel(qt_ref, k_ref, v_ref, km_ref, vm_ref, b_ref, o_ref):
    lanes = SWA_GROUP * BLOCK
    nsub = qt_ref.shape[2] // lanes
    seq = k_ref.shape[1]
    n_blocks = seq // BLOCK
    used = SWA_BAND + 2 * N_META
    kpad = jnp.zeros((used - SWA_BAND - N_META, SWA_HEAD_DIM), BF16)
    vpad = jnp.zeros((SWA_KEYS - SWA_BAND - N_META, SWA_HEAD_DIM), BF16)
    ppad = jnp.zeros((SWA_KEYS - used, lanes), BF16)
    ext = jnp.concatenate([jnp.ones((SWA_KEYS, 16), F32),
                           jnp.zeros((SWA_KEYS, 128 - SWA_HEAD_DIM - 16), F32)], axis=1)

    def window(j):
        b = pl.program_id(2) * nsub + j
        start = pl.multiple_of(jnp.clip((b - 1) * BLOCK, 0, seq - SWA_BAND), BLOCK)
        variant = jnp.where(b == 0, 1, jnp.where(b == n_blocks - 1, 2, 0))
        return start, variant

    def scores(j):
        start, variant = window(j)
        kt = jnp.concatenate([k_ref[0, pl.ds(start, SWA_BAND), :], km_ref[0], kpad], axis=0)
        return (_dot(kt, qt_ref[0, :, j * lanes:(j + 1) * lanes])
                + b_ref[variant, 0, 0:used, :])

    s_next = scores(0)
    for j in range(nsub):
        s = s_next
        if j + 1 < nsub:
            s_next = scores(j + 1)
        start, _ = window(j)
        vrows = jnp.concatenate([v_ref[0, pl.ds(start, SWA_BAND), :], vm_ref[0], vpad], axis=0)
        vt = jnp.concatenate([vrows.astype(F32), ext], axis=1).T.astype(BF16)
        m = jnp.max(s, axis=0, keepdims=True)
        p = jnp.concatenate([jnp.exp2(s - m).astype(BF16), ppad], axis=0)
        acc = _dot(vt, p)
        acc = acc * (1.0 / acc[SWA_HEAD_DIM:SWA_HEAD_DIM + 1])
        o = acc.T[:, :SWA_HEAD_DIM].reshape(SWA_GROUP, BLOCK, SWA_HEAD_DIM)
        o_ref[j * BLOCK:(j + 1) * BLOCK, :] = jnp.concatenate(
            [o[g] for g in range(SWA_GROUP)], axis=1).astype(BF16)


def _swa_attn_cols(qt, k, v, k_meta, v_meta, bias, batch, seq0, tq):
    rows = k.shape[1]
    seq = rows // batch
    nq = seq // tq
    kv_spec = pl.BlockSpec((1, seq, SWA_HEAD_DIM), lambda kv, s, i: (kv, s, 0))
    meta_spec = pl.BlockSpec((1, N_META, SWA_HEAD_DIM), lambda kv, s, i: (kv, seq0 + s, 0))
    return pl.pallas_call(
        _swa_attn_cols_kernel,
        grid=(SWA_KV_HEADS, batch, nq),
        in_specs=[
            pl.BlockSpec((1, SWA_HEAD_DIM, tq * SWA_GROUP), lambda kv, s, i: (kv, 0, s * nq + i)),
            kv_spec, kv_spec, meta_spec, meta_spec,
            pl.BlockSpec((bias.shape[0], 1) + bias.shape[2:], lambda kv, s, i: (0, kv, 0, 0)),
        ],
        out_specs=pl.BlockSpec((tq, SWA_GROUP * SWA_HEAD_DIM), lambda kv, s, i: (s * nq + i, kv)),
        out_shape=jax.ShapeDtypeStruct((rows, SWA_HEADS * SWA_HEAD_DIM), BF16),
        compiler_params=_cparams(("parallel", "parallel", "arbitrary")),
        name="swa_attn_cols",
    )(qt, k, v, k_meta, v_meta, bias)


def _attn_out_kernel(o_ref, w_ref, g_ref, h_ref, out_ref):
    m = _dot(o_ref[...], w_ref[...])
    out_ref[...] = h_ref[...] + _rms(m, g_ref[...])


def _attn_out(o, w, layer, g, h, tm):
    rows = h.shape[0]
    return pl.pallas_call(
        _attn_out_kernel,
        grid=(rows // tm,),
        in_specs=[pl.BlockSpec((tm, o.shape[1]), lambda i: (i, 0)),
                  _layer_spec(w, layer), _full_spec(g),
                  pl.BlockSpec((tm, D_MODEL), lambda i: (i, 0))],
        out_specs=pl.BlockSpec((tm, D_MODEL), lambda i: (i, 0)),
        out_shape=jax.ShapeDtypeStruct((rows, D_MODEL), F32),
        compiler_params=_cparams(("parallel",)),
        name="attn_out",
    )(o, w, g, h)


def _mlp_kernel(h_ref, gpre_ref, win_ref, wout_ref, gpost_ref, out_ref, xn_ref):
    j = pl.program_id(1)
    last = pl.num_programs(1) - 1

    n_sub = win_ref.shape[1] // MLP_SUB

    def ffn(xn, c):
        u = jnp.maximum(_dot(xn, win_ref[:, c * MLP_SUB:(c + 1) * MLP_SUB]), 0.0)
        return _dot((u * u).astype(BF16), wout_ref[c * MLP_SUB:(c + 1) * MLP_SUB, :])

    @pl.when(j == 0)
    def _():
        xn = _rms(h_ref[...], gpre_ref[...]).astype(BF16)
        xn_ref[...] = xn
        out_ref[...] = ffn(xn, 0)
        for c in range(1, n_sub):
            out_ref[...] += ffn(xn, c)

    @pl.when(jnp.logical_and(j > 0, j < last))
    def _():
        for c in range(n_sub):
            out_ref[...] += ffn(xn_ref[...], c)

    @pl.when(j == last)
    def _():
        for c in range(n_sub - 1):
            out_ref[...] += ffn(xn_ref[...], c)
        f = out_ref[...] + ffn(xn_ref[...], n_sub - 1)
        out_ref[...] = h_ref[...] + _rms(f, gpost_ref[...])


def _mlp(h, gpre, w_in, w_out, layer, gpost, tm, tf):
    rows = h.shape[0]
    return pl.pallas_call(
        _mlp_kernel,
        grid=(rows // tm, D_FF // tf),
        in_specs=[pl.BlockSpec((tm, D_MODEL), lambda i, j: (i, 0)),
                  pl.BlockSpec(gpre.shape, lambda i, j: (0, 0)),
                  pl.BlockSpec((None, D_MODEL, tf), lambda i, j: (layer, 0, j)),
                  pl.BlockSpec((None, tf, D_MODEL), lambda i, j: (layer, j, 0)),
                  pl.BlockSpec(gpost.shape, lambda i, j: (0, 0))],
        out_specs=pl.BlockSpec((tm, D_MODEL), lambda i, j: (i, 0)),
        out_shape=jax.ShapeDtypeStruct((rows, D_MODEL), F32),
        scratch_shapes=[pltpu.VMEM((tm, D_MODEL), BF16)],
        compiler_params=_cparams(("parallel", "arbitrary")),
        name="mlp",
    )(h, gpre, w_in, w_out, gpost)


def _rope_dup(w):
    return jnp.concatenate([w, w], axis=-1)


def _prep_mla(w_dq, q_norm, w_uq, w_dkv, kv_norm, w_ukv):
    wd = jnp.concatenate([w_dq, w_dkv[:, :MLA_KV_LORA], _rope_dup(w_dkv[:, MLA_KV_LORA:])], axis=1)
    uq = w_uq.reshape(MLA_Q_LORA, MLA_HEADS, MLA_NOPE + MLA_ROPE)
    wuq = jnp.concatenate([uq[..., :MLA_NOPE], _rope_dup(uq[..., MLA_NOPE:])], axis=-1)
    wuq = wuq.reshape(MLA_Q_LORA, MLA_HEADS * MLA_QK_PAD)
    ukv = w_ukv.reshape(MLA_KV_LORA, MLA_HEADS, MLA_NOPE + MLA_V)
    wuk = ukv[..., :MLA_NOPE].reshape(MLA_KV_LORA, MLA_HEADS * MLA_NOPE)
    wuv = ukv[..., MLA_NOPE:].reshape(MLA_KV_LORA, MLA_HEADS * MLA_V)
    wd, wuq, wuk, wuv = (t.astype(BF16) for t in (wd, wuq, wuk, wuv))
    rows = (wd, q_norm[None], kv_norm[None], wuq, wuk, wuv)
    cols = (wd, q_norm[None], kv_norm[None], wuq.T, wuk, wuv.T)
    return rows, cols


def _rope_tables(pos):
    half = MLA_ROPE // 2
    inv = ROPE_THETA ** (-(jnp.arange(half, dtype=F32) / half))
    ang = pos.astype(F32)[:, None] * inv[None, :]
    cos, sin = jnp.cos(ang), jnp.sin(ang)
    zero = jnp.zeros((pos.shape[0], 128 - MLA_ROPE), F32)
    return (jnp.concatenate([cos, cos, zero], axis=1), jnp.concatenate([-sin, sin, zero], axis=1),
            cos.T, sin.T)


def _tiles(rows, seq):
    return {"proj": 512, "attn_out": 512, "mlp": min(1024, rows), "mlp_ff": 1024,
            "mla_queries": min(4096, seq), "mla_keys": 512, "swa_queries": min(4096, seq)}


def _trunks(xs, meta_tokens, p):
    seq = xs[0].shape[1]
    batches = [x.shape[0] for x in xs]
    seq0 = [sum(batches[:g]) for g in range(len(xs))]
    total = sum(batches)
    hs = [x.reshape(b * seq, D_MODEL) for x, b in zip(xs, batches)]
    rows_m = total * N_META
    hm = jnp.broadcast_to(meta_tokens[None], (total, N_META, D_MODEL)).reshape(rows_m, D_MODEL)
    tabs_r = _rope_tables(N_META + jnp.arange(seq))
    cos_m, sin_m = (jnp.tile(t, (total, 1)) for t in _rope_tables(jnp.arange(N_META))[:2])
    for i in range(DEPTH):
        j = i // 2
        g_pre = p["norm_mix_pre"][i][None]
        g_post = p["norm_mix_post"][i][None]
        os, oms = [], []
        if i % 2 == 0:
            w_rows, w_cols = p["mla"][j]
            q_m, kc_m, v_m = _mla_proj_rows(hm, g_pre, w_rows, cos_m, sin_m)
            for h, b, s0 in zip(hs, batches, seq0):
                t = _tiles(h.shape[0], seq)
                qt, kc, vt = _mla_proj_cols(h, g_pre, w_cols, tabs_r, t["proj"], seq)
                o, o_m = _mla_attn(qt, kc, vt, q_m, kc_m, v_m, b, s0, t["mla_queries"],
                                   t["mla_keys"])
                os.append(o)
                oms.append(o_m)
            w_o = p["mla_w_o"]
        else:
            q_m, k_m, v_m = _swa_proj(hm, g_pre, p["swa_w_qkv"], j, rows_m)
            for h, b, s0 in zip(hs, batches, seq0):
                t = _tiles(h.shape[0], seq)
                qt, k, v = _swa_proj_cols(h, g_pre, p["swa_wqt"], p["swa_wkv"], j, t["proj"])
                os.append(_swa_attn_cols(qt, k, v, k_m, v_m, p["swa_bias"][j], b, s0,
                                         t["swa_queries"]))
                oms.append(_swa_attn(q_m, k, v, k_m, v_m, p["swa_bias_meta"][j], b, s0))
            w_o = p["swa_w_o"]
        mlp_args = (p["norm_mlp_pre"][i][None], p["mlp_w_in"], p["mlp_w_out"], i,
                    p["norm_mlp_post"][i][None])
        tiles = [_tiles(h.shape[0], seq) for h in hs]
        hs = [_attn_out(o, w_o, j, g_post, h, t["attn_out"]) for o, h, t in zip(os, hs, tiles)]
        hs = [_mlp(h, *mlp_args, t["mlp"], t["mlp_ff"]) for h, t in zip(hs, tiles)]
        hm = _attn_out(jnp.concatenate(oms, axis=0), w_o, j, g_post, hm, rows_m)
        hm = _mlp(hm, *mlp_args, rows_m, MLP_SUB)
    return tuple(h.reshape(b, seq, D_MODEL) for h, b in zip(hs, batches))


def kernel(x_prompt, x_sample, meta_tokens, rel_bias, mla_w_dq, mla_q_norm, mla_w_uq, mla_w_dkv, mla_kv_norm, mla_w_ukv, mla_w_o, swa_w_qkv, swa_w_o, swa_sink, mlp_w_in, mlp_w_out, norm_mix_pre, norm_mix_post, norm_mlp_pre, norm_mlp_post):
    seq = x_prompt.shape[1]
    assert x_sample.shape[1] == seq and seq % 512 == 0
    n_mla = mla_w_dq.shape[0]
    n_swa = swa_w_qkv.shape[0]
    dq = SWA_HEADS * SWA_HEAD_DIM
    qpos = N_META + np.arange(BLOCK)
    codes = np.stack([
        _swa_tile_codes(qpos + BLOCK, 0),
        _swa_tile_codes(qpos, 0),
        _swa_tile_codes(qpos + 2 * BLOCK, 0),
    ])
    codes_meta = _swa_tile_codes(np.arange(N_META), 0)[None]
    p = {
        "mla": [_prep_mla(mla_w_dq[j], mla_q_norm[j], mla_w_uq[j], mla_w_dkv[j], mla_kv_norm[j],
                          mla_w_ukv[j]) for j in range(n_mla)],
        "mla_w_o": mla_w_o.astype(BF16),
        "swa_w_qkv": swa_w_qkv.astype(BF16),
        "swa_wqt": jnp.swapaxes(swa_w_qkv[:, :, :dq], 1, 2).astype(BF16),
        "swa_wkv": swa_w_qkv[:, :, dq:].astype(BF16),
        "swa_w_o": swa_w_o.astype(BF16),
        "swa_bias": [_swa_bias(jnp.asarray(codes.transpose(0, 2, 1)), rel_bias, swa_sink[j], True)
                     for j in range(n_swa)],
        "swa_bias_meta": [_swa_bias(jnp.asarray(codes_meta), rel_bias, swa_sink[j], False)
                          for j in range(n_swa)],
        "mlp_w_in": mlp_w_in.astype(BF16),
        "mlp_w_out": mlp_w_out.astype(BF16),
        "norm_mix_pre": norm_mix_pre, "norm_mix_post": norm_mix_post,
        "norm_mlp_pre": norm_mlp_pre, "norm_mlp_post": norm_mlp_post,
    }
    return _trunks((x_prompt, x_sample), meta_tokens, p)
```
